```python
import math
import jax, jax.numpy as jnp
from jax import lax
import numpy as np

D_MODEL = 2048
BATCH = 1
SEQ = 8192
DEPTH = 1

CHUNK = 64
N_META = 16
META_PAD = (-N_META) % CHUNK
Q_BLOCK = 128
PAD_CHUNK = 2 ** 30
M_HEADS = 4
M_DQK = 128
M_DV = 256
M_QK = M_HEADS * M_DQK
M_V = M_HEADS * M_DV
CONV_W = 4
A_HEADS = 8
A_DH = 64
A_DV = 2 * A_DH
A_QK = A_HEADS * 2 * A_DH
A_V = A_HEADS * A_DV
ROT_DIM = A_DH // 4
ROPE_THETA = 500000.0
D_FF = 5632
EPS = 1e-6
IN_SPLITS = (M_QK, M_QK, M_V, M_V, M_HEADS, M_HEADS, A_QK, A_QK, A_V, D_MODEL, D_MODEL)
N_IN = sum(IN_SPLITS)
F_GATE_OFFSET = 2 * M_QK + 2 * M_V + M_HEADS

kernel_name = "hybrid_mlstm_diffattn_macaron_block"


def rmsnorm(x, g):
    xf = x.astype(jnp.float32)
    xf = xf * lax.rsqrt(jnp.mean(xf * xf, axis=-1, keepdims=True) + EPS)
    return xf.astype(x.dtype) * g


def head_rmsnorm(t, g):
    H, d = t.shape[-2], t.shape[-1]
    return rmsnorm(t, g.reshape(H, d))


def swiglu(x, w_gate, w_up, w_down):
    return (jax.nn.silu(x @ w_gate) * (x @ w_up)) @ w_down


def causal_depthwise_conv(x, w, b):
    K, C = w.shape
    y = lax.conv_general_dilated(x, w[:, None, :].astype(x.dtype), window_strides=(1,),
                                 padding=[(K - 1, 0)], dimension_numbers=('NWC', 'WIO', 'NWC'),
                                 feature_group_count=C)
    return y + b


def chunk_ids(L):
    p = jnp.arange(L, dtype=jnp.int32)
    return jnp.where(p < N_META, 0, 1 + (p - N_META) // CHUNK).astype(jnp.int32)


def rope_tables(L):
    inv_freq = ROPE_THETA ** (-jnp.arange(0, ROT_DIM, 2, dtype=jnp.float32) / ROT_DIM)
    ang = jnp.arange(L, dtype=jnp.float32)[:, None] * inv_freq[None, :]
    return jnp.cos(ang), jnp.sin(ang)


def partial_rope(x, cos, sin):
    half = ROT_DIM // 2
    x1 = x[..., :half].astype(jnp.float32)
    x2 = x[..., half:ROT_DIM].astype(jnp.float32)
    r1 = x1 * cos - x2 * sin
    r2 = x2 * cos + x1 * sin
    return jnp.concatenate([r1.astype(x.dtype), r2.astype(x.dtype), x[..., ROT_DIM:]], axis=-1)


def mlstm_chunkwise(q, k, v, log_i, log_f):
    out_dtype = v.dtype
    f32 = jnp.float32
    q, k, v = q.astype(f32), k.astype(f32), v.astype(f32)
    log_i, log_f = log_i.astype(f32), log_f.astype(f32)
    B, H, L, dk = q.shape
    dv = v.shape[-1]
    pw = ((0, 0), (0, 0), (META_PAD, 0))
    q = jnp.pad(q, pw + ((0, 0),))
    k = jnp.pad(k, pw + ((0, 0),))
    v = jnp.pad(v, pw + ((0, 0),))
    log_i = jnp.pad(log_i, pw, constant_values=-jnp.inf)
    log_f = jnp.pad(log_f, pw)
    Lm = L + META_PAD
    nc = Lm // CHUNK
    to_chunks = lambda t: jnp.moveaxis(t.reshape(B, H, nc, CHUNK, *t.shape[3:]), 2, 0)
    qc, kc, vc, lic = to_chunks(q), to_chunks(k), to_chunks(v), to_chunks(log_i)
    bc = jnp.cumsum(to_chunks(log_f), axis=-1)
    tril = jnp.tril(jnp.ones((CHUNK, CHUNK), dtype=bool))

    def step(carry, inp):
        C, n, m = carry
        qi, ki, vi, li, bi = inp
        g = bi[..., -1]
        logD = jnp.where(tril, bi[..., :, None] - bi[..., None, :] + li[..., None, :], -jnp.inf)
        m_inter = bi + m[..., None]
        m_row = jnp.maximum(m_inter, jnp.max(logD, axis=-1))
        w_inter = jnp.exp(m_inter - m_row)
        s = jnp.einsum('bhid,bhjd->bhij', qi, ki) * jnp.exp(logD - m_row[..., None])
        num = w_inter[..., None] * jnp.einsum('bhid,bhde->bhie', qi, C) + jnp.einsum('bhij,bhje->bhie', s, vi)
        den = w_inter * jnp.einsum('bhid,bhd->bhi', qi, n) + jnp.sum(s, axis=-1)
        h = num / jnp.maximum(jnp.abs(den), jnp.exp(-m_row))[..., None]
        log_w = g[..., None] - bi + li
        m_new = jnp.maximum(g + m, jnp.max(log_w, axis=-1))
        wk = jnp.exp(log_w - m_new[..., None])
        decay = jnp.exp(g + m - m_new)
        C_new = decay[..., None, None] * C + jnp.einsum('bhj,bhjd,bhje->bhde', wk, ki, vi)
        n_new = decay[..., None] * n + jnp.einsum('bhj,bhjd->bhd', wk, ki)
        return (C_new, n_new, m_new), h

    init = (jnp.zeros((B, H, dk, dv), f32), jnp.zeros((B, H, dk), f32), jnp.zeros((B, H), f32))
    _, hs = lax.scan(step, init, (qc, kc, vc, lic, bc))
    hs = jnp.moveaxis(hs, 0, 2).reshape(B, H, Lm, dv)[:, :, META_PAD:]
    return hs.astype(out_dtype)


def diff_attention(q, k, v, lam, cid):
    B, H, _, L, dh = q.shape
    Lp = -(-L // Q_BLOCK) * Q_BLOCK
    pad = Lp - L
    q = jnp.pad(q, ((0, 0), (0, 0), (0, 0), (0, pad), (0, 0)))
    k = jnp.pad(k, ((0, 0), (0, 0), (0, 0), (0, pad), (0, 0)))
    v = jnp.pad(v, ((0, 0), (0, 0), (0, pad), (0, 0)))
    cidp = jnp.concatenate([cid, jnp.full((pad,), PAD_CHUNK, cid.dtype)])
    nqb = Lp // Q_BLOCK
    qb = q.reshape(B, H, 2, nqb, Q_BLOCK, dh).transpose(3, 0, 1, 2, 4, 5)
    cqb = cidp.reshape(nqb, Q_BLOCK)
    scale = dh ** -0.5

    def block(args):
        qi, ci = args
        s = jnp.einsum('bhmqd,bhmkd->bhmqk', qi, k).astype(jnp.float32) * scale
        vis = cidp[None, :] <= ci[:, None]
        p = jax.nn.softmax(jnp.where(vis, s, -jnp.inf), axis=-1)
        a = p[:, :, 0] - lam * p[:, :, 1]
        return jnp.einsum('bhqk,bhke->bhqe', a.astype(v.dtype), v)

    o = lax.map(block, (qb, cqb))
    return o.transpose(1, 2, 0, 3, 4).reshape(B, H, Lp, -1)[:, :, :L]


def lambda_init_fn(layer):
    return 0.8 - 0.6 * math.exp(-0.3 * layer)


def hybrid_mixer(u, w_in, b_in, conv_w, conv_b, m_norm_g, m_w_branch, lq1, lk1, lq2, lk2,
                 a_norm_g, a_w_branch, w_out, cos, sin, cid, lam_init):
    B, L, _ = u.shape
    z = u @ w_in + b_in
    split_idx = [int(i) for i in np.cumsum(IN_SPLITS)[:-1]]
    mq, mk, mv, mo, mi, mf, aq, ak, av, gm, ga = jnp.split(z, split_idx, axis=-1)
    qk = jax.nn.silu(causal_depthwise_conv(jnp.concatenate([mq, mk], axis=-1), conv_w, conv_b))
    mq, mk = jnp.split(qk, 2, axis=-1)
    q = mq.reshape(B, L, M_HEADS, M_DQK).transpose(0, 2, 1, 3)
    k = mk.reshape(B, L, M_HEADS, M_DQK).transpose(0, 2, 1, 3) * (M_DQK ** -0.5)
    v = mv.reshape(B, L, M_HEADS, M_DV).transpose(0, 2, 1, 3)
    log_i = mi.transpose(0, 2, 1)
    log_f = jax.nn.log_sigmoid(mf.astype(jnp.float32)).transpose(0, 2, 1)
    hm = mlstm_chunkwise(q, k, v, log_i, log_f)
    hm = head_rmsnorm(hm.transpose(0, 2, 1, 3), m_norm_g).reshape(B, L, M_V)
    br_m = (jax.nn.sigmoid(mo) * hm) @ m_w_branch
    aq = partial_rope(aq.reshape(B, L, A_HEADS, 2, A_DH).transpose(0, 2, 3, 1, 4), cos, sin)
    ak = partial_rope(ak.reshape(B, L, A_HEADS, 2, A_DH).transpose(0, 2, 3, 1, 4), cos, sin)
    av = av.reshape(B, L, A_HEADS, A_DV).transpose(0, 2, 1, 3)
    f32 = jnp.float32
    lam = (jnp.exp(jnp.sum(lq1.astype(f32) * lk1.astype(f32)))
           - jnp.exp(jnp.sum(lq2.astype(f32) * lk2.astype(f32))) + lam_init)
    ha = diff_attention(aq, ak, av, lam, cid)
    ha = (head_rmsnorm(ha.transpose(0, 2, 1, 3), a_norm_g) * (1.0 - lam_init)).reshape(B, L, A_V)
    br_a = ha @ a_w_branch
    y = jax.nn.sigmoid(gm) * br_m + jax.nn.sigmoid(ga) * br_a
    return y @ w_out


def setup_inputs(seed: int = 0) -> dict:
    key = jax.random.key(seed)
    ks = jax.random.split(key, 32)
    f32 = jnp.float32
    nrm = lambda k, shape, scale: jax.random.normal(k, shape, f32) * scale
    gain = lambda k, shape: 1.0 + 0.02 * jax.random.normal(k, shape, f32)
    b_in = nrm(ks[11], (DEPTH, N_IN), 0.02)
    f_bias = jnp.linspace(3.0, 6.0, M_HEADS, dtype=f32)
    b_in = b_in.at[:, F_GATE_OFFSET:F_GATE_OFFSET + M_HEADS].add(f_bias)
    return {
        "x": nrm(ks[0], (BATCH, SEQ, D_MODEL), 1.0),
        "meta": nrm(ks[1], (N_META, D_MODEL), 1.0),
        "ffn1_pre_g": gain(ks[2], (DEPTH, D_MODEL)),
        "ffn1_post_g": gain(ks[3], (DEPTH, D_MODEL)),
        "ffn1_w_gate": nrm(ks[4], (DEPTH, D_MODEL, D_FF), D_MODEL ** -0.5),
        "ffn1_w_up": nrm(ks[5], (DEPTH, D_MODEL, D_FF), D_MODEL ** -0.5),
        "ffn1_w_down": nrm(ks[6], (DEPTH, D_FF, D_MODEL), D_FF ** -0.5),
        "mix_pre_g": gain(ks[7], (DEPTH, D_MODEL)),
        "mix_post_g": gain(ks[8], (DEPTH, D_MODEL)),
        "w_in": nrm(ks[9], (DEPTH, D_MODEL, N_IN), D_MODEL ** -0.5),
        "b_in": b_in,
        "m_conv_w": nrm(ks[12], (DEPTH, CONV_W, 2 * M_QK), CONV_W ** -0.5),
        "m_conv_b": nrm(ks[13], (DEPTH, 2 * M_QK), 0.02),
        "m_norm_g": gain(ks[14], (DEPTH, M_V)),
        "m_w_branch": nrm(ks[15], (DEPTH, M_V, D_MODEL), M_V ** -0.5),
        "a_lambda_q1": nrm(ks[16], (DEPTH, A_DH), 0.1),
        "a_lambda_k1": nrm(ks[17], (DEPTH, A_DH), 0.1),
        "a_lambda_q2": nrm(ks[18], (DEPTH, A_DH), 0.1),
        "a_lambda_k2": nrm(ks[19], (DEPTH, A_DH), 0.1),
        "a_norm_g": gain(ks[20], (DEPTH, A_V)),
        "a_w_branch": nrm(ks[21], (DEPTH, A_V, D_MODEL), A_V ** -0.5),
        "w_out": nrm(ks[22], (DEPTH, D_MODEL, D_MODEL), D_MODEL ** -0.5),
        "ffn2_pre_g": gain(ks[23], (DEPTH, D_MODEL)),
        "ffn2_post_g": gain(ks[24], (DEPTH, D_MODEL)),
        "ffn2_w_gate": nrm(ks[25], (DEPTH, D_MODEL, D_FF), D_MODEL ** -0.5),
        "ffn2_w_up": nrm(ks[26], (DEPTH, D_MODEL, D_FF), D_MODEL ** -0.5),
        "ffn2_w_down": nrm(ks[27], (DEPTH, D_FF, D_MODEL), D_FF ** -0.5),
    }


def reference(x, meta, ffn1_pre_g, ffn1_post_g, ffn1_w_gate, ffn1_w_up, ffn1_w_down,
              mix_pre_g, mix_post_g, w_in, b_in, m_conv_w, m_conv_b, m_norm_g, m_w_branch,
              a_lambda_q1, a_lambda_k1, a_lambda_q2, a_lambda_k2, a_norm_g, a_w_branch, w_out,
              ffn2_pre_g, ffn2_post_g, ffn2_w_gate, ffn2_w_up, ffn2_w_down):
    B = x.shape[0]
    h = jnp.concatenate([jnp.broadcast_to(meta[None].astype(x.dtype), (B, N_META, D_MODEL)), x], axis=1)
    L = h.shape[1]
    cid = chunk_ids(L)
    cos, sin = rope_tables(L)
    for l in range(DEPTH):
        u = rmsnorm(h, ffn1_pre_g[l])
        h = h + 0.5 * rmsnorm(swiglu(u, ffn1_w_gate[l], ffn1_w_up[l], ffn1_w_down[l]), ffn1_post_g[l])
        u = rmsnorm(h, mix_pre_g[l])
        y = hybrid_mixer(u, w_in[l], b_in[l], m_conv_w[l], m_conv_b[l], m_norm_g[l], m_w_branch[l],
                         a_lambda_q1[l], a_lambda_k1[l], a_lambda_q2[l], a_lambda_k2[l],
                         a_norm_g[l], a_w_branch[l], w_out[l], cos, sin, cid, lambda_init_fn(l))
        h = h + rmsnorm(y, mix_post_g[l])
        u = rmsnorm(h, ffn2_pre_g[l])
        h = h + 0.5 * rmsnorm(swiglu(u, ffn2_w_gate[l], ffn2_w_up[l], ffn2_w_down[l]), ffn2_post_g[l])
    return h[:, N_META:]
```

```python
import functools

import jax
import jax.numpy as jnp
from jax import lax
from jax.experimental import pallas as pl
from jax.experimental.pallas import tpu as pltpu

F32 = jnp.float32
BF16 = jnp.bfloat16

EPS = 1e-6
N_META = 16
M_HEADS = 4
M_DQK = 128
M_DV = 256
M_QK = M_HEADS * M_DQK
M_V = M_HEADS * M_DV
CONV_W = 4
A_HEADS = 8
A_DH = 64
A_DV = 2 * A_DH
A_QK = A_HEADS * 2 * A_DH
A_V = A_HEADS * A_DV
ROT_DIM = A_DH // 4
ROPE_THETA = 500000.0
LAMBDA_INIT = 0.8 - 0.6 * 1.0

LANES = 128
GATE_OFFSET = 2 * M_QK + 2 * M_V
N_GATES = 2 * M_HEADS
COL_QK = 0
COL_MV = 2 * M_QK
COL_MO = COL_MV + M_V
COL_AQ = COL_MO + M_V
COL_AK = COL_AQ + A_QK
COL_AV = COL_AK + A_QK
COL_GM = COL_AV + A_V

VMEM_LIMIT = 56 * 1024 * 1024


def _params(sem):
    return pltpu.CompilerParams(dimension_semantics=sem, vmem_limit_bytes=VMEM_LIMIT)


def _rms(x):
    return x * lax.rsqrt(jnp.mean(x * x, axis=-1, keepdims=True) + EPS)


def _ffn_kernel(h_ref, gpre_ref, gpost_ref, wg_ref, wu_ref, wd_ref, o_ref, u_scr, acc_scr):
    f = pl.program_id(1)

    @pl.when(f == 0)
    def _():
        u_scr[...] = (_rms(h_ref[...]) * gpre_ref[...]).astype(BF16)
        acc_scr[...] = jnp.zeros_like(acc_scr)

    u = u_scr[...]
    g = jnp.dot(u, wg_ref[...], preferred_element_type=F32)
    up = jnp.dot(u, wu_ref[...], preferred_element_type=F32)
    a = (g * jax.nn.sigmoid(g) * up).astype(BF16)
    acc_scr[...] += jnp.dot(a, wd_ref[...], preferred_element_type=F32)

    @pl.when(f == pl.num_programs(1) - 1)
    def _():
        o_ref[...] = h_ref[...] + 0.5 * (_rms(acc_scr[...]) * gpost_ref[...])


def _ffn(h, g_pre, g_post, w_gate, w_up, w_down, *, tm, tf):
    rows, d = h.shape
    d_ff = w_gate.shape[1]
    tm = min(tm, rows)
    return pl.pallas_call(
        _ffn_kernel,
        out_shape=jax.ShapeDtypeStruct((rows, d), F32),
        grid=(rows // tm, d_ff // tf),
        in_specs=[
            pl.BlockSpec((tm, d), lambda i, f: (i, 0)),
            pl.BlockSpec((1, d), lambda i, f: (0, 0)),
            pl.BlockSpec((1, d), lambda i, f: (0, 0)),
            pl.BlockSpec((d, tf), lambda i, f: (0, f)),
            pl.BlockSpec((d, tf), lambda i, f: (0, f)),
            pl.BlockSpec((tf, d), lambda i, f: (f, 0)),
        ],
        out_specs=pl.BlockSpec((tm, d), lambda i, f: (i, 0)),
        scratch_shapes=[pltpu.VMEM((tm, d), BF16), pltpu.VMEM((tm, d), F32)],
        compiler_params=_params(("parallel", "arbitrary")),
        name="ffn",
    )(h, g_pre, g_post, w_gate, w_up, w_down)


def _rope(z, c, sa, sb):
    outs = []
    for grp in range(z.shape[1] // LANES):
        x = z[:, grp * LANES:(grp + 1) * LANES]
        outs.append(x * c + pltpu.roll(x, LANES - ROT_DIM // 2, 1) * sa
                    + pltpu.roll(x, ROT_DIM // 2, 1) * sb)
    return jnp.concatenate(outs, axis=1)


def _inproj_kernel(h_ref, g_ref, w_ref, b_ref, wgate_ref, bgate_ref, c_ref, sa_ref, sb_ref,
                   z_ref, gate_ref, u_scr, *, tn):
    j = pl.program_id(1)

    @pl.when(j == 0)
    def _():
        u = (_rms(h_ref[...]) * g_ref[...]).astype(BF16)
        u_scr[...] = u
        gate_ref[...] = jnp.dot(u, wgate_ref[...], preferred_element_type=F32) + bgate_ref[...]

    z = jnp.dot(u_scr[...], w_ref[...], preferred_element_type=F32) + b_ref[...]
    is_q = j == COL_AQ // tn
    is_k = j == COL_AK // tn

    @pl.when(is_q)
    def _():
        z_ref[...] = (_rope(z, c_ref[...], sa_ref[...], sb_ref[...]) * (A_DH ** -0.5)).astype(BF16)

    @pl.when(is_k)
    def _():
        z_ref[...] = _rope(z, c_ref[...], sa_ref[...], sb_ref[...]).astype(BF16)

    @pl.when(jnp.logical_not(jnp.logical_or(is_q, is_k)))
    def _():
        z_ref[...] = z.astype(BF16)


def _inproj(h, g, w_main, b_main, w_gate, b_gate, rope_c, rope_sa, rope_sb, *, tm):
    rows, d = h.shape
    n = w_main.shape[1]
    tn = A_QK
    tm = min(tm, rows)
    return pl.pallas_call(
        functools.partial(_inproj_kernel, tn=tn),
        out_shape=(jax.ShapeDtypeStruct((rows, n), BF16),
                   jax.ShapeDtypeStruct((rows, LANES), F32)),
        grid=(rows // tm, n // tn),
        in_specs=[
            pl.BlockSpec((tm, d), lambda i, j: (i, 0)),
            pl.BlockSpec((1, d), lambda i, j: (0, 0)),
            pl.BlockSpec((d, tn), lambda i, j: (0, j)),
            pl.BlockSpec((1, tn), lambda i, j: (0, j)),
            pl.BlockSpec((d, LANES), lambda i, j: (0, 0)),
            pl.BlockSpec((1, LANES), lambda i, j: (0, 0)),
            pl.BlockSpec((tm, LANES), lambda i, j: (i, 0)),
            pl.BlockSpec((tm, LANES), lambda i, j: (i, 0)),
            pl.BlockSpec((tm, LANES), lambda i, j: (i, 0)),
        ],
        out_specs=(pl.BlockSpec((tm, tn), lambda i, j: (i, j)),
                   pl.BlockSpec((tm, LANES), lambda i, j: (i, 0))),
        scratch_shapes=[pltpu.VMEM((tm, d), BF16)],
        compiler_params=_params(("parallel", "arbitrary")),
        name="inproj",
    )(h, g, w_main, b_main, w_gate, b_gate, rope_c, rope_sa, rope_sb)


CONV_PAD = 8


def _conv_silu(xbuf, n, cw_ref, cb_ref):
    y = cb_ref[...]
    for i in range(CONV_W):
        start = CONV_PAD + i - (CONV_W - 1)
        y = y + cw_ref[i:i + 1, :] * xbuf[start:start + n, :]
    return y * jax.nn.sigmoid(y)


def _log_sigmoid(x):
    return jnp.minimum(x, 0.0) - jnp.log1p(jnp.exp(-jnp.abs(x)))


def _cumsum_rows(x):
    n = x.shape[0]
    r = lax.broadcasted_iota(jnp.int32, (n, n), 0)
    c = lax.broadcasted_iota(jnp.int32, (n, n), 1)
    tril = (r >= c).astype(F32)
    return jnp.dot(tril, x, preferred_element_type=F32, precision=lax.Precision.HIGHEST)


def _state_update(c_scr, m_scr, hd, kf, vext, li_col, b_col, g_tot, m_prev):
    log_w = (g_tot - b_col) + li_col
    m_new = jnp.maximum(g_tot + m_prev, jnp.max(log_w, axis=0, keepdims=True))
    wk = jnp.exp(log_w - m_new)
    decay = jnp.exp(g_tot + m_prev - m_new)
    kw = (kf * wk).astype(BF16)
    upd = lax.dot_general(kw, vext, (((0,), (0,)), ((), ())), preferred_element_type=F32)
    c_scr[hd] = decay * c_scr[hd] + upd
    m_scr[hd] = jnp.broadcast_to(m_new, m_scr.shape[1:])


def _mlstm_kernel(qk_ref, mv_ref, mo_ref, gt_ref, qkm_ref, mvm_ref, gtm_ref, cw_ref, cb_ref,
                  ng_ref, o_ref, c_scr, m_scr, xbuf):
    i = pl.program_id(0)
    t = qk_ref.shape[0]
    kscale = M_DQK ** -0.5

    @pl.when(i == 0)
    def _():
        nm = qkm_ref.shape[0]
        c_scr[...] = jnp.zeros_like(c_scr)
        m_scr[...] = jnp.zeros_like(m_scr)
        xbuf[0:CONV_PAD, :] = jnp.zeros((CONV_PAD, xbuf.shape[1]), F32)
        xbuf[CONV_PAD:CONV_PAD + nm, :] = qkm_ref[...].astype(F32)
        qk = _conv_silu(xbuf, nm, cw_ref, cb_ref)
        xbuf[0:CONV_PAD, :] = xbuf[nm:nm + CONV_PAD, :]
        gts = gtm_ref[...]
        b_all = _cumsum_rows(_log_sigmoid(gts))
        ones = jnp.ones((nm, LANES), BF16)
        for hd in range(M_HEADS):
            kf = qk[:, M_QK + hd * M_DQK:M_QK + (hd + 1) * M_DQK] * kscale
            vext = jnp.concatenate([mvm_ref[:, hd * M_DV:(hd + 1) * M_DV], ones], axis=1)
            b_col = b_all[:, M_HEADS + hd:M_HEADS + hd + 1]
            _state_update(c_scr, m_scr, hd, kf, vext, gts[:, hd:hd + 1], b_col,
                          b_col[nm - 1:nm, :], m_scr[hd][0:1, 0:1])

    xbuf[CONV_PAD:CONV_PAD + t, :] = qk_ref[...].astype(F32)
    qk = _conv_silu(xbuf, t, cw_ref, cb_ref)
    xbuf[0:CONV_PAD, :] = xbuf[t:t + CONV_PAD, :]

    gts = gt_ref[...]
    b_all = _cumsum_rows(_log_sigmoid(gts))
    lane = lax.broadcasted_iota(jnp.int32, gts.shape, 1)
    rowform = jnp.where(lane < M_HEADS, gts, b_all).T
    r = lax.broadcasted_iota(jnp.int32, (t, t), 0)
    c = lax.broadcasted_iota(jnp.int32, (t, t), 1)
    causal = r >= c
    ones = jnp.ones((t, LANES), BF16)

    for hd in range(M_HEADS):
        qh = qk[:, hd * M_DQK:(hd + 1) * M_DQK].astype(BF16)
        kf = qk[:, M_QK + hd * M_DQK:M_QK + (hd + 1) * M_DQK] * kscale
        kh = kf.astype(BF16)
        vext = jnp.concatenate([mv_ref[:, hd * M_DV:(hd + 1) * M_DV], ones], axis=1)
        li_col = gts[:, hd:hd + 1]
        b_col = b_all[:, M_HEADS + hd:M_HEADS + hd + 1]
        li_row = rowform[hd:hd + 1, :]
        b_row = rowform[M_HEADS + hd:M_HEADS + hd + 1, :]
        g_tot = b_col[t - 1:t, :]
        m_prev = m_scr[hd][0:1, 0:1]

        log_d = jnp.where(causal, (b_col - b_row) + li_row, -jnp.inf)
        m_inter = b_col + m_prev
        m_row = jnp.maximum(m_inter, jnp.max(log_d, axis=1, keepdims=True))
        s = lax.dot_general(qh, kh, (((1,), (1,)), ((), ())), preferred_element_type=F32)
        sd = (s * jnp.exp(log_d - m_row)).astype(BF16)
        inter = jnp.dot(qh, c_scr[hd].astype(BF16), preferred_element_type=F32)
        numden = jnp.exp(m_inter - m_row) * inter + jnp.dot(sd, vext, preferred_element_type=F32)
        den = jnp.maximum(jnp.abs(numden[:, M_DV:M_DV + 1]), jnp.exp(-m_row))
        hh = numden[:, :M_DV] / den
        hn = _rms(hh) * ng_ref[:, hd * M_DV:(hd + 1) * M_DV]
        og = jax.nn.sigmoid(mo_ref[:, hd * M_DV:(hd + 1) * M_DV].astype(F32))
        o_ref[:, hd * M_DV:(hd + 1) * M_DV] = (og * hn).astype(BF16)

        _state_update(c_scr, m_scr, hd, kf, vext, li_col, b_col, g_tot, m_prev)


def _mlstm(zx, gx, zm, gm, conv_w, conv_b, norm_g, *, t):
    rows = zx.shape[0]
    nm = zm.shape[0]
    w_qk = 2 * M_QK
    return pl.pallas_call(
        _mlstm_kernel,
        out_shape=jax.ShapeDtypeStruct((rows, M_V), BF16),
        grid=(rows // t,),
        in_specs=[
            pl.BlockSpec((t, w_qk), lambda i: (i, COL_QK // w_qk)),
            pl.BlockSpec((t, M_V), lambda i: (i, COL_MV // M_V)),
            pl.BlockSpec((t, M_V), lambda i: (i, COL_MO // M_V)),
            pl.BlockSpec((t, LANES), lambda i: (i, 0)),
            pl.BlockSpec((nm, w_qk), lambda i: (0, COL_QK // w_qk)),
            pl.BlockSpec((nm, M_V), lambda i: (0, COL_MV // M_V)),
            pl.BlockSpec((nm, LANES), lambda i: (0, 0)),
            pl.BlockSpec((CONV_W, w_qk), lambda i: (0, 0)),
            pl.BlockSpec((1, w_qk), lambda i: (0, 0)),
            pl.BlockSpec((1, M_V), lambda i: (0, 0)),
        ],
        out_specs=pl.BlockSpec((t, M_V), lambda i: (i, 0)),
        scratch_shapes=[
            pltpu.VMEM((M_HEADS, M_DQK, M_DV + LANES), F32),
            pltpu.VMEM((M_HEADS, 8, LANES), F32),
            pltpu.VMEM((t + CONV_PAD, w_qk), F32),
        ],
        compiler_params=_params(("arbitrary",)),
        name="mlstm",
    )(zx, zx, zx, gx, zm, zm, gm, conv_w, conv_b, norm_g)


ATT_CHUNK = 64


def _attn_kernel(q_ref, k_ref, v_ref, km_ref, vm_ref, lq1_ref, lk1_ref, lq2_ref, lk2_ref,
                 g_ref, o_ref):
    i = pl.program_id(1)
    tq = q_ref.shape[0]
    tk = tq
    q = q_ref[...]
    lane = lax.broadcasted_iota(jnp.int32, q.shape, 1)
    zero = jnp.zeros_like(q)
    q2 = jnp.concatenate([jnp.where(lane < A_DH, q, zero), jnp.where(lane >= A_DH, q, zero)], axis=0)
    nt = (((1,), (1,)), ((), ()))

    def scores(kb):
        return lax.dot_general(q2, kb, nt, preferred_element_type=F32)

    s = scores(km_ref[...])
    m = jnp.max(s, axis=1, keepdims=True)
    p = jnp.exp(s - m)
    l = jnp.sum(p, axis=1, keepdims=True)
    acc = jnp.dot(p.astype(BF16), vm_ref[...], preferred_element_type=F32)

    def step(s, vb, carry):
        m, l, acc = carry
        m_new = jnp.maximum(m, jnp.max(s, axis=1, keepdims=True))
        alpha = jnp.exp(m - m_new)
        p = jnp.exp(s - m_new)
        l = alpha * l + jnp.sum(p, axis=1, keepdims=True)
        acc = alpha * acc + jnp.dot(p.astype(BF16), vb, preferred_element_type=F32)
        return m_new, l, acc

    def body(j, carry):
        off = pl.multiple_of(j * tk, tk)
        return step(scores(k_ref[pl.ds(off, tk), :]), v_ref[pl.ds(off, tk), :], carry)

    carry = lax.fori_loop(0, i, body, (m, l, acc))

    off = pl.multiple_of(i * tk, tk)
    r = lax.broadcasted_iota(jnp.int32, (2 * tq, tk), 0)
    c = lax.broadcasted_iota(jnp.int32, (2 * tq, tk), 1)
    qpos = jnp.where(r >= tq, r - tq, r)
    vis = (c // ATT_CHUNK) <= (qpos // ATT_CHUNK)
    s = jnp.where(vis, scores(k_ref[pl.ds(off, tk), :]), -jnp.inf)
    m, l, acc = step(s, v_ref[pl.ds(off, tk), :], carry)

    lam = (jnp.exp(jnp.sum(lq1_ref[...] * lk1_ref[...], axis=1, keepdims=True))
           - jnp.exp(jnp.sum(lq2_ref[...] * lk2_ref[...], axis=1, keepdims=True)) + LAMBDA_INIT)
    o = acc / l
    o = o[:tq] - lam * o[tq:]
    o_ref[...] = (_rms(o) * g_ref[...] * (1.0 - LAMBDA_INIT)).astype(BF16)


def _attn(zx, zm, lq1, lk1, lq2, lk2, norm_g, *, tq):
    rows = zx.shape[0]
    nm = zm.shape[0]
    small = pl.BlockSpec((1, A_DH), lambda h, i: (0, 0))
    return pl.pallas_call(
        _attn_kernel,
        out_shape=jax.ShapeDtypeStruct((rows, A_V), BF16),
        grid=(A_HEADS, rows // tq),
        in_specs=[
            pl.BlockSpec((tq, A_DV), lambda h, i: (i, COL_AQ // A_DV + h)),
            pl.BlockSpec((rows, A_DV), lambda h, i: (0, COL_AK // A_DV + h)),
            pl.BlockSpec((rows, A_DV), lambda h, i: (0, COL_AV // A_DV + h)),
            pl.BlockSpec((nm, A_DV), lambda h, i: (0, COL_AK // A_DV + h)),
            pl.BlockSpec((nm, A_DV), lambda h, i: (0, COL_AV // A_DV + h)),
            small, small, small, small,
            pl.BlockSpec((1, A_DV), lambda h, i: (0, h)),
        ],
        out_specs=pl.BlockSpec((tq, A_DV), lambda h, i: (i, h)),
        compiler_params=_params(("parallel", "arbitrary")),
        name="diffattn",
    )(zx, zx, zx, zm, zm, lq1, lk1, lq2, lk2, norm_g)


def _merge_kernel(hm_ref, ha_ref, gm_ref, ga_ref, h_ref, mw_ref, aw_ref, wo_ref, g_ref, o_ref):
    br_m = jnp.dot(hm_ref[...], mw_ref[...], preferred_element_type=F32)
    br_a = jnp.dot(ha_ref[...], aw_ref[...], preferred_element_type=F32)
    y = (jax.nn.sigmoid(gm_ref[...].astype(F32)) * br_m
         + jax.nn.sigmoid(ga_ref[...].astype(F32)) * br_a)
    out = jnp.dot(y.astype(BF16), wo_ref[...], preferred_element_type=F32)
    o_ref[...] = h_ref[...] + _rms(out) * g_ref[...]


def _merge(hm, ha, zx, h, mw, aw, wo, g_post, *, tm):
    rows, d = h.shape
    const = lambda i: (0, 0)
    return pl.pallas_call(
        _merge_kernel,
        out_shape=jax.ShapeDtypeStruct((rows, d), F32),
        grid=(rows // tm,),
        in_specs=[
            pl.BlockSpec((tm, M_V), lambda i: (i, 0)),
            pl.BlockSpec((tm, A_V), lambda i: (i, 0)),
            pl.BlockSpec((tm, d), lambda i: (i, COL_GM // d)),
            pl.BlockSpec((tm, d), lambda i: (i, COL_GM // d + 1)),
            pl.BlockSpec((tm, d), lambda i: (i, 0)),
            pl.BlockSpec((M_V, d), const),
            pl.BlockSpec((A_V, d), const),
            pl.BlockSpec((d, d), const),
            pl.BlockSpec((1, d), const),
        ],
        out_specs=pl.BlockSpec((tm, d), lambda i: (i, 0)),
        compiler_params=_params(("parallel",)),
        name="merge",
    )(hm, ha, zx, zx, h, mw, aw, wo, g_post)


def _rope_tables(n_rows):
    half = ROT_DIM // 2
    inv_freq = ROPE_THETA ** (-jnp.arange(0, ROT_DIM, 2, dtype=F32) / ROT_DIM)
    ang = jnp.arange(n_rows, dtype=F32)[:, None] * inv_freq[None, :]
    cos, sin = jnp.cos(ang), jnp.sin(ang)
    zeros = jnp.zeros((n_rows, A_DH - ROT_DIM), F32)
    zh = jnp.zeros((n_rows, half), F32)
    c = jnp.concatenate([cos, cos, zeros + 1.0], axis=1)
    sa = jnp.concatenate([-sin, zh, zeros], axis=1)
    sb = jnp.concatenate([zh, sin, zeros], axis=1)
    return tuple(jnp.concatenate([tab, tab], axis=1) for tab in (c, sa, sb))


def kernel(x, meta, ffn1_pre_g, ffn1_post_g, ffn1_w_gate, ffn1_w_up, ffn1_w_down, mix_pre_g, mix_post_g, w_in, b_in, m_conv_w, m_conv_b, m_norm_g, m_w_branch, a_lambda_q1, a_lambda_k1, a_lambda_q2, a_lambda_k2, a_norm_g, a_w_branch, w_out, ffn2_pre_g, ffn2_post_g, ffn2_w_gate, ffn2_w_up, ffn2_w_down):
    batch, seq, d = x.shape
    assert batch == 1 and meta.shape == (N_META, d)
    assert (2 * d) % A_QK == 0 and COL_GM % d == 0
    l = 0
    xr = x.reshape(seq, d)
    row = lambda v: v[l].reshape(1, -1)
    bf = lambda w: w[l].astype(BF16)

    w_in_l, b_in_l = w_in[l], b_in[l]
    gate_end = GATE_OFFSET + N_GATES
    w_main = jnp.concatenate([w_in_l[:, :GATE_OFFSET], w_in_l[:, gate_end:]], axis=1).astype(BF16)
    b_main = jnp.concatenate([b_in_l[:GATE_OFFSET], b_in_l[gate_end:]]).reshape(1, -1)
    w_gate = jnp.pad(w_in_l[:, GATE_OFFSET:gate_end], ((0, 0), (0, LANES - N_GATES))).astype(BF16)
    b_gate = jnp.pad(b_in_l[GATE_OFFSET:gate_end], (0, LANES - N_GATES)).reshape(1, -1)
    rope = _rope_tables(N_META + seq)
    rope_m = tuple(tab[:N_META] for tab in rope)
    rope_x = tuple(tab[N_META:] for tab in rope)

    ffn1 = functools.partial(_ffn, g_pre=row(ffn1_pre_g), g_post=row(ffn1_post_g),
                             w_gate=bf(ffn1_w_gate), w_up=bf(ffn1_w_up), w_down=bf(ffn1_w_down),
                             tm=512, tf=512)
    inproj = functools.partial(_inproj, g=row(mix_pre_g), w_main=w_main, b_main=b_main,
                               w_gate=w_gate, b_gate=b_gate, tm=1024)
    hx = ffn1(xr)
    hmeta = ffn1(meta)
    zx, gx = inproj(hx, rope_c=rope_x[0], rope_sa=rope_x[1], rope_sb=rope_x[2])
    zm, gm = inproj(hmeta, rope_c=rope_m[0], rope_sa=rope_m[1], rope_sb=rope_m[2])

    hm = _mlstm(zx, gx, zm, gm, m_conv_w[l], row(m_conv_b), row(m_norm_g), t=256)
    ha = _attn(zx, zm, row(a_lambda_q1), row(a_lambda_k1), row(a_lambda_q2), row(a_lambda_k2),
               row(a_norm_g), tq=256)
    h2 = _merge(hm, ha, zx, hx, bf(m_w_branch), bf(a_w_branch), bf(w_out), row(mix_post_g), tm=512)
    out = _ffn(h2, row(ffn2_pre_g), row(ffn2_post_g), bf(ffn2_w_gate), bf(ffn2_w_up),
               bf(ffn2_w_down), tm=512, tf=512)
    return out.reshape(batch, seq, d)
```

```python
import functools

import jax
import jax.numpy as jnp
from jax import lax
from jax.experimental import pallas as pl
from jax.experimental.pallas import tpu as pltpu

F32 = jnp.float32
BF16 = jnp.bfloat16

EPS = 1e-6
N_META = 16
M_HEADS = 4
M_DQK = 128
M_DV = 256
M_QK = M_HEADS * M_DQK
M_V = M_HEADS * M_DV
CONV_W = 4
A_HEADS = 8
A_DH = 64
A_DV = 2 * A_DH
A_QK = A_HEADS * 2 * A_DH
A_V = A_HEADS * A_DV
ROT_DIM = A_DH // 4
ROPE_THETA = 500000.0
LAMBDA_INIT = 0.8 - 0.6 * 1.0
Q_SCALE = A_DH ** -0.5 * 1.4426950408889634

LANES = 128
GATE_OFFSET = 2 * M_QK + 2 * M_V
N_GATES = 2 * M_HEADS
COL_QK = 0
COL_MV = 2 * M_QK
COL_MO = COL_MV + M_V
COL_AQ = COL_MO + M_V
COL_AK = COL_AQ + A_QK
COL_AV = COL_AK + A_QK
COL_GM = COL_AV + A_V

VMEM_LIMIT = 56 * 1024 * 1024


def _params(sem):
    return pltpu.CompilerParams(dimension_semantics=sem, vmem_limit_bytes=VMEM_LIMIT)


def _rms(x):
    return x * lax.rsqrt(jnp.mean(x * x, axis=-1, keepdims=True) + EPS)


def _ffn_kernel(h_ref, gpre_ref, gpost_ref, wg_ref, wu_ref, wd_ref, o_ref, u_scr, acc_scr):
    f = pl.program_id(1)

    @pl.when(f == 0)
    def _():
        u_scr[...] = (_rms(h_ref[...]) * gpre_ref[...]).astype(BF16)
        acc_scr[...] = jnp.zeros_like(acc_scr)

    u = u_scr[...]
    g = jnp.dot(u, wg_ref[...], preferred_element_type=F32)
    up = jnp.dot(u, wu_ref[...], preferred_element_type=F32)
    a = (g * jax.nn.sigmoid(g) * up).astype(BF16)
    acc_scr[...] += jnp.dot(a, wd_ref[...], preferred_element_type=F32)

    @pl.when(f == pl.num_programs(1) - 1)
    def _():
        o_ref[...] = h_ref[...] + 0.5 * (_rms(acc_scr[...]) * gpost_ref[...])


def _ffn(h, g_pre, g_post, w_gate, w_up, w_down, *, tm, tf):
    rows, d = h.shape
    d_ff = w_gate.shape[1]
    tm = min(tm, rows)
    return pl.pallas_call(
        _ffn_kernel,
        out_shape=jax.ShapeDtypeStruct((rows, d), F32),
        grid=(rows // tm, d_ff // tf),
        in_specs=[
            pl.BlockSpec((tm, d), lambda i, f: (i, 0)),
            pl.BlockSpec((1, d), lambda i, f: (0, 0)),
            pl.BlockSpec((1, d), lambda i, f: (0, 0)),
            pl.BlockSpec((d, tf), lambda i, f: (0, f)),
            pl.BlockSpec((d, tf), lambda i, f: (0, f)),
            pl.BlockSpec((tf, d), lambda i, f: (f, 0)),
        ],
        out_specs=pl.BlockSpec((tm, d), lambda i, f: (i, 0)),
        scratch_shapes=[pltpu.VMEM((tm, d), BF16), pltpu.VMEM((tm, d), F32)],
        compiler_params=_params(("parallel", "arbitrary")),
        name="ffn",
    )(h, g_pre, g_post, w_gate, w_up, w_down)


def _rope(z, c, sa, sb):
    outs = []
    for grp in range(z.shape[1] // LANES):
        x = z[:, grp * LANES:(grp + 1) * LANES]
        outs.append(x * c + pltpu.roll(x, LANES - ROT_DIM // 2, 1) * sa
                    + pltpu.roll(x, ROT_DIM // 2, 1) * sb)
    return jnp.concatenate(outs, axis=1)


def _inproj_kernel(h_ref, g_ref, w_ref, b_ref, wgate_ref, bgate_ref, c_ref, sa_ref, sb_ref,
                   z_ref, gate_ref, u_scr, *, tn):
    j = pl.program_id(1)

    @pl.when(j == 0)
    def _():
        u = (_rms(h_ref[...]) * g_ref[...]).astype(BF16)
        u_scr[...] = u
        gate_ref[...] = jnp.dot(u, wgate_ref[...], preferred_element_type=F32) + bgate_ref[...]

    z = jnp.dot(u_scr[...], w_ref[...], preferred_element_type=F32) + b_ref[...]
    is_q = j == COL_AQ // tn
    is_k = j == COL_AK // tn

    @pl.when(is_q)
    def _():
        z_ref[...] = (_rope(z, c_ref[...], sa_ref[...], sb_ref[...]) * Q_SCALE).astype(BF16)

    @pl.when(is_k)
    def _():
        z_ref[...] = _rope(z, c_ref[...], sa_ref[...], sb_ref[...]).astype(BF16)

    @pl.when(jnp.logical_not(jnp.logical_or(is_q, is_k)))
    def _():
        z_ref[...] = z.astype(BF16)


def _inproj(h, g, w_main, b_main, w_gate, b_gate, rope_c, rope_sa, rope_sb, *, tm):
    rows, d = h.shape
    n = w_main.shape[1]
    tn = A_QK
    tm = min(tm, rows)
    return pl.pallas_call(
        functools.partial(_inproj_kernel, tn=tn),
        out_shape=(jax.ShapeDtypeStruct((rows, n), BF16),
                   jax.ShapeDtypeStruct((rows, LANES), F32)),
        grid=(rows // tm, n // tn),
        in_specs=[
            pl.BlockSpec((tm, d), lambda i, j: (i, 0)),
            pl.BlockSpec((1, d), lambda i, j: (0, 0)),
            pl.BlockSpec((d, tn), lambda i, j: (0, j)),
            pl.BlockSpec((1, tn), lambda i, j: (0, j)),
            pl.BlockSpec((d, LANES), lambda i, j: (0, 0)),
            pl.BlockSpec((1, LANES), lambda i, j: (0, 0)),
            pl.BlockSpec((tm, LANES), lambda i, j: (i, 0)),
            pl.BlockSpec((tm, LANES), lambda i, j: (i, 0)),
            pl.BlockSpec((tm, LANES), lambda i, j: (i, 0)),
        ],
        out_specs=(pl.BlockSpec((tm, tn), lambda i, j: (i, j)),
                   pl.BlockSpec((tm, LANES), lambda i, j: (i, 0))),
        scratch_shapes=[pltpu.VMEM((tm, d), BF16)],
        compiler_params=_params(("parallel", "arbitrary")),
        name="inproj",
    )(h, g, w_main, b_main, w_gate, b_gate, rope_c, rope_sa, rope_sb)


CONV_PAD = 8


def _conv_silu(xbuf, n, cw_ref, cb_ref):
    y = cb_ref[...]
    for i in range(CONV_W):
        start = CONV_PAD + i - (CONV_W - 1)
        y = y + cw_ref[i:i + 1, :] * xbuf[start:start + n, :]
    return y * jax.nn.sigmoid(y)


def _log_sigmoid(x):
    return jnp.minimum(x, 0.0) - jnp.log1p(jnp.exp(-jnp.abs(x)))


def _cumsum_rows(x):
    n = x.shape[0]
    r = lax.broadcasted_iota(jnp.int32, (n, n), 0)
    c = lax.broadcasted_iota(jnp.int32, (n, n), 1)
    tril = (r >= c).astype(F32)
    return jnp.dot(tril, x, preferred_element_type=F32, precision=lax.Precision.HIGHEST)


def _state_update(c_scr, m_scr, hd, kf, vext, li_col, b_col, g_tot, m_prev):
    log_w = (g_tot - b_col) + li_col
    m_new = jnp.maximum(g_tot + m_prev, jnp.max(log_w, axis=0, keepdims=True))
    wk = jnp.exp(log_w - m_new)
    decay = jnp.exp(g_tot + m_prev - m_new)
    kw = (kf * wk).astype(BF16)
    upd = lax.dot_general(kw, vext, (((0,), (0,)), ((), ())), preferred_element_type=F32)
    c_scr[hd] = decay * c_scr[hd] + upd
    m_scr[hd] = jnp.broadcast_to(m_new, m_scr.shape[1:])


def _mlstm_kernel(qk_ref, mv_ref, mo_ref, gt_ref, qkm_ref, mvm_ref, gtm_ref, cw_ref, cb_ref,
                  ng_ref, o_ref, c_scr, m_scr, xbuf):
    i = pl.program_id(0)
    t = qk_ref.shape[0]
    kscale = M_DQK ** -0.5

    @pl.when(i == 0)
    def _():
        nm = qkm_ref.shape[0]
        c_scr[...] = jnp.zeros_like(c_scr)
        m_scr[...] = jnp.zeros_like(m_scr)
        xbuf[0:CONV_PAD, :] = jnp.zeros((CONV_PAD, xbuf.shape[1]), F32)
        xbuf[CONV_PAD:CONV_PAD + nm, :] = qkm_ref[...].astype(F32)
        qk = _conv_silu(xbuf, nm, cw_ref, cb_ref)
        xbuf[0:CONV_PAD, :] = xbuf[nm:nm + CONV_PAD, :]
        gts = gtm_ref[...]
        b_all = _cumsum_rows(_log_sigmoid(gts))
        ones = jnp.ones((nm, LANES), BF16)
        for hd in range(M_HEADS):
            kf = qk[:, M_QK + hd * M_DQK:M_QK + (hd + 1) * M_DQK] * kscale
            vext = jnp.concatenate([mvm_ref[:, hd * M_DV:(hd + 1) * M_DV], ones], axis=1)
            b_col = b_all[:, M_HEADS + hd:M_HEADS + hd + 1]
            _state_update(c_scr, m_scr, hd, kf, vext, gts[:, hd:hd + 1], b_col,
                          b_col[nm - 1:nm, :], m_scr[hd][0:1, 0:1])

    xbuf[CONV_PAD:CONV_PAD + t, :] = qk_ref[...].astype(F32)
    qk = _conv_silu(xbuf, t, cw_ref, cb_ref)
    xbuf[0:CONV_PAD, :] = xbuf[t:t + CONV_PAD, :]

    gts = gt_ref[...]
    b_all = _cumsum_rows(_log_sigmoid(gts))
    lane = lax.broadcasted_iota(jnp.int32, gts.shape, 1)
    rowform = jnp.where(lane < M_HEADS, gts, b_all).T
    r = lax.broadcasted_iota(jnp.int32, (t, t), 0)
    c = lax.broadcasted_iota(jnp.int32, (t, t), 1)
    causal = r >= c
    ones = jnp.ones((t, LANES), BF16)

    for hd in range(M_HEADS):
        qh = qk[:, hd * M_DQK:(hd + 1) * M_DQK].astype(BF16)
        kf = qk[:, M_QK + hd * M_DQK:M_QK + (hd + 1) * M_DQK] * kscale
        kh = kf.astype(BF16)
        vext = jnp.concatenate([mv_ref[:, hd * M_DV:(hd + 1) * M_DV], ones], axis=1)
        li_col = gts[:, hd:hd + 1]
        b_col = b_all[:, M_HEADS + hd:M_HEADS + hd + 1]
        li_row = rowform[hd:hd + 1, :]
        b_row = rowform[M_HEADS + hd:M_HEADS + hd + 1, :]
        g_tot = b_col[t - 1:t, :]
        m_prev = m_scr[hd][0:1, 0:1]

        log_d = jnp.where(causal, (b_col - b_row) + li_row, -jnp.inf)
        m_inter = b_col + m_prev
        m_row = jnp.maximum(m_inter, jnp.max(log_d, axis=1, keepdims=True))
        s = lax.dot_general(qh, kh, (((1,), (1,)), ((), ())), preferred_element_type=F32)
        sd = (s * jnp.exp(log_d - m_row)).astype(BF16)
        inter = jnp.dot(qh, c_scr[hd].astype(BF16), preferred_element_type=F32)
        numden = jnp.exp(m_inter - m_row) * inter + jnp.dot(sd, vext, preferred_element_type=F32)
        den = jnp.maximum(jnp.abs(numden[:, M_DV:M_DV + 1]), jnp.exp(-m_row))
        hh = numden[:, :M_DV] / den
        hn = _rms(hh) * ng_ref[:, hd * M_DV:(hd + 1) * M_DV]
        og = jax.nn.sigmoid(mo_ref[:, hd * M_DV:(hd + 1) * M_DV].astype(F32))
        o_ref[:, hd * M_DV:(hd + 1) * M_DV] = (og * hn).astype(BF16)

        _state_update(c_scr, m_scr, hd, kf, vext, li_col, b_col, g_tot, m_prev)


def _mlstm(zx, gx, zm, gm, conv_w, conv_b, norm_g, *, t):
    rows = zx.shape[0]
    nm = zm.shape[0]
    w_qk = 2 * M_QK
    return pl.pallas_call(
        _mlstm_kernel,
        out_shape=jax.ShapeDtypeStruct((rows, M_V), BF16),
        grid=(rows // t,),
        in_specs=[
            pl.BlockSpec((t, w_qk), lambda i: (i, COL_QK // w_qk)),
            pl.BlockSpec((t, M_V), lambda i: (i, COL_MV // M_V)),
            pl.BlockSpec((t, M_V), lambda i: (i, COL_MO // M_V)),
            pl.BlockSpec((t, LANES), lambda i: (i, 0)),
            pl.BlockSpec((nm, w_qk), lambda i: (0, COL_QK // w_qk)),
            pl.BlockSpec((nm, M_V), lambda i: (0, COL_MV // M_V)),
            pl.BlockSpec((nm, LANES), lambda i: (0, 0)),
            pl.BlockSpec((CONV_W, w_qk), lambda i: (0, 0)),
            pl.BlockSpec((1, w_qk), lambda i: (0, 0)),
            pl.BlockSpec((1, M_V), lambda i: (0, 0)),
        ],
        out_specs=pl.BlockSpec((t, M_V), lambda i: (i, 0)),
        scratch_shapes=[
            pltpu.VMEM((M_HEADS, M_DQK, M_DV + LANES), F32),
            pltpu.VMEM((M_HEADS, 8, LANES), F32),
            pltpu.VMEM((t + CONV_PAD, w_qk), F32),
        ],
        compiler_params=_params(("arbitrary",)),
        name="mlstm",
    )(zx, zx, zx, gx, zm, zm, gm, conv_w, conv_b, norm_g)


ATT_CHUNK = 64


def _attn_kernel(q_ref, k_ref, v_ref, km_ref, vm_ref, lq1_ref, lk1_ref, lq2_ref, lk2_ref,
                 g_ref, o_ref, q2_scr, s0_scr, s1_scr, m_scr, acc_scr):
    i = pl.program_id(1)
    tq = q_ref.shape[0]
    tk = tq // 2
    rows = 2 * tq
    nt = (((1,), (1,)), ((), ()))
    q = q_ref[...]
    lane = lax.broadcasted_iota(jnp.int32, q.shape, 1)
    zero = jnp.zeros_like(q)
    q2_scr[0:tq, :] = jnp.where(lane < A_DH, q, zero)
    q2_scr[tq:rows, :] = jnp.where(lane >= A_DH, q, zero)
    ones_k = jnp.ones((tk, LANES), BF16)

    def scores(blk):
        off = pl.multiple_of(blk * tk, tk)
        return lax.dot_general(q2_scr[...], k_ref[pl.ds(off, tk), :], nt, preferred_element_type=F32)

    def softmax_pv(s_ref, blk):
        cols = [s_ref[:, c * LANES:(c + 1) * LANES] for c in range(tk // LANES)]
        part = cols[0]
        for sc in cols[1:]:
            part = jnp.maximum(part, sc)
        m_old = m_scr[...]
        m_new = jnp.maximum(m_old, jnp.max(part, axis=1, keepdims=True))
        m_scr[...] = m_new
        alpha = jnp.exp2(m_old - m_new)
        p = jnp.concatenate(
            [jnp.exp2(s_ref[:, c * LANES:(c + 1) * LANES] - m_new) for c in range(tk // LANES)],
            axis=1).astype(BF16)
        off = pl.multiple_of(blk * tk, tk)
        vext = jnp.concatenate([v_ref[pl.ds(off, tk), :], ones_k], axis=1)
        pv = jnp.dot(p, vext, preferred_element_type=F32)
        for c in range(2):
            sl = slice(c * LANES, (c + 1) * LANES)
            acc_scr[:, sl] = alpha * acc_scr[:, sl] + pv[:, sl]

    s = lax.dot_general(q2_scr[...], km_ref[...], nt, preferred_element_type=F32)
    m0 = jnp.max(s, axis=1, keepdims=True)
    p = jnp.exp2(s - m0).astype(BF16)
    vext = jnp.concatenate([vm_ref[...], jnp.ones((vm_ref.shape[0], LANES), BF16)], axis=1)
    acc_scr[...] = jnp.dot(p, vext, preferred_element_type=F32)
    m_scr[...] = jnp.broadcast_to(m0, m_scr.shape)

    s0_scr[...] = scores(0)

    def pair(pr, carry):
        s1_scr[...] = scores(2 * pr + 1)
        softmax_pv(s0_scr, 2 * pr)
        s0_scr[...] = scores(2 * pr + 2)
        softmax_pv(s1_scr, 2 * pr + 1)
        return carry

    lax.fori_loop(0, i, pair, 0)

    s1_scr[...] = scores(2 * i + 1)
    r = lax.broadcasted_iota(jnp.int32, (rows, tk), 0)
    c = lax.broadcasted_iota(jnp.int32, (rows, tk), 1)
    qchunk = jnp.where(r >= tq, r - tq, r) // ATT_CHUNK
    s0_scr[...] = jnp.where(c // ATT_CHUNK <= qchunk, s0_scr[...], -jnp.inf)
    softmax_pv(s0_scr, 2 * i)
    s1_scr[...] = jnp.where((c + tk) // ATT_CHUNK <= qchunk, s1_scr[...], -jnp.inf)
    softmax_pv(s1_scr, 2 * i + 1)

    lam = (jnp.exp(jnp.sum(lq1_ref[...] * lk1_ref[...], axis=1, keepdims=True))
           - jnp.exp(jnp.sum(lq2_ref[...] * lk2_ref[...], axis=1, keepdims=True)) + LAMBDA_INIT)
    o = acc_scr[:, 0:A_DV] / acc_scr[:, A_DV:A_DV + 1]
    o = o[:tq] - lam * o[tq:]
    o_ref[...] = (_rms(o) * g_ref[...] * (1.0 - LAMBDA_INIT)).astype(BF16)


def _attn(zx, zm, lq1, lk1, lq2, lk2, norm_g, *, tq):
    rows = zx.shape[0]
    nm = zm.shape[0]
    small = pl.BlockSpec((1, A_DH), lambda h, i: (0, 0))
    return pl.pallas_call(
        _attn_kernel,
        out_shape=jax.ShapeDtypeStruct((rows, A_V), BF16),
        grid=(A_HEADS, rows // tq),
        in_specs=[
            pl.BlockSpec((tq, A_DV), lambda h, i: (i, COL_AQ // A_DV + h)),
            pl.BlockSpec((rows, A_DV), lambda h, i: (0, COL_AK // A_DV + h)),
            pl.BlockSpec((rows, A_DV), lambda h, i: (0, COL_AV // A_DV + h)),
            pl.BlockSpec((nm, A_DV), lambda h, i: (0, COL_AK // A_DV + h)),
            pl.BlockSpec((nm, A_DV), lambda h, i: (0, COL_AV // A_DV + h)),
            small, small, small, small,
            pl.BlockSpec((1, A_DV), lambda h, i: (0, h)),
        ],
        out_specs=pl.BlockSpec((tq, A_DV), lambda h, i: (i, h)),
        scratch_shapes=[
            pltpu.VMEM((2 * tq, A_DV), BF16),
            pltpu.VMEM((2 * tq, tq // 2), F32),
            pltpu.VMEM((2 * tq, tq // 2), F32),
            pltpu.VMEM((2 * tq, LANES), F32),
            pltpu.VMEM((2 * tq, 2 * LANES), F32),
        ],
        compiler_params=_params(("parallel", "arbitrary")),
        name="diffattn",
    )(zx, zx, zx, zm, zm, lq1, lk1, lq2, lk2, norm_g)


def _merge_kernel(hm_ref, ha_ref, gm_ref, ga_ref, h_ref, mw_ref, aw_ref, wo_ref, g_ref, o_ref):
    br_m = jnp.dot(hm_ref[...], mw_ref[...], preferred_element_type=F32)
    br_a = jnp.dot(ha_ref[...], aw_ref[...], preferred_element_type=F32)
    y = (jax.nn.sigmoid(gm_ref[...].astype(F32)) * br_m
         + jax.nn.sigmoid(ga_ref[...].astype(F32)) * br_a)
    out = jnp.dot(y.astype(BF16), wo_ref[...], preferred_element_type=F32)
    o_ref[...] = h_ref[...] + _rms(out) * g_ref[...]


def _merge(hm, ha, zx, h, mw, aw, wo, g_post, *, tm):
    rows, d = h.shape
    const = lambda i: (0, 0)
    return pl.pallas_call(
        _merge_kernel,
        out_shape=jax.ShapeDtypeStruct((rows, d), F32),
        grid=(rows // tm,),
        in_specs=[
            pl.BlockSpec((tm, M_V), lambda i: (i, 0)),
            pl.BlockSpec((tm, A_V), lambda i: (i, 0)),
            pl.BlockSpec((tm, d), lambda i: (i, COL_GM // d)),
            pl.BlockSpec((tm, d), lambda i: (i, COL_GM // d + 1)),
            pl.BlockSpec((tm, d), lambda i: (i, 0)),
            pl.BlockSpec((M_V, d), const),
            pl.BlockSpec((A_V, d), const),
            pl.BlockSpec((d, d), const),
            pl.BlockSpec((1, d), const),
        ],
        out_specs=pl.BlockSpec((tm, d), lambda i: (i, 0)),
        compiler_params=_params(("parallel",)),
        name="merge",
    )(hm, ha, zx, zx, h, mw, aw, wo, g_post)


def _rope_tables(n_rows):
    half = ROT_DIM // 2
    inv_freq = ROPE_THETA ** (-jnp.arange(0, ROT_DIM, 2, dtype=F32) / ROT_DIM)
    ang = jnp.arange(n_rows, dtype=F32)[:, None] * inv_freq[None, :]
    cos, sin = jnp.cos(ang), jnp.sin(ang)
    zeros = jnp.zeros((n_rows, A_DH - ROT_DIM), F32)
    zh = jnp.zeros((n_rows, half), F32)
    c = jnp.concatenate([cos, cos, zeros + 1.0], axis=1)
    sa = jnp.concatenate([-sin, zh, zeros], axis=1)
    sb = jnp.concatenate([zh, sin, zeros], axis=1)
    return tuple(jnp.concatenate([tab, tab], axis=1) for tab in (c, sa, sb))


def kernel(x, meta, ffn1_pre_g, ffn1_post_g, ffn1_w_gate, ffn1_w_up, ffn1_w_down, mix_pre_g, mix_post_g, w_in, b_in, m_conv_w, m_conv_b, m_norm_g, m_w_branch, a_lambda_q1, a_lambda_k1, a_lambda_q2, a_lambda_k2, a_norm_g, a_w_branch, w_out, ffn2_pre_g, ffn2_post_g, ffn2_w_gate, ffn2_w_up, ffn2_w_down):
    batch, seq, d = x.shape
    assert batch == 1 and meta.shape == (N_META, d)
    assert (2 * d) % A_QK == 0 and COL_GM % d == 0
    l = 0
    xr = x.reshape(seq, d)
    row = lambda v: v[l].reshape(1, -1)
    bf = lambda w: w[l].astype(BF16)

    w_in_l, b_in_l = w_in[l], b_in[l]
    gate_end = GATE_OFFSET + N_GATES
    w_main = jnp.concatenate([w_in_l[:, :GATE_OFFSET], w_in_l[:, gate_end:]], axis=1).astype(BF16)
    b_main = jnp.concatenate([b_in_l[:GATE_OFFSET], b_in_l[gate_end:]]).reshape(1, -1)
    w_gate = jnp.pad(w_in_l[:, GATE_OFFSET:gate_end], ((0, 0), (0, LANES - N_GATES))).astype(BF16)
    b_gate = jnp.pad(b_in_l[GATE_OFFSET:gate_end], (0, LANES - N_GATES)).reshape(1, -1)
    rope = _rope_tables(N_META + seq)
    rope_m = tuple(tab[:N_META] for tab in rope)
    rope_x = tuple(tab[N_META:] for tab in rope)

    ffn1 = functools.partial(_ffn, g_pre=row(ffn1_pre_g), g_post=row(ffn1_post_g),
                             w_gate=bf(ffn1_w_gate), w_up=bf(ffn1_w_up), w_down=bf(ffn1_w_down),
                             tm=512, tf=512)
    inproj = functools.partial(_inproj, g=row(mix_pre_g), w_main=w_main, b_main=b_main,
                               w_gate=w_gate, b_gate=b_gate, tm=1024)
    hx = ffn1(xr)
    hmeta = ffn1(meta)
    zx, gx = inproj(hx, rope_c=rope_x[0], rope_sa=rope_x[1], rope_sb=rope_x[2])
    zm, gm = inproj(hmeta, rope_c=rope_m[0], rope_sa=rope_m[1], rope_sb=rope_m[2])

    hm = _mlstm(zx, gx, zm, gm, m_conv_w[l], row(m_conv_b), row(m_norm_g), t=256)
    ha = _attn(zx, zm, row(a_lambda_q1), row(a_lambda_k1), row(a_lambda_q2), row(a_lambda_k2),
               row(a_norm_g), tq=512)
    h2 = _merge(hm, ha, zx, hx, bf(m_w_branch), bf(a_w_branch), bf(w_out), row(mix_post_g), tm=512)
    out = _ffn(h2, row(ffn2_pre_g), row(ffn2_post_g), bf(ffn2_w_gate), bf(ffn2_w_up),
               bf(ffn2_w_down), tm=512, tf=512)
    return out.reshape(batch, seq, d)
```

```python
import functools

import jax
import jax.numpy as jnp
from jax import lax
from jax.experimental import pallas as pl
from jax.experimental.pallas import tpu as pltpu

F32 = jnp.float32
BF16 = jnp.bfloat16

EPS = 1e-6
N_META = 16
M_HEADS = 4
M_DQK = 128
M_DV = 256
M_QK = M_HEADS * M_DQK
M_V = M_HEADS * M_DV
CONV_W = 4
A_HEADS = 8
A_DH = 64
A_DV = 2 * A_DH
A_QK = A_HEADS * 2 * A_DH
A_V = A_HEADS * A_DV
ROT_DIM = A_DH // 4
ROPE_THETA = 500000.0
LAMBDA_INIT = 0.8 - 0.6 * 1.0
Q_SCALE = A_DH ** -0.5 * 1.4426950408889634

LANES = 128
GATE_OFFSET = 2 * M_QK + 2 * M_V
N_GATES = 2 * M_HEADS
COL_QK = 0
COL_MV = 2 * M_QK
COL_MO = COL_MV + M_V
COL_AQ = COL_MO + M_V
COL_AK = COL_AQ + A_QK
COL_AV = COL_AK + A_QK
COL_GM = COL_AV + A_V

VMEM_LIMIT = 56 * 1024 * 1024


def _params(sem):
    return pltpu.CompilerParams(dimension_semantics=sem, vmem_limit_bytes=VMEM_LIMIT)


def _rms(x):
    return x * lax.rsqrt(jnp.mean(x * x, axis=-1, keepdims=True) + EPS)


def _ffn_kernel(h_ref, gpre_ref, gpost_ref, wg_ref, wu_ref, wd_ref, o_ref, u_scr, acc_scr):
    f = pl.program_id(1)

    @pl.when(f == 0)
    def _():
        u_scr[...] = (_rms(h_ref[...]) * gpre_ref[...]).astype(BF16)
        acc_scr[...] = jnp.zeros_like(acc_scr)

    u = u_scr[...]
    g = jnp.dot(u, wg_ref[...], preferred_element_type=F32)
    up = jnp.dot(u, wu_ref[...], preferred_element_type=F32)
    a = (g * jax.nn.sigmoid(g) * up).astype(BF16)
    acc_scr[...] += jnp.dot(a, wd_ref[...], preferred_element_type=F32)

    @pl.when(f == pl.num_programs(1) - 1)
    def _():
        o_ref[...] = h_ref[...] + 0.5 * (_rms(acc_scr[...]) * gpost_ref[...])


def _ffn(h, g_pre, g_post, w_gate, w_up, w_down, *, tm, tf):
    rows, d = h.shape
    d_ff = w_gate.shape[1]
    tm = min(tm, rows)
    return pl.pallas_call(
        _ffn_kernel,
        out_shape=jax.ShapeDtypeStruct((rows, d), F32),
        grid=(rows // tm, d_ff // tf),
        in_specs=[
            pl.BlockSpec((tm, d), lambda i, f: (i, 0)),
            pl.BlockSpec((1, d), lambda i, f: (0, 0)),
            pl.BlockSpec((1, d), lambda i, f: (0, 0)),
            pl.BlockSpec((d, tf), lambda i, f: (0, f)),
            pl.BlockSpec((d, tf), lambda i, f: (0, f)),
            pl.BlockSpec((tf, d), lambda i, f: (f, 0)),
        ],
        out_specs=pl.BlockSpec((tm, d), lambda i, f: (i, 0)),
        scratch_shapes=[pltpu.VMEM((tm, d), BF16), pltpu.VMEM((tm, d), F32)],
        compiler_params=_params(("parallel", "arbitrary")),
        name="ffn",
    )(h, g_pre, g_post, w_gate, w_up, w_down)


def _rope(z, c, sa, sb):
    outs = []
    for grp in range(z.shape[1] // LANES):
        x = z[:, grp * LANES:(grp + 1) * LANES]
        outs.append(x * c + pltpu.roll(x, LANES - ROT_DIM // 2, 1) * sa
                    + pltpu.roll(x, ROT_DIM // 2, 1) * sb)
    return jnp.concatenate(outs, axis=1)


def _inproj_kernel(h_ref, g_ref, w_ref, b_ref, wgate_ref, bgate_ref, c_ref, sa_ref, sb_ref,
                   z_ref, gate_ref, u_scr, *, tn):
    j = pl.program_id(1)

    @pl.when(j == 0)
    def _():
        u = (_rms(h_ref[...]) * g_ref[...]).astype(BF16)
        u_scr[...] = u
        gate_ref[...] = jnp.dot(u, wgate_ref[...], preferred_element_type=F32) + bgate_ref[...]

    z = jnp.dot(u_scr[...], w_ref[...], preferred_element_type=F32) + b_ref[...]
    is_q = j == COL_AQ // tn
    is_k = j == COL_AK // tn

    @pl.when(is_q)
    def _():
        z_ref[...] = (_rope(z, c_ref[...], sa_ref[...], sb_ref[...]) * Q_SCALE).astype(BF16)

    @pl.when(is_k)
    def _():
        z_ref[...] = _rope(z, c_ref[...], sa_ref[...], sb_ref[...]).astype(BF16)

    @pl.when(jnp.logical_not(jnp.logical_or(is_q, is_k)))
    def _():
        z_ref[...] = z.astype(BF16)


def _inproj(h, g, w_main, b_main, w_gate, b_gate, rope_c, rope_sa, rope_sb, *, tm):
    rows, d = h.shape
    n = w_main.shape[1]
    tn = A_QK
    tm = min(tm, rows)
    return pl.pallas_call(
        functools.partial(_inproj_kernel, tn=tn),
        out_shape=(jax.ShapeDtypeStruct((rows, n), BF16),
                   jax.ShapeDtypeStruct((rows, LANES), F32)),
        grid=(rows // tm, n // tn),
        in_specs=[
            pl.BlockSpec((tm, d), lambda i, j: (i, 0)),
            pl.BlockSpec((1, d), lambda i, j: (0, 0)),
            pl.BlockSpec((d, tn), lambda i, j: (0, j)),
            pl.BlockSpec((1, tn), lambda i, j: (0, j)),
            pl.BlockSpec((d, LANES), lambda i, j: (0, 0)),
            pl.BlockSpec((1, LANES), lambda i, j: (0, 0)),
            pl.BlockSpec((tm, LANES), lambda i, j: (i, 0)),
            pl.BlockSpec((tm, LANES), lambda i, j: (i, 0)),
            pl.BlockSpec((tm, LANES), lambda i, j: (i, 0)),
        ],
        out_specs=(pl.BlockSpec((tm, tn), lambda i, j: (i, j)),
                   pl.BlockSpec((tm, LANES), lambda i, j: (i, 0))),
        scratch_shapes=[pltpu.VMEM((tm, d), BF16)],
        compiler_params=_params(("parallel", "arbitrary")),
        name="inproj",
    )(h, g, w_main, b_main, w_gate, b_gate, rope_c, rope_sa, rope_sb)


CONV_PAD = 8


def _conv_silu(xbuf, n, cw_ref, cb_ref):
    y = cb_ref[...]
    for i in range(CONV_W):
        start = CONV_PAD + i - (CONV_W - 1)
        y = y + cw_ref[i:i + 1, :] * xbuf[start:start + n, :]
    return y * jax.nn.sigmoid(y)


def _log_sigmoid(x):
    return jnp.minimum(x, 0.0) - jnp.log1p(jnp.exp(-jnp.abs(x)))


def _cumsum_rows(x):
    n = x.shape[0]
    r = lax.broadcasted_iota(jnp.int32, (n, n), 0)
    c = lax.broadcasted_iota(jnp.int32, (n, n), 1)
    tril = (r >= c).astype(F32)
    return jnp.dot(tril, x, preferred_element_type=F32, precision=lax.Precision.HIGHEST)


def _state_update(c_scr, m_scr, hd, kf, vext, li_col, b_col, g_tot, m_prev):
    log_w = (g_tot - b_col) + li_col
    m_new = jnp.maximum(g_tot + m_prev, jnp.max(log_w, axis=0, keepdims=True))
    wk = jnp.exp(log_w - m_new)
    decay = jnp.exp(g_tot + m_prev - m_new)
    kw = (kf * wk).astype(BF16)
    upd = lax.dot_general(kw, vext, (((0,), (0,)), ((), ())), preferred_element_type=F32)
    c_scr[hd] = decay * c_scr[hd] + upd
    m_scr[hd] = jnp.broadcast_to(m_new, m_scr.shape[1:])


def _mlstm_kernel(qk_ref, mv_ref, mo_ref, gt_ref, qkm_ref, mvm_ref, gtm_ref, cw_ref, cb_ref,
                  ng_ref, o_ref, c_scr, m_scr, xbuf):
    i = pl.program_id(0)
    t = qk_ref.shape[0]
    kscale = M_DQK ** -0.5

    @pl.when(i == 0)
    def _():
        nm = qkm_ref.shape[0]
        c_scr[...] = jnp.zeros_like(c_scr)
        m_scr[...] = jnp.zeros_like(m_scr)
        xbuf[0:CONV_PAD, :] = jnp.zeros((CONV_PAD, xbuf.shape[1]), F32)
        xbuf[CONV_PAD:CONV_PAD + nm, :] = qkm_ref[...].astype(F32)
        qk = _conv_silu(xbuf, nm, cw_ref, cb_ref)
        xbuf[0:CONV_PAD, :] = xbuf[nm:nm + CONV_PAD, :]
        gts = gtm_ref[...]
        b_all = _cumsum_rows(_log_sigmoid(gts))
        ones = jnp.ones((nm, LANES), BF16)
        for hd in range(M_HEADS):
            kf = qk[:, M_QK + hd * M_DQK:M_QK + (hd + 1) * M_DQK] * kscale
            vext = jnp.concatenate([mvm_ref[:, hd * M_DV:(hd + 1) * M_DV], ones], axis=1)
            b_col = b_all[:, M_HEADS + hd:M_HEADS + hd + 1]
            _state_update(c_scr, m_scr, hd, kf, vext, gts[:, hd:hd + 1], b_col,
                          b_col[nm - 1:nm, :], m_scr[hd][0:1, 0:1])

    xbuf[CONV_PAD:CONV_PAD + t, :] = qk_ref[...].astype(F32)
    qk = _conv_silu(xbuf, t, cw_ref, cb_ref)
    xbuf[0:CONV_PAD, :] = xbuf[t:t + CONV_PAD, :]

    gts = gt_ref[...]
    b_all = _cumsum_rows(_log_sigmoid(gts))
    lane = lax.broadcasted_iota(jnp.int32, gts.shape, 1)
    rowform = jnp.where(lane < M_HEADS, gts, b_all).T
    r = lax.broadcasted_iota(jnp.int32, (t, t), 0)
    c = lax.broadcasted_iota(jnp.int32, (t, t), 1)
    causal = r >= c
    ones = jnp.ones((t, LANES), BF16)

    for hd in range(M_HEADS):
        qh = qk[:, hd * M_DQK:(hd + 1) * M_DQK].astype(BF16)
        kf = qk[:, M_QK + hd * M_DQK:M_QK + (hd + 1) * M_DQK] * kscale
        kh = kf.astype(BF16)
        vext = jnp.concatenate([mv_ref[:, hd * M_DV:(hd + 1) * M_DV], ones], axis=1)
        li_col = gts[:, hd:hd + 1]
        b_col = b_all[:, M_HEADS + hd:M_HEADS + hd + 1]
        li_row = rowform[hd:hd + 1, :]
        b_row = rowform[M_HEADS + hd:M_HEADS + hd + 1, :]
        g_tot = b_col[t - 1:t, :]
        m_prev = m_scr[hd][0:1, 0:1]

        log_d = jnp.where(causal, (b_col - b_row) + li_row, -jnp.inf)
        m_inter = b_col + m_prev
        m_row = jnp.maximum(m_inter, jnp.max(log_d, axis=1, keepdims=True))
        s = lax.dot_general(qh, kh, (((1,), (1,)), ((), ())), preferred_element_type=F32)
        sd = (s * jnp.exp(log_d - m_row)).astype(BF16)
        inter = jnp.dot(qh, c_scr[hd].astype(BF16), preferred_element_type=F32)
        numden = jnp.exp(m_inter - m_row) * inter + jnp.dot(sd, vext, preferred_element_type=F32)
        den = jnp.maximum(jnp.abs(numden[:, M_DV:M_DV + 1]), jnp.exp(-m_row))
        hh = numden[:, :M_DV] / den
        hn = _rms(hh) * ng_ref[:, hd * M_DV:(hd + 1) * M_DV]
        og = jax.nn.sigmoid(mo_ref[:, hd * M_DV:(hd + 1) * M_DV].astype(F32))
        o_ref[:, hd * M_DV:(hd + 1) * M_DV] = (og * hn).astype(BF16)

        _state_update(c_scr, m_scr, hd, kf, vext, li_col, b_col, g_tot, m_prev)


def _mlstm(zx, gx, zm, gm, conv_w, conv_b, norm_g, *, t):
    rows = zx.shape[0]
    nm = zm.shape[0]
    w_qk = 2 * M_QK
    return pl.pallas_call(
        _mlstm_kernel,
        out_shape=jax.ShapeDtypeStruct((rows, M_V), BF16),
        grid=(rows // t,),
        in_specs=[
            pl.BlockSpec((t, w_qk), lambda i: (i, COL_QK // w_qk)),
            pl.BlockSpec((t, M_V), lambda i: (i, COL_MV // M_V)),
            pl.BlockSpec((t, M_V), lambda i: (i, COL_MO // M_V)),
            pl.BlockSpec((t, LANES), lambda i: (i, 0)),
            pl.BlockSpec((nm, w_qk), lambda i: (0, COL_QK // w_qk)),
            pl.BlockSpec((nm, M_V), lambda i: (0, COL_MV // M_V)),
            pl.BlockSpec((nm, LANES), lambda i: (0, 0)),
            pl.BlockSpec((CONV_W, w_qk), lambda i: (0, 0)),
            pl.BlockSpec((1, w_qk), lambda i: (0, 0)),
            pl.BlockSpec((1, M_V), lambda i: (0, 0)),
        ],
        out_specs=pl.BlockSpec((t, M_V), lambda i: (i, 0)),
        scratch_shapes=[
            pltpu.VMEM((M_HEADS, M_DQK, M_DV + LANES), F32),
            pltpu.VMEM((M_HEADS, 8, LANES), F32),
            pltpu.VMEM((t + CONV_PAD, w_qk), F32),
        ],
        compiler_params=_params(("arbitrary",)),
        name="mlstm",
    )(zx, zx, zx, gx, zm, zm, gm, conv_w, conv_b, norm_g)


ATT_CHUNK = 64


ATT_STRIP = 64


def _attn_kernel(q_ref, k_ref, v_ref, km_ref, vm_ref, lq1_ref, lk1_ref, lq2_ref, lk2_ref,
                 g_ref, o_ref, q2_scr, s0_scr, s1_scr, p0_scr, p1_scr, m_scr, alpha_scr, acc_scr):
    i = pl.program_id(1)
    tq = q_ref.shape[0]
    tk = tq
    rows = 2 * tq
    ncol = tk // LANES
    nt = (((1,), (1,)), ((), ()))
    q = q_ref[...]
    lane = lax.broadcasted_iota(jnp.int32, q.shape, 1)
    zero = jnp.zeros_like(q)
    q2_scr[0:tq, :] = jnp.where(lane < A_DH, q, zero)
    q2_scr[tq:rows, :] = jnp.where(lane >= A_DH, q, zero)
    ones_k = jnp.ones((tk, LANES), BF16)

    def scores(blk):
        off = pl.multiple_of(blk * tk, tk)
        return lax.dot_general(q2_scr[...], k_ref[pl.ds(off, tk), :], nt, preferred_element_type=F32)

    def softmax_pv(s_ref, p_ref, blk, masked):
        for st in range(rows // ATT_STRIP):
            rs = slice(st * ATT_STRIP, (st + 1) * ATT_STRIP)
            if masked:
                qchunk = ((st * ATT_STRIP) % tq) // ATT_CHUNK
                nvis = (qchunk + 1) * ATT_CHUNK
                cvis = lax.broadcasted_iota(jnp.int32, (ATT_STRIP, LANES), 1)
                cols = []
                for c in range(-(-nvis // LANES)):
                    sc = s_ref[rs, c * LANES:(c + 1) * LANES]
                    if (c + 1) * LANES > nvis:
                        sc = jnp.where(cvis < nvis - c * LANES, sc, -jnp.inf)
                    cols.append(sc)
            else:
                cols = [s_ref[rs, c * LANES:(c + 1) * LANES] for c in range(ncol)]
            part = cols[0]
            for sc in cols[1:]:
                part = jnp.maximum(part, sc)
            m_old = m_scr[rs, :]
            m_new = jnp.maximum(m_old, jnp.max(part, axis=1, keepdims=True))
            m_scr[rs, :] = m_new
            alpha_scr[rs, :] = jnp.exp2(m_old - m_new)
            if not masked:
                cols = [s_ref[rs, c * LANES:(c + 1) * LANES] for c in range(ncol)]
            pcols = [jnp.exp2(sc - m_new) for sc in cols]
            pcols += [jnp.zeros_like(pcols[0])] * (ncol - len(cols))
            p_ref[rs, :] = jnp.concatenate(pcols, axis=1).astype(BF16)
        off = pl.multiple_of(blk * tk, tk)
        vext = jnp.concatenate([v_ref[pl.ds(off, tk), :], ones_k], axis=1)
        pv = jnp.dot(p_ref[...], vext, preferred_element_type=F32)
        alpha = alpha_scr[...]
        for c in range(2):
            sl = slice(c * LANES, (c + 1) * LANES)
            acc_scr[:, sl] = alpha * acc_scr[:, sl] + pv[:, sl]

    s = lax.dot_general(q2_scr[...], km_ref[...], nt, preferred_element_type=F32)
    m0 = jnp.max(s, axis=1, keepdims=True)
    p = jnp.exp2(s - m0).astype(BF16)
    vext = jnp.concatenate([vm_ref[...], jnp.ones((vm_ref.shape[0], LANES), BF16)], axis=1)
    acc_scr[...] = jnp.dot(p, vext, preferred_element_type=F32)
    m_scr[...] = jnp.broadcast_to(m0, m_scr.shape)

    s0_scr[...] = scores(0)

    def pair(pr, carry):
        s1_scr[...] = scores(2 * pr + 1)
        softmax_pv(s0_scr, p0_scr, 2 * pr, False)
        s0_scr[...] = scores(2 * pr + 2)
        softmax_pv(s1_scr, p1_scr, 2 * pr + 1, False)
        return carry

    lax.fori_loop(0, i // 2, pair, 0)

    @pl.when(i % 2 == 0)
    def _():
        softmax_pv(s0_scr, p0_scr, i, True)

    @pl.when(i % 2 == 1)
    def _():
        s1_scr[...] = scores(i)
        softmax_pv(s0_scr, p0_scr, i - 1, False)
        softmax_pv(s1_scr, p1_scr, i, True)

    lam = (jnp.exp(jnp.sum(lq1_ref[...] * lk1_ref[...], axis=1, keepdims=True))
           - jnp.exp(jnp.sum(lq2_ref[...] * lk2_ref[...], axis=1, keepdims=True)) + LAMBDA_INIT)
    o = acc_scr[:, 0:A_DV] / acc_scr[:, A_DV:A_DV + 1]
    o = o[:tq] - lam * o[tq:]
    o_ref[...] = (_rms(o) * g_ref[...] * (1.0 - LAMBDA_INIT)).astype(BF16)


def _attn(zx, zm, lq1, lk1, lq2, lk2, norm_g, *, tq):
    rows = zx.shape[0]
    nm = zm.shape[0]
    small = pl.BlockSpec((1, A_DH), lambda h, i: (0, 0))
    return pl.pallas_call(
        _attn_kernel,
        out_shape=jax.ShapeDtypeStruct((rows, A_V), BF16),
        grid=(A_HEADS, rows // tq),
        in_specs=[
            pl.BlockSpec((tq, A_DV), lambda h, i: (i, COL_AQ // A_DV + h)),
            pl.BlockSpec((rows, A_DV), lambda h, i: (0, COL_AK // A_DV + h)),
            pl.BlockSpec((rows, A_DV), lambda h, i: (0, COL_AV // A_DV + h)),
            pl.BlockSpec((nm, A_DV), lambda h, i: (0, COL_AK // A_DV + h)),
            pl.BlockSpec((nm, A_DV), lambda h, i: (0, COL_AV // A_DV + h)),
            small, small, small, small,
            pl.BlockSpec((1, A_DV), lambda h, i: (0, h)),
        ],
        out_specs=pl.BlockSpec((tq, A_DV), lambda h, i: (i, h)),
        scratch_shapes=[
            pltpu.VMEM((2 * tq, A_DV), BF16),
            pltpu.VMEM((2 * tq, tq), F32),
            pltpu.VMEM((2 * tq, tq), F32),
            pltpu.VMEM((2 * tq, tq), BF16),
            pltpu.VMEM((2 * tq, tq), BF16),
            pltpu.VMEM((2 * tq, LANES), F32),
            pltpu.VMEM((2 * tq, LANES), F32),
            pltpu.VMEM((2 * tq, 2 * LANES), F32),
        ],
        compiler_params=_params(("parallel", "arbitrary")),
        name="diffattn",
    )(zx, zx, zx, zm, zm, lq1, lk1, lq2, lk2, norm_g)


def _merge_kernel(hm_ref, ha_ref, gm_ref, ga_ref, h_ref, mw_ref, aw_ref, wo_ref, g_ref, o_ref):
    br_m = jnp.dot(hm_ref[...], mw_ref[...], preferred_element_type=F32)
    br_a = jnp.dot(ha_ref[...], aw_ref[...], preferred_element_type=F32)
    y = (jax.nn.sigmoid(gm_ref[...].astype(F32)) * br_m
         + jax.nn.sigmoid(ga_ref[...].astype(F32)) * br_a)
    out = jnp.dot(y.astype(BF16), wo_ref[...], preferred_element_type=F32)
    o_ref[...] = h_ref[...] + _rms(out) * g_ref[...]


def _merge(hm, ha, zx, h, mw, aw, wo, g_post, *, tm):
    rows, d = h.shape
    const = lambda i: (0, 0)
    return pl.pallas_call(
        _merge_kernel,
        out_shape=jax.ShapeDtypeStruct((rows, d), F32),
        grid=(rows // tm,),
        in_specs=[
            pl.BlockSpec((tm, M_V), lambda i: (i, 0)),
            pl.BlockSpec((tm, A_V), lambda i: (i, 0)),
            pl.BlockSpec((tm, d), lambda i: (i, COL_GM // d)),
            pl.BlockSpec((tm, d), lambda i: (i, COL_GM // d + 1)),
            pl.BlockSpec((tm, d), lambda i: (i, 0)),
            pl.BlockSpec((M_V, d), const),
            pl.BlockSpec((A_V, d), const),
            pl.BlockSpec((d, d), const),
            pl.BlockSpec((1, d), const),
        ],
        out_specs=pl.BlockSpec((tm, d), lambda i: (i, 0)),
        compiler_params=_params(("parallel",)),
        name="merge",
    )(hm, ha, zx, zx, h, mw, aw, wo, g_post)


def _rope_tables(n_rows):
    half = ROT_DIM // 2
    inv_freq = ROPE_THETA ** (-jnp.arange(0, ROT_DIM, 2, dtype=F32) / ROT_DIM)
    ang = jnp.arange(n_rows, dtype=F32)[:, None] * inv_freq[None, :]
    cos, sin = jnp.cos(ang), jnp.sin(ang)
    zeros = jnp.zeros((n_rows, A_DH - ROT_DIM), F32)
    zh = jnp.zeros((n_rows, half), F32)
    c = jnp.concatenate([cos, cos, zeros + 1.0], axis=1)
    sa = jnp.concatenate([-sin, zh, zeros], axis=1)
    sb = jnp.concatenate([zh, sin, zeros], axis=1)
    return tuple(jnp.concatenate([tab, tab], axis=1) for tab in (c, sa, sb))


def kernel(x, meta, ffn1_pre_g, ffn1_post_g, ffn1_w_gate, ffn1_w_up, ffn1_w_down, mix_pre_g, mix_post_g, w_in, b_in, m_conv_w, m_conv_b, m_norm_g, m_w_branch, a_lambda_q1, a_lambda_k1, a_lambda_q2, a_lambda_k2, a_norm_g, a_w_branch, w_out, ffn2_pre_g, ffn2_post_g, ffn2_w_gate, ffn2_w_up, ffn2_w_down):
    batch, seq, d = x.shape
    assert batch == 1 and meta.shape == (N_META, d)
    assert (2 * d) % A_QK == 0 and COL_GM % d == 0
    l = 0
    xr = x.reshape(seq, d)
    row = lambda v: v[l].reshape(1, -1)
    bf = lambda w: w[l].astype(BF16)

    w_in_l, b_in_l = w_in[l], b_in[l]
    gate_end = GATE_OFFSET + N_GATES
    w_main = jnp.concatenate([w_in_l[:, :GATE_OFFSET], w_in_l[:, gate_end:]], axis=1).astype(BF16)
    b_main = jnp.concatenate([b_in_l[:GATE_OFFSET], b_in_l[gate_end:]]).reshape(1, -1)
    w_gate = jnp.pad(w_in_l[:, GATE_OFFSET:gate_end], ((0, 0), (0, LANES - N_GATES))).astype(BF16)
    b_gate = jnp.pad(b_in_l[GATE_OFFSET:gate_end], (0, LANES - N_GATES)).reshape(1, -1)
    rope = _rope_tables(N_META + seq)
    rope_m = tuple(tab[:N_META] for tab in rope)
    rope_x = tuple(tab[N_META:] for tab in rope)

    ffn1 = functools.partial(_ffn, g_pre=row(ffn1_pre_g), g_post=row(ffn1_post_g),
                             w_gate=bf(ffn1_w_gate), w_up=bf(ffn1_w_up), w_down=bf(ffn1_w_down),
                             tm=512, tf=512)
    inproj = functools.partial(_inproj, g=row(mix_pre_g), w_main=w_main, b_main=b_main,
                               w_gate=w_gate, b_gate=b_gate, tm=1024)
    hx = ffn1(xr)
    hmeta = ffn1(meta)
    zx, gx = inproj(hx, rope_c=rope_x[0], rope_sa=rope_x[1], rope_sb=rope_x[2])
    zm, gm = inproj(hmeta, rope_c=rope_m[0], rope_sa=rope_m[1], rope_sb=rope_m[2])

    hm = _mlstm(zx, gx, zm, gm, m_conv_w[l], row(m_conv_b), row(m_norm_g), t=256)
    ha = _attn(zx, zm, row(a_lambda_q1), row(a_lambda_k1), row(a_lambda_q2), row(a_lambda_k2),
               row(a_norm_g), tq=512)
    h2 = _merge(hm, ha, zx, hx, bf(m_w_branch), bf(a_w_branch), bf(w_out), row(mix_post_g), tm=512)
    out = _ffn(h2, row(ffn2_pre_g), row(ffn2_post_g), bf(ffn2_w_gate), bf(ffn2_w_up),
               bf(ffn2_w_down), tm=512, tf=512)
    return out.reshape(batch, seq, d)
```

```python
import functools

import jax
import jax.numpy as jnp
import numpy as np
from jax import lax
from jax.experimental import pallas as pl
from jax.experimental.pallas import tpu as pltpu

F32 = jnp.float32
BF16 = jnp.bfloat16

EPS = 1e-6
N_META = 16
M_HEADS = 4
M_DQK = 128
M_DV = 256
M_QK = M_HEADS * M_DQK
M_V = M_HEADS * M_DV
CONV_W = 4
A_HEADS = 8
A_DH = 64
A_DV = 2 * A_DH
A_QK = A_HEADS * 2 * A_DH
A_V = A_HEADS * A_DV
ROT_DIM = A_DH // 4
ROPE_THETA = 500000.0
LAMBDA_INIT = 0.8 - 0.6 * 1.0
Q_SCALE = A_DH ** -0.5 * 1.4426950408889634

LANES = 128
GATE_OFFSET = 2 * M_QK + 2 * M_V
N_GATES = 2 * M_HEADS
COL_QK = 0
COL_MV = 2 * M_QK
COL_MO = COL_MV + M_V
COL_AQ = COL_MO + M_V
COL_AK = COL_AQ + A_QK
COL_AV = COL_AK + A_QK
COL_GM = COL_AV + A_V

VMEM_LIMIT = 56 * 1024 * 1024
VMEM_PHYSICAL = 64 * 1024 * 1024


def _params(sem, vmem_limit=VMEM_LIMIT):
    assert vmem_limit < VMEM_PHYSICAL
    return pltpu.CompilerParams(dimension_semantics=sem, vmem_limit_bytes=vmem_limit)


def _rms(x):
    return x * lax.rsqrt(jnp.mean(x * x, axis=-1, keepdims=True) + EPS)


def _ffn_kernel(h_ref, gpre_ref, gpost_ref, wg_ref, wu_ref, wd_ref, o_ref, u_scr):
    f = pl.program_id(1)

    @pl.when(f == 0)
    def _():
        u_scr[...] = (_rms(h_ref[...]) * gpre_ref[...]).astype(BF16)
        o_ref[...] = jnp.zeros_like(o_ref)

    u = u_scr[...]
    g = jnp.dot(u, wg_ref[...], preferred_element_type=F32)
    up = jnp.dot(u, wu_ref[...], preferred_element_type=F32)
    a = (g * jax.nn.sigmoid(g) * up).astype(BF16)
    o_ref[...] += jnp.dot(a, wd_ref[...], preferred_element_type=F32)

    @pl.when(f == pl.num_programs(1) - 1)
    def _():
        o_ref[...] = h_ref[...] + 0.5 * (_rms(o_ref[...]) * gpost_ref[...])


def _ffn(h, g_pre, g_post, w_gate, w_up, w_down, *, tm, tf):
    rows, d = h.shape
    d_ff = w_gate.shape[1]
    tm = min(tm, rows)
    return pl.pallas_call(
        _ffn_kernel,
        out_shape=jax.ShapeDtypeStruct((rows, d), F32),
        grid=(rows // tm, d_ff // tf),
        in_specs=[
            pl.BlockSpec((tm, d), lambda i, f: (i, 0)),
            pl.BlockSpec((1, d), lambda i, f: (0, 0)),
            pl.BlockSpec((1, d), lambda i, f: (0, 0)),
            pl.BlockSpec((d, tf), lambda i, f: (0, f)),
            pl.BlockSpec((d, tf), lambda i, f: (0, f)),
            pl.BlockSpec((tf, d), lambda i, f: (f, 0)),
        ],
        out_specs=pl.BlockSpec((tm, d), lambda i, f: (i, 0)),
        scratch_shapes=[pltpu.VMEM((tm, d), BF16)],
        compiler_params=_params(("parallel", "arbitrary")),
        name="ffn",
    )(h, g_pre, g_post, w_gate, w_up, w_down)


def _rope(z, c, sa, sb):
    outs = []
    for grp in range(z.shape[1] // LANES):
        x = z[:, grp * LANES:(grp + 1) * LANES]
        outs.append(x * c + pltpu.roll(x, LANES - ROT_DIM // 2, 1) * sa
                    + pltpu.roll(x, ROT_DIM // 2, 1) * sb)
    return jnp.concatenate(outs, axis=1)


def _inproj_kernel(h_ref, g_ref, wa_ref, wb_ref, b_ref, wgate_ref, bgate_ref, c_ref, sa_ref, sb_ref,
                   z_ref, gate_ref, u_scr, *, tn):
    j = pl.program_id(1)
    n_a = GATE_OFFSET // tn

    @pl.when(j == 0)
    def _():
        u = (_rms(h_ref[...]) * g_ref[...]).astype(BF16)
        u_scr[...] = u
        gate_ref[...] = jnp.dot(u, wgate_ref[...], preferred_element_type=F32) + bgate_ref[...]

    def project(w_ref):
        return jnp.dot(u_scr[...], w_ref[...], preferred_element_type=F32) + b_ref[...]

    is_q = j == COL_AQ // tn
    is_k = j == COL_AK // tn

    @pl.when(j < n_a)
    def _():
        z_ref[...] = project(wa_ref).astype(BF16)

    @pl.when(is_q)
    def _():
        z = _rope(project(wb_ref), c_ref[...], sa_ref[...], sb_ref[...])
        z_ref[...] = (z * Q_SCALE).astype(BF16)

    @pl.when(is_k)
    def _():
        z_ref[...] = _rope(project(wb_ref), c_ref[...], sa_ref[...], sb_ref[...]).astype(BF16)

    @pl.when(j > COL_AK // tn)
    def _():
        z_ref[...] = project(wb_ref).astype(BF16)


def _inproj(h, g, w_a, w_b, b_main, w_gate, b_gate, rope_c, rope_sa, rope_sb, *, tm):
    rows, d = h.shape
    tn = A_QK
    n_a = w_a.shape[1] // tn
    n = w_a.shape[1] + w_b.shape[1]
    assert COL_AQ == GATE_OFFSET and w_a.shape[1] == GATE_OFFSET
    tm = min(tm, rows)
    return pl.pallas_call(
        functools.partial(_inproj_kernel, tn=tn),
        out_shape=(jax.ShapeDtypeStruct((rows, n), BF16),
                   jax.ShapeDtypeStruct((rows, LANES), F32)),
        grid=(rows // tm, n // tn),
        in_specs=[
            pl.BlockSpec((tm, d), lambda i, j: (i, 0)),
            pl.BlockSpec((1, d), lambda i, j: (0, 0)),
            pl.BlockSpec((d, tn), lambda i, j: (0, jnp.minimum(j, n_a - 1))),
            pl.BlockSpec((d, tn), lambda i, j: (0, jnp.maximum(j - n_a, 0))),
            pl.BlockSpec((1, tn), lambda i, j: (0, j)),
            pl.BlockSpec((d, LANES), lambda i, j: (0, 0)),
            pl.BlockSpec((1, LANES), lambda i, j: (0, 0)),
            pl.BlockSpec((tm, LANES), lambda i, j: (i, 0)),
            pl.BlockSpec((tm, LANES), lambda i, j: (i, 0)),
            pl.BlockSpec((tm, LANES), lambda i, j: (i, 0)),
        ],
        out_specs=(pl.BlockSpec((tm, tn), lambda i, j: (i, j)),
                   pl.BlockSpec((tm, LANES), lambda i, j: (i, 0))),
        scratch_shapes=[pltpu.VMEM((tm, d), BF16)],
        compiler_params=_params(("parallel", "arbitrary")),
        name="inproj",
    )(h, g, w_a, w_b, b_main, w_gate, b_gate, rope_c, rope_sa, rope_sb)


CONV_PAD = 8


def _conv_silu(xbuf, n, cw_ref, cb_ref):
    y = cb_ref[...]
    for i in range(CONV_W):
        start = CONV_PAD + i - (CONV_W - 1)
        y = y + cw_ref[i:i + 1, :] * xbuf[start:start + n, :]
    return y * jax.nn.sigmoid(y)


def _log_sigmoid(x):
    return jnp.minimum(x, 0.0) - jnp.log1p(jnp.exp(-jnp.abs(x)))


def _cumsum_rows(x):
    n = x.shape[0]
    r = lax.broadcasted_iota(jnp.int32, (n, n), 0)
    c = lax.broadcasted_iota(jnp.int32, (n, n), 1)
    tril = (r >= c).astype(F32)
    return jnp.dot(tril, x, preferred_element_type=F32, precision=lax.Precision.HIGHEST)


def _state_update(c_scr, m_scr, hd, kf, vext, li_col, b_col, g_tot, m_prev):
    log_w = (g_tot - b_col) + li_col
    m_new = jnp.maximum(g_tot + m_prev, jnp.max(log_w, axis=0, keepdims=True))
    wk = jnp.exp(log_w - m_new)
    decay = jnp.exp(g_tot + m_prev - m_new)
    kw = (kf * wk).astype(BF16)
    upd = lax.dot_general(kw, vext, (((0,), (0,)), ((), ())), preferred_element_type=F32)
    c_scr[hd] = decay * c_scr[hd] + upd
    m_scr[hd] = jnp.broadcast_to(m_new, m_scr.shape[1:])


def _mlstm_kernel(qk_ref, mv_ref, mo_ref, gt_ref, qkm_ref, mvm_ref, gtm_ref, cw_ref, cb_ref,
                  ng_ref, o_ref, c_scr, m_scr, xbuf):
    i = pl.program_id(0)
    t = qk_ref.shape[0]
    kscale = M_DQK ** -0.5

    @pl.when(i == 0)
    def _():
        nm = qkm_ref.shape[0]
        c_scr[...] = jnp.zeros_like(c_scr)
        m_scr[...] = jnp.zeros_like(m_scr)
        xbuf[0:CONV_PAD, :] = jnp.zeros((CONV_PAD, xbuf.shape[1]), F32)
        xbuf[CONV_PAD:CONV_PAD + nm, :] = qkm_ref[...].astype(F32)
        qk = _conv_silu(xbuf, nm, cw_ref, cb_ref)
        xbuf[0:CONV_PAD, :] = xbuf[nm:nm + CONV_PAD, :]
        gts = gtm_ref[...]
        b_all = _cumsum_rows(_log_sigmoid(gts))
        ones = jnp.ones((nm, LANES), BF16)
        for hd in range(M_HEADS):
            kf = qk[:, M_QK + hd * M_DQK:M_QK + (hd + 1) * M_DQK] * kscale
            vext = jnp.concatenate([mvm_ref[:, hd * M_DV:(hd + 1) * M_DV], ones], axis=1)
            b_col = b_all[:, M_HEADS + hd:M_HEADS + hd + 1]
            _state_update(c_scr, m_scr, hd, kf, vext, gts[:, hd:hd + 1], b_col,
                          b_col[nm - 1:nm, :], m_scr[hd][0:1, 0:1])

    xbuf[CONV_PAD:CONV_PAD + t, :] = qk_ref[...].astype(F32)
    qk = _conv_silu(xbuf, t, cw_ref, cb_ref)
    xbuf[0:CONV_PAD, :] = xbuf[t:t + CONV_PAD, :]

    gts = gt_ref[...]
    b_all = _cumsum_rows(_log_sigmoid(gts))
    lane = lax.broadcasted_iota(jnp.int32, gts.shape, 1)
    rowform = jnp.where(lane < M_HEADS, gts, b_all).T
    r = lax.broadcasted_iota(jnp.int32, (t, t), 0)
    c = lax.broadcasted_iota(jnp.int32, (t, t), 1)
    causal = r >= c
    ones = jnp.ones((t, LANES), BF16)

    for hd in range(M_HEADS):
        qh = qk[:, hd * M_DQK:(hd + 1) * M_DQK].astype(BF16)
        kf = qk[:, M_QK + hd * M_DQK:M_QK + (hd + 1) * M_DQK] * kscale
        kh = kf.astype(BF16)
        vext = jnp.concatenate([mv_ref[:, hd * M_DV:(hd + 1) * M_DV], ones], axis=1)
        li_col = gts[:, hd:hd + 1]
        b_col = b_all[:, M_HEADS + hd:M_HEADS + hd + 1]
        li_row = rowform[hd:hd + 1, :]
        b_row = rowform[M_HEADS + hd:M_HEADS + hd + 1, :]
        g_tot = b_col[t - 1:t, :]
        m_prev = m_scr[hd][0:1, 0:1]

        log_d = jnp.where(causal, (b_col - b_row) + li_row, -jnp.inf)
        m_inter = b_col + m_prev
        m_row = jnp.maximum(m_inter, jnp.max(log_d, axis=1, keepdims=True))
        s = lax.dot_general(qh, kh, (((1,), (1,)), ((), ())), preferred_element_type=F32)
        sd = (s * jnp.exp(log_d - m_row)).astype(BF16)
        inter = jnp.dot(qh, c_scr[hd].astype(BF16), preferred_element_type=F32)
        numden = jnp.exp(m_inter - m_row) * inter + jnp.dot(sd, vext, preferred_element_type=F32)
        den = jnp.maximum(jnp.abs(numden[:, M_DV:M_DV + 1]), jnp.exp(-m_row))
        hh = numden[:, :M_DV] / den
        hn = _rms(hh) * ng_ref[:, hd * M_DV:(hd + 1) * M_DV]
        og = jax.nn.sigmoid(mo_ref[:, hd * M_DV:(hd + 1) * M_DV].astype(F32))
        o_ref[:, hd * M_DV:(hd + 1) * M_DV] = (og * hn).astype(BF16)

        _state_update(c_scr, m_scr, hd, kf, vext, li_col, b_col, g_tot, m_prev)


def _mlstm(zx, gx, zm, gm, conv_w, conv_b, norm_g, *, t):
    rows = zx.shape[0]
    nm = zm.shape[0]
    w_qk = 2 * M_QK
    return pl.pallas_call(
        _mlstm_kernel,
        out_shape=jax.ShapeDtypeStruct((rows, M_V), BF16),
        grid=(rows // t,),
        in_specs=[
            pl.BlockSpec((t, w_qk), lambda i: (i, COL_QK // w_qk)),
            pl.BlockSpec((t, M_V), lambda i: (i, COL_MV // M_V)),
            pl.BlockSpec((t, M_V), lambda i: (i, COL_MO // M_V)),
            pl.BlockSpec((t, LANES), lambda i: (i, 0)),
            pl.BlockSpec((nm, w_qk), lambda i: (0, COL_QK // w_qk)),
            pl.BlockSpec((nm, M_V), lambda i: (0, COL_MV // M_V)),
            pl.BlockSpec((nm, LANES), lambda i: (0, 0)),
            pl.BlockSpec((CONV_W, w_qk), lambda i: (0, 0)),
            pl.BlockSpec((1, w_qk), lambda i: (0, 0)),
            pl.BlockSpec((1, M_V), lambda i: (0, 0)),
        ],
        out_specs=pl.BlockSpec((t, M_V), lambda i: (i, 0)),
        scratch_shapes=[
            pltpu.VMEM((M_HEADS, M_DQK, M_DV + LANES), F32),
            pltpu.VMEM((M_HEADS, 8, LANES), F32),
            pltpu.VMEM((t + CONV_PAD, w_qk), F32),
        ],
        compiler_params=_params(("arbitrary",)),
        name="mlstm",
    )(zx, zx, zx, gx, zm, zm, gm, conv_w, conv_b, norm_g)


ATT_CHUNK = 64


ATT_STRIP = 64


def _attn_kernel(q_ref, k_ref, v_ref, km_ref, vm_ref, lq1_ref, lk1_ref, lq2_ref, lk2_ref,
                 g_ref, o_ref, q2_scr, s0_scr, s1_scr, p0_scr, p1_scr, m_scr, alpha_scr, acc_scr):
    i = pl.program_id(1)
    tq = q_ref.shape[0]
    tk = tq
    rows = 2 * tq
    ncol = tk // LANES
    nt = (((1,), (1,)), ((), ()))
    q = q_ref[...]
    lane = lax.broadcasted_iota(jnp.int32, q.shape, 1)
    zero = jnp.zeros_like(q)
    q2_scr[0:tq, :] = jnp.where(lane < A_DH, q, zero)
    q2_scr[tq:rows, :] = jnp.where(lane >= A_DH, q, zero)
    ones_k = jnp.ones((tk, LANES), BF16)

    def scores(blk):
        off = pl.multiple_of(blk * tk, tk)
        return lax.dot_general(q2_scr[...], k_ref[pl.ds(off, tk), :], nt, preferred_element_type=F32)

    def softmax_pv(s_ref, p_ref, blk, masked):
        for st in range(rows // ATT_STRIP):
            rs = slice(st * ATT_STRIP, (st + 1) * ATT_STRIP)
            if masked:
                qchunk = ((st * ATT_STRIP) % tq) // ATT_CHUNK
                nvis = (qchunk + 1) * ATT_CHUNK
                cvis = lax.broadcasted_iota(jnp.int32, (ATT_STRIP, LANES), 1)
                cols = []
                for c in range(-(-nvis // LANES)):
                    sc = s_ref[rs, c * LANES:(c + 1) * LANES]
                    if (c + 1) * LANES > nvis:
                        sc = jnp.where(cvis < nvis - c * LANES, sc, -jnp.inf)
                    cols.append(sc)
            else:
                cols = [s_ref[rs, c * LANES:(c + 1) * LANES] for c in range(ncol)]
            part = cols[0]
            for sc in cols[1:]:
                part = jnp.maximum(part, sc)
            m_old = m_scr[rs, :]
            m_new = jnp.maximum(m_old, jnp.max(part, axis=1, keepdims=True))
            m_scr[rs, :] = m_new
            alpha_scr[rs, :] = jnp.exp2(m_old - m_new)
            if not masked:
                cols = [s_ref[rs, c * LANES:(c + 1) * LANES] for c in range(ncol)]
            pcols = [jnp.exp2(sc - m_new) for sc in cols]
            pcols += [jnp.zeros_like(pcols[0])] * (ncol - len(cols))
            p_ref[rs, :] = jnp.concatenate(pcols, axis=1).astype(BF16)
        off = pl.multiple_of(blk * tk, tk)
        vext = jnp.concatenate([v_ref[pl.ds(off, tk), :], ones_k], axis=1)
        pv = jnp.dot(p_ref[...], vext, preferred_element_type=F32)
        alpha = alpha_scr[...]
        for c in range(2):
            sl = slice(c * LANES, (c + 1) * LANES)
            acc_scr[:, sl] = alpha * acc_scr[:, sl] + pv[:, sl]

    s = lax.dot_general(q2_scr[...], km_ref[...], nt, preferred_element_type=F32)
    m0 = jnp.max(s, axis=1, keepdims=True)
    p = jnp.exp2(s - m0).astype(BF16)
    vext = jnp.concatenate([vm_ref[...], jnp.ones((vm_ref.shape[0], LANES), BF16)], axis=1)
    acc_scr[...] = jnp.dot(p, vext, preferred_element_type=F32)
    m_scr[...] = jnp.broadcast_to(m0, m_scr.shape)

    s0_scr[...] = scores(0)

    def pair(pr, carry):
        s1_scr[...] = scores(2 * pr + 1)
        softmax_pv(s0_scr, p0_scr, 2 * pr, False)
        s0_scr[...] = scores(2 * pr + 2)
        softmax_pv(s1_scr, p1_scr, 2 * pr + 1, False)
        return carry

    lax.fori_loop(0, i // 2, pair, 0)

    @pl.when(i % 2 == 0)
    def _():
        softmax_pv(s0_scr, p0_scr, i, True)

    @pl.when(i % 2 == 1)
    def _():
        s1_scr[...] = scores(i)
        softmax_pv(s0_scr, p0_scr, i - 1, False)
        softmax_pv(s1_scr, p1_scr, i, True)

    lam = (jnp.exp(jnp.sum(lq1_ref[...] * lk1_ref[...], axis=1, keepdims=True))
           - jnp.exp(jnp.sum(lq2_ref[...] * lk2_ref[...], axis=1, keepdims=True)) + LAMBDA_INIT)
    o = acc_scr[:, 0:A_DV] / acc_scr[:, A_DV:A_DV + 1]
    o = o[:tq] - lam * o[tq:]
    o_ref[...] = (_rms(o) * g_ref[...] * (1.0 - LAMBDA_INIT)).astype(BF16)


def _attn(zx, zm, lq1, lk1, lq2, lk2, norm_g, *, tq):
    rows = zx.shape[0]
    nm = zm.shape[0]
    small = pl.BlockSpec((1, A_DH), lambda h, i: (0, 0))
    return pl.pallas_call(
        _attn_kernel,
        out_shape=jax.ShapeDtypeStruct((rows, A_V), BF16),
        grid=(A_HEADS, rows // tq),
        in_specs=[
            pl.BlockSpec((tq, A_DV), lambda h, i: (i, COL_AQ // A_DV + h)),
            pl.BlockSpec((rows, A_DV), lambda h, i: (0, COL_AK // A_DV + h)),
            pl.BlockSpec((rows, A_DV), lambda h, i: (0, COL_AV // A_DV + h)),
            pl.BlockSpec((nm, A_DV), lambda h, i: (0, COL_AK // A_DV + h)),
            pl.BlockSpec((nm, A_DV), lambda h, i: (0, COL_AV // A_DV + h)),
            small, small, small, small,
            pl.BlockSpec((1, A_DV), lambda h, i: (0, h)),
        ],
        out_specs=pl.BlockSpec((tq, A_DV), lambda h, i: (i, h)),
        scratch_shapes=[
            pltpu.VMEM((2 * tq, A_DV), BF16),
            pltpu.VMEM((2 * tq, tq), F32),
            pltpu.VMEM((2 * tq, tq), F32),
            pltpu.VMEM((2 * tq, tq), BF16),
            pltpu.VMEM((2 * tq, tq), BF16),
            pltpu.VMEM((2 * tq, LANES), F32),
            pltpu.VMEM((2 * tq, LANES), F32),
            pltpu.VMEM((2 * tq, 2 * LANES), F32),
        ],
        compiler_params=_params(("parallel", "arbitrary")),
        name="diffattn",
    )(zx, zx, zx, zm, zm, lq1, lk1, lq2, lk2, norm_g)


def _merge_kernel(hm_ref, ha_ref, gm_ref, ga_ref, h_ref, mw_ref, aw_ref, wo_ref, g_ref, o_ref):
    br_m = jnp.dot(hm_ref[...], mw_ref[...], preferred_element_type=F32)
    br_a = jnp.dot(ha_ref[...], aw_ref[...], preferred_element_type=F32)
    y = (jax.nn.sigmoid(gm_ref[...].astype(F32)) * br_m
         + jax.nn.sigmoid(ga_ref[...].astype(F32)) * br_a)
    out = jnp.dot(y.astype(BF16), wo_ref[...], preferred_element_type=F32)
    o_ref[...] = h_ref[...] + _rms(out) * g_ref[...]


def _merge(hm, ha, zx, h, mw, aw, wo, g_post, *, tm):
    rows, d = h.shape
    const = lambda i: (0, 0)
    return pl.pallas_call(
        _merge_kernel,
        out_shape=jax.ShapeDtypeStruct((rows, d), F32),
        grid=(rows // tm,),
        in_specs=[
            pl.BlockSpec((tm, M_V), lambda i: (i, 0)),
            pl.BlockSpec((tm, A_V), lambda i: (i, 0)),
            pl.BlockSpec((tm, d), lambda i: (i, COL_GM // d)),
            pl.BlockSpec((tm, d), lambda i: (i, COL_GM // d + 1)),
            pl.BlockSpec((tm, d), lambda i: (i, 0)),
            pl.BlockSpec((M_V, d), const),
            pl.BlockSpec((A_V, d), const),
            pl.BlockSpec((d, d), const),
            pl.BlockSpec((1, d), const),
        ],
        out_specs=pl.BlockSpec((tm, d), lambda i: (i, 0)),
        compiler_params=_params(("parallel",)),
        name="merge",
    )(hm, ha, zx, zx, h, mw, aw, wo, g_post)


def _rope_tables(n_rows):
    half = ROT_DIM // 2
    f32 = np.float32
    inv_freq = np.power(f32(ROPE_THETA), -np.arange(0, ROT_DIM, 2, dtype=f32) / f32(ROT_DIM))
    ang = np.arange(n_rows, dtype=f32)[:, None] * inv_freq[None, :]
    cos, sin = np.cos(ang).astype(f32), np.sin(ang).astype(f32)
    zeros = np.zeros((n_rows, A_DH - ROT_DIM), f32)
    zh = np.zeros((n_rows, half), f32)
    c = np.concatenate([cos, cos, zeros + f32(1.0)], axis=1)
    sa = np.concatenate([-sin, zh, zeros], axis=1)
    sb = np.concatenate([zh, sin, zeros], axis=1)
    return tuple(np.concatenate([tab, tab], axis=1) for tab in (c, sa, sb))


def kernel(x, meta, ffn1_pre_g, ffn1_post_g, ffn1_w_gate, ffn1_w_up, ffn1_w_down, mix_pre_g, mix_post_g, w_in, b_in, m_conv_w, m_conv_b, m_norm_g, m_w_branch, a_lambda_q1, a_lambda_k1, a_lambda_q2, a_lambda_k2, a_norm_g, a_w_branch, w_out, ffn2_pre_g, ffn2_post_g, ffn2_w_gate, ffn2_w_up, ffn2_w_down):
    batch, seq, d = x.shape
    assert batch == 1 and meta.shape == (N_META, d)
    assert (2 * d) % A_QK == 0 and COL_GM % d == 0
    l = 0
    xr = x.reshape(seq, d)
    row = lambda v: v[l].reshape(1, -1)
    bf = lambda w: w[l].astype(BF16)

    w_in_l, b_in_l = w_in[l], b_in[l]
    gate_end = GATE_OFFSET + N_GATES
    w_a = w_in_l[:, :GATE_OFFSET].astype(BF16)
    w_b = w_in_l[:, gate_end:].astype(BF16)
    b_main = jnp.concatenate([b_in_l[:GATE_OFFSET], b_in_l[gate_end:]]).reshape(1, -1)
    w_gate = jnp.pad(w_in_l[:, GATE_OFFSET:gate_end], ((0, 0), (0, LANES - N_GATES))).astype(BF16)
    b_gate = jnp.pad(b_in_l[GATE_OFFSET:gate_end], (0, LANES - N_GATES)).reshape(1, -1)
    rope = _rope_tables(N_META + seq)
    rope_m = tuple(tab[:N_META] for tab in rope)
    rope_x = tuple(tab[N_META:] for tab in rope)

    ffn1 = functools.partial(_ffn, g_pre=row(ffn1_pre_g), g_post=row(ffn1_post_g),
                             w_gate=bf(ffn1_w_gate), w_up=bf(ffn1_w_up), w_down=bf(ffn1_w_down),
                             tm=512, tf=512)
    inproj = functools.partial(_inproj, g=row(mix_pre_g), w_a=w_a, w_b=w_b, b_main=b_main,
                               w_gate=w_gate, b_gate=b_gate, tm=1024)
    hx = ffn1(xr)
    hmeta = ffn1(meta)
    zx, gx = inproj(hx, rope_c=rope_x[0], rope_sa=rope_x[1], rope_sb=rope_x[2])
    zm, gm = inproj(hmeta, rope_c=rope_m[0], rope_sa=rope_m[1], rope_sb=rope_m[2])

    hm = _mlstm(zx, gx, zm, gm, m_conv_w[l], row(m_conv_b), row(m_norm_g), t=256)
    ha = _attn(zx, zm, row(a_lambda_q1), row(a_lambda_k1), row(a_lambda_q2), row(a_lambda_k2),
               row(a_norm_g), tq=512)
    h2 = _merge(hm, ha, zx, hx, bf(m_w_branch), bf(a_w_branch), bf(w_out), row(mix_post_g), tm=512)
    out = _ffn(h2, row(ffn2_pre_g), row(ffn2_post_g), bf(ffn2_w_gate), bf(ffn2_w_up),
               bf(ffn2_w_down), tm=512, tf=512)
    return out.reshape(batch, seq, d)
```

```python
import functools

import jax
import jax.numpy as jnp
import numpy as np
from jax import lax
from jax.experimental import pallas as pl
from jax.experimental.pallas import tpu as pltpu

F32 = jnp.float32
BF16 = jnp.bfloat16

EPS = 1e-6
N_META = 16
M_HEADS = 4
M_DQK = 128
M_DV = 256
M_QK = M_HEADS * M_DQK
M_V = M_HEADS * M_DV
CONV_W = 4
A_HEADS = 8
A_DH = 64
A_DV = 2 * A_DH
A_QK = A_HEADS * 2 * A_DH
A_V = A_HEADS * A_DV
ROT_DIM = A_DH // 4
ROPE_THETA = 500000.0
LAMBDA_INIT = 0.8 - 0.6 * 1.0
Q_SCALE = A_DH ** -0.5 * 1.4426950408889634

LANES = 128
GATE_OFFSET = 2 * M_QK + 2 * M_V
N_GATES = 2 * M_HEADS
COL_QK = 0
COL_MV = 2 * M_QK
COL_MO = COL_MV + M_V
COL_AQ = COL_MO + M_V
COL_AK = COL_AQ + A_QK
COL_AV = COL_AK + A_QK
COL_GM = COL_AV + A_V

VMEM_LIMIT = 56 * 1024 * 1024
VMEM_PHYSICAL = 64 * 1024 * 1024


def _params(sem, vmem_limit=VMEM_LIMIT):
    assert vmem_limit < VMEM_PHYSICAL
    return pltpu.CompilerParams(dimension_semantics=sem, vmem_limit_bytes=vmem_limit)


def _rms(x):
    return x * lax.rsqrt(jnp.mean(x * x, axis=-1, keepdims=True) + EPS)


def _ffn_kernel(h_ref, gpre_ref, gpost_ref, wg_ref, wu_ref, wd_ref, *rest, emit_norm):
    if emit_norm:
        gnext_ref, o_ref, un_ref, u_scr = rest
    else:
        o_ref, u_scr = rest
    f = pl.program_id(1)

    @pl.when(f == 0)
    def _():
        u_scr[...] = (_rms(h_ref[...]) * gpre_ref[...]).astype(BF16)
        o_ref[...] = jnp.zeros_like(o_ref)

    u = u_scr[...]
    g = jnp.dot(u, wg_ref[...], preferred_element_type=F32)
    up = jnp.dot(u, wu_ref[...], preferred_element_type=F32)
    a = (g * jax.nn.sigmoid(g) * up).astype(BF16)
    o_ref[...] += jnp.dot(a, wd_ref[...], preferred_element_type=F32)

    @pl.when(f == pl.num_programs(1) - 1)
    def _():
        out = h_ref[...] + 0.5 * (_rms(o_ref[...]) * gpost_ref[...])
        o_ref[...] = out
        if emit_norm:
            un_ref[...] = (_rms(out) * gnext_ref[...]).astype(BF16)


def _ffn(h, g_pre, g_post, w_gate, w_up, w_down, g_next=None, *, tm, tf):
    rows, d = h.shape
    d_ff = w_gate.shape[1]
    tm = min(tm, rows)
    emit_norm = g_next is not None
    vec = pl.BlockSpec((1, d), lambda i, f: (0, 0))
    tile = pl.BlockSpec((tm, d), lambda i, f: (i, 0))
    in_specs = [tile, vec, vec,
                pl.BlockSpec((d, tf), lambda i, f: (0, f)),
                pl.BlockSpec((d, tf), lambda i, f: (0, f)),
                pl.BlockSpec((tf, d), lambda i, f: (f, 0))]
    args = [h, g_pre, g_post, w_gate, w_up, w_down]
    out_shape = jax.ShapeDtypeStruct((rows, d), F32)
    out_specs = tile
    if emit_norm:
        in_specs.append(vec)
        args.append(g_next)
        out_shape = (out_shape, jax.ShapeDtypeStruct((rows, d), BF16))
        out_specs = (tile, tile)
    return pl.pallas_call(
        functools.partial(_ffn_kernel, emit_norm=emit_norm),
        out_shape=out_shape,
        grid=(rows // tm, d_ff // tf),
        in_specs=in_specs,
        out_specs=out_specs,
        scratch_shapes=[pltpu.VMEM((tm, d), BF16)],
        compiler_params=_params(("parallel", "arbitrary")),
        name="ffn",
    )(*args)


def _rope(z, c, sa, sb):
    outs = []
    for grp in range(z.shape[1] // LANES):
        x = z[:, grp * LANES:(grp + 1) * LANES]
        outs.append(x * c + pltpu.roll(x, LANES - ROT_DIM // 2, 1) * sa
                    + pltpu.roll(x, ROT_DIM // 2, 1) * sb)
    return jnp.concatenate(outs, axis=1)


_NT = (((1,), (1,)), ((), ()))


def _inproj_kernel(u_ref, wt_ref, b_ref, wgate_ref, bgate_ref, c_ref, sa_ref, sb_ref,
                   z_ref, gate_ref, *, tn):
    j = pl.program_id(1)

    @pl.when(j == 0)
    def _():
        wg = wgate_ref[...].astype(BF16)
        wg = jnp.concatenate([wg, jnp.zeros((LANES - N_GATES, wg.shape[1]), BF16)], axis=0)
        gate_ref[...] = (lax.dot_general(u_ref[...], wg, _NT, preferred_element_type=F32)
                         + bgate_ref[...])

    def project():
        w = wt_ref[...].astype(BF16)
        return lax.dot_general(u_ref[...], w, _NT, preferred_element_type=F32) + b_ref[...]

    is_q = j == COL_AQ // tn
    is_k = j == COL_AK // tn

    @pl.when(is_q)
    def _():
        z = _rope(project(), c_ref[...], sa_ref[...], sb_ref[...])
        z_ref[...] = (z * Q_SCALE).astype(BF16)

    @pl.when(is_k)
    def _():
        z_ref[...] = _rope(project(), c_ref[...], sa_ref[...], sb_ref[...]).astype(BF16)

    @pl.when(jnp.logical_not(jnp.logical_or(is_q, is_k)))
    def _():
        z_ref[...] = project().astype(BF16)


def _inproj(u, w_t, b_main, b_gate, rope_c, rope_sa, rope_sb, *, tm):
    rows, d = u.shape
    tn = A_QK
    n_a = GATE_OFFSET // tn
    n = w_t.shape[0] - N_GATES
    assert COL_AQ == GATE_OFFSET and GATE_OFFSET % tn == 0 and n % tn == 0
    tm = min(tm, rows)

    def w_row(i, j):
        return (pl.multiple_of(jnp.where(j < n_a, j * tn, j * tn + N_GATES), N_GATES), 0)

    return pl.pallas_call(
        functools.partial(_inproj_kernel, tn=tn),
        out_shape=(jax.ShapeDtypeStruct((rows, n), BF16),
                   jax.ShapeDtypeStruct((rows, LANES), F32)),
        grid=(rows // tm, n // tn),
        in_specs=[
            pl.BlockSpec((tm, d), lambda i, j: (i, 0)),
            pl.BlockSpec((pl.Element(tn), pl.Element(d)), w_row),
            pl.BlockSpec((1, tn), lambda i, j: (0, j)),
            pl.BlockSpec((N_GATES, d), lambda i, j: (GATE_OFFSET // N_GATES, 0)),
            pl.BlockSpec((1, LANES), lambda i, j: (0, 0)),
            pl.BlockSpec((tm, LANES), lambda i, j: (i, 0)),
            pl.BlockSpec((tm, LANES), lambda i, j: (i, 0)),
            pl.BlockSpec((tm, LANES), lambda i, j: (i, 0)),
        ],
        out_specs=(pl.BlockSpec((tm, tn), lambda i, j: (i, j)),
                   pl.BlockSpec((tm, LANES), lambda i, j: (i, 0))),
        compiler_params=_params(("parallel", "arbitrary")),
        name="inproj",
    )(u, w_t, b_main, w_t, b_gate, rope_c, rope_sa, rope_sb)


CONV_PAD = 8


def _conv_silu(xbuf, n, cw_ref, cb_ref):
    y = cb_ref[...]
    for i in range(CONV_W):
        start = CONV_PAD + i - (CONV_W - 1)
        y = y + cw_ref[i:i + 1, :] * xbuf[start:start + n, :]
    return y * jax.nn.sigmoid(y)


def _log_sigmoid(x):
    return jnp.minimum(x, 0.0) - jnp.log1p(jnp.exp(-jnp.abs(x)))


def _cumsum_rows(x):
    n = x.shape[0]
    r = lax.broadcasted_iota(jnp.int32, (n, n), 0)
    c = lax.broadcasted_iota(jnp.int32, (n, n), 1)
    tril = (r >= c).astype(F32)
    return jnp.dot(tril, x, preferred_element_type=F32, precision=lax.Precision.HIGHEST)


def _state_update(c_scr, m_scr, hd, kf, vext, li_col, b_col, g_tot, m_prev):
    log_w = (g_tot - b_col) + li_col
    m_new = jnp.maximum(g_tot + m_prev, jnp.max(log_w, axis=0, keepdims=True))
    wk = jnp.exp(log_w - m_new)
    decay = jnp.exp(g_tot + m_prev - m_new)
    kw = (kf * wk).astype(BF16)
    upd = lax.dot_general(kw, vext, (((0,), (0,)), ((), ())), preferred_element_type=F32)
    c_scr[hd] = decay * c_scr[hd] + upd
    m_scr[hd] = jnp.broadcast_to(m_new, m_scr.shape[1:])


def _mlstm_kernel(qk_ref, mv_ref, mo_ref, gt_ref, qkm_ref, mvm_ref, gtm_ref, cw_ref, cb_ref,
                  ng_ref, o_ref, c_scr, m_scr, xbuf):
    i = pl.program_id(0)
    t = qk_ref.shape[0]
    kscale = M_DQK ** -0.5

    @pl.when(i == 0)
    def _():
        nm = qkm_ref.shape[0]
        c_scr[...] = jnp.zeros_like(c_scr)
        m_scr[...] = jnp.zeros_like(m_scr)
        xbuf[0:CONV_PAD, :] = jnp.zeros((CONV_PAD, xbuf.shape[1]), F32)
        xbuf[CONV_PAD:CONV_PAD + nm, :] = qkm_ref[...].astype(F32)
        qk = _conv_silu(xbuf, nm, cw_ref, cb_ref)
        xbuf[0:CONV_PAD, :] = xbuf[nm:nm + CONV_PAD, :]
        gts = gtm_ref[...]
        b_all = _cumsum_rows(_log_sigmoid(gts))
        ones = jnp.ones((nm, LANES), BF16)
        for hd in range(M_HEADS):
            kf = qk[:, M_QK + hd * M_DQK:M_QK + (hd + 1) * M_DQK] * kscale
            vext = jnp.concatenate([mvm_ref[:, hd * M_DV:(hd + 1) * M_DV], ones], axis=1)
            b_col = b_all[:, M_HEADS + hd:M_HEADS + hd + 1]
            _state_update(c_scr, m_scr, hd, kf, vext, gts[:, hd:hd + 1], b_col,
                          b_col[nm - 1:nm, :], m_scr[hd][0:1, 0:1])

    xbuf[CONV_PAD:CONV_PAD + t, :] = qk_ref[...].astype(F32)
    qk = _conv_silu(xbuf, t, cw_ref, cb_ref)
    xbuf[0:CONV_PAD, :] = xbuf[t:t + CONV_PAD, :]

    gts = gt_ref[...]
    b_all = _cumsum_rows(_log_sigmoid(gts))
    lane = lax.broadcasted_iota(jnp.int32, gts.shape, 1)
    rowform = jnp.where(lane < M_HEADS, gts, b_all).T
    r = lax.broadcasted_iota(jnp.int32, (t, t), 0)
    c = lax.broadcasted_iota(jnp.int32, (t, t), 1)
    causal = r >= c
    ones = jnp.ones((t, LANES), BF16)

    for hd in range(M_HEADS):
        qh = qk[:, hd * M_DQK:(hd + 1) * M_DQK].astype(BF16)
        kf = qk[:, M_QK + hd * M_DQK:M_QK + (hd + 1) * M_DQK] * kscale
        kh = kf.astype(BF16)
        vext = jnp.concatenate([mv_ref[:, hd * M_DV:(hd + 1) * M_DV], ones], axis=1)
        li_col = gts[:, hd:hd + 1]
        b_col = b_all[:, M_HEADS + hd:M_HEADS + hd + 1]
        li_row = rowform[hd:hd + 1, :]
        b_row = rowform[M_HEADS + hd:M_HEADS + hd + 1, :]
        g_tot = b_col[t - 1:t, :]
        m_prev = m_scr[hd][0:1, 0:1]

        log_d = jnp.where(causal, (b_col - b_row) + li_row, -jnp.inf)
        m_inter = b_col + m_prev
        m_row = jnp.maximum(m_inter, jnp.max(log_d, axis=1, keepdims=True))
        s = lax.dot_general(qh, kh, (((1,), (1,)), ((), ())), preferred_element_type=F32)
        sd = (s * jnp.exp(log_d - m_row)).astype(BF16)
        inter = jnp.dot(qh, c_scr[hd].astype(BF16), preferred_element_type=F32)
        numden = jnp.exp(m_inter - m_row) * inter + jnp.dot(sd, vext, preferred_element_type=F32)
        den = jnp.maximum(jnp.abs(numden[:, M_DV:M_DV + 1]), jnp.exp(-m_row))
        hh = numden[:, :M_DV] / den
        hn = _rms(hh) * ng_ref[:, hd * M_DV:(hd + 1) * M_DV]
        og = jax.nn.sigmoid(mo_ref[:, hd * M_DV:(hd + 1) * M_DV].astype(F32))
        o_ref[:, hd * M_DV:(hd + 1) * M_DV] = (og * hn).astype(BF16)

        _state_update(c_scr, m_scr, hd, kf, vext, li_col, b_col, g_tot, m_prev)


def _mlstm(zx, gx, zm, gm, conv_w, conv_b, norm_g, *, t):
    rows = zx.shape[0]
    nm = zm.shape[0]
    w_qk = 2 * M_QK
    return pl.pallas_call(
        _mlstm_kernel,
        out_shape=jax.ShapeDtypeStruct((rows, M_V), BF16),
        grid=(rows // t,),
        in_specs=[
            pl.BlockSpec((t, w_qk), lambda i: (i, COL_QK // w_qk)),
            pl.BlockSpec((t, M_V), lambda i: (i, COL_MV // M_V)),
            pl.BlockSpec((t, M_V), lambda i: (i, COL_MO // M_V)),
            pl.BlockSpec((t, LANES), lambda i: (i, 0)),
            pl.BlockSpec((nm, w_qk), lambda i: (0, COL_QK // w_qk)),
            pl.BlockSpec((nm, M_V), lambda i: (0, COL_MV // M_V)),
            pl.BlockSpec((nm, LANES), lambda i: (0, 0)),
            pl.BlockSpec((CONV_W, w_qk), lambda i: (0, 0)),
            pl.BlockSpec((1, w_qk), lambda i: (0, 0)),
            pl.BlockSpec((1, M_V), lambda i: (0, 0)),
        ],
        out_specs=pl.BlockSpec((t, M_V), lambda i: (i, 0)),
        scratch_shapes=[
            pltpu.VMEM((M_HEADS, M_DQK, M_DV + LANES), F32),
            pltpu.VMEM((M_HEADS, 8, LANES), F32),
            pltpu.VMEM((t + CONV_PAD, w_qk), F32),
        ],
        compiler_params=_params(("arbitrary",)),
        name="mlstm",
    )(zx, zx, zx, gx, zm, zm, gm, conv_w, conv_b, norm_g)


ATT_CHUNK = 64


ATT_STRIP = 64


def _attn_kernel(q_ref, k_ref, v_ref, km_ref, vm_ref, lq1_ref, lk1_ref, lq2_ref, lk2_ref,
                 g_ref, o_ref, q2_scr, s0_scr, s1_scr, p0_scr, p1_scr, m_scr, alpha_scr, acc_scr):
    i = pl.program_id(1)
    tq = q_ref.shape[0]
    tk = tq
    rows = 2 * tq
    ncol = tk // LANES
    nt = (((1,), (1,)), ((), ()))
    q = q_ref[...]
    lane = lax.broadcasted_iota(jnp.int32, q.shape, 1)
    zero = jnp.zeros_like(q)
    q2_scr[0:tq, :] = jnp.where(lane < A_DH, q, zero)
    q2_scr[tq:rows, :] = jnp.where(lane >= A_DH, q, zero)
    ones_k = jnp.ones((tk, LANES), BF16)

    def scores(blk):
        off = pl.multiple_of(blk * tk, tk)
        return lax.dot_general(q2_scr[...], k_ref[pl.ds(off, tk), :], nt, preferred_element_type=F32)

    def softmax_pv(s_ref, p_ref, blk, masked):
        for st in range(rows // ATT_STRIP):
            rs = slice(st * ATT_STRIP, (st + 1) * ATT_STRIP)
            if masked:
                qchunk = ((st * ATT_STRIP) % tq) // ATT_CHUNK
                nvis = (qchunk + 1) * ATT_CHUNK
                cvis = lax.broadcasted_iota(jnp.int32, (ATT_STRIP, LANES), 1)
                cols = []
                for c in range(-(-nvis // LANES)):
                    sc = s_ref[rs, c * LANES:(c + 1) * LANES]
                    if (c + 1) * LANES > nvis:
                        sc = jnp.where(cvis < nvis - c * LANES, sc, -jnp.inf)
                    cols.append(sc)
            else:
                cols = [s_ref[rs, c * LANES:(c + 1) * LANES] for c in range(ncol)]
            part = cols[0]
            for sc in cols[1:]:
                part = jnp.maximum(part, sc)
            m_old = m_scr[rs, :]
            m_new = jnp.maximum(m_old, jnp.max(part, axis=1, keepdims=True))
            m_scr[rs, :] = m_new
            alpha_scr[rs, :] = jnp.exp2(m_old - m_new)
            if not masked:
                cols = [s_ref[rs, c * LANES:(c + 1) * LANES] for c in range(ncol)]
            pcols = [jnp.exp2(sc - m_new) for sc in cols]
            pcols += [jnp.zeros_like(pcols[0])] * (ncol - len(cols))
            p_ref[rs, :] = jnp.concatenate(pcols, axis=1).astype(BF16)
        off = pl.multiple_of(blk * tk, tk)
        vext = jnp.concatenate([v_ref[pl.ds(off, tk), :], ones_k], axis=1)
        pv = jnp.dot(p_ref[...], vext, preferred_element_type=F32)
        alpha = alpha_scr[...]
        for c in range(2):
            sl = slice(c * LANES, (c + 1) * LANES)
            acc_scr[:, sl] = alpha * acc_scr[:, sl] + pv[:, sl]

    s = lax.dot_general(q2_scr[...], km_ref[...], nt, preferred_element_type=F32)
    m0 = jnp.max(s, axis=1, keepdims=True)
    p = jnp.exp2(s - m0).astype(BF16)
    vext = jnp.concatenate([vm_ref[...], jnp.ones((vm_ref.shape[0], LANES), BF16)], axis=1)
    acc_scr[...] = jnp.dot(p, vext, preferred_element_type=F32)
    m_scr[...] = jnp.broadcast_to(m0, m_scr.shape)

    s0_scr[...] = scores(0)

    def pair(pr, carry):
        s1_scr[...] = scores(2 * pr + 1)
        softmax_pv(s0_scr, p0_scr, 2 * pr, False)
        s0_scr[...] = scores(2 * pr + 2)
        softmax_pv(s1_scr, p1_scr, 2 * pr + 1, False)
        return carry

    lax.fori_loop(0, i // 2, pair, 0)

    @pl.when(i % 2 == 0)
    def _():
        softmax_pv(s0_scr, p0_scr, i, True)

    @pl.when(i % 2 == 1)
    def _():
        s1_scr[...] = scores(i)
        softmax_pv(s0_scr, p0_scr, i - 1, False)
        softmax_pv(s1_scr, p1_scr, i, True)

    lam = (jnp.exp(jnp.sum(lq1_ref[...] * lk1_ref[...], axis=1, keepdims=True))
           - jnp.exp(jnp.sum(lq2_ref[...] * lk2_ref[...], axis=1, keepdims=True)) + LAMBDA_INIT)
    o = acc_scr[:, 0:A_DV] / acc_scr[:, A_DV:A_DV + 1]
    o = o[:tq] - lam * o[tq:]
    o_ref[...] = (_rms(o) * g_ref[...] * (1.0 - LAMBDA_INIT)).astype(BF16)


def _attn(zx, zm, lq1, lk1, lq2, lk2, norm_g, *, tq):
    rows = zx.shape[0]
    nm = zm.shape[0]
    small = pl.BlockSpec((1, A_DH), lambda h, i: (0, 0))
    return pl.pallas_call(
        _attn_kernel,
        out_shape=jax.ShapeDtypeStruct((rows, A_V), BF16),
        grid=(A_HEADS, rows // tq),
        in_specs=[
            pl.BlockSpec((tq, A_DV), lambda h, i: (i, COL_AQ // A_DV + h)),
            pl.BlockSpec((rows, A_DV), lambda h, i: (0, COL_AK // A_DV + h)),
            pl.BlockSpec((rows, A_DV), lambda h, i: (0, COL_AV // A_DV + h)),
            pl.BlockSpec((nm, A_DV), lambda h, i: (0, COL_AK // A_DV + h)),
            pl.BlockSpec((nm, A_DV), lambda h, i: (0, COL_AV // A_DV + h)),
            small, small, small, small,
            pl.BlockSpec((1, A_DV), lambda h, i: (0, h)),
        ],
        out_specs=pl.BlockSpec((tq, A_DV), lambda h, i: (i, h)),
        scratch_shapes=[
            pltpu.VMEM((2 * tq, A_DV), BF16),
            pltpu.VMEM((2 * tq, tq), F32),
            pltpu.VMEM((2 * tq, tq), F32),
            pltpu.VMEM((2 * tq, tq), BF16),
            pltpu.VMEM((2 * tq, tq), BF16),
            pltpu.VMEM((2 * tq, LANES), F32),
            pltpu.VMEM((2 * tq, LANES), F32),
            pltpu.VMEM((2 * tq, 2 * LANES), F32),
        ],
        compiler_params=_params(("parallel", "arbitrary")),
        name="diffattn",
    )(zx, zx, zx, zm, zm, lq1, lk1, lq2, lk2, norm_g)


def _merge_kernel(hm_ref, ha_ref, gm_ref, ga_ref, h_ref, mw_ref, aw_ref, wo_ref, g_ref, o_ref):
    br_m = jnp.dot(hm_ref[...], mw_ref[...], preferred_element_type=F32)
    br_a = jnp.dot(ha_ref[...], aw_ref[...], preferred_element_type=F32)
    y = (jax.nn.sigmoid(gm_ref[...].astype(F32)) * br_m
         + jax.nn.sigmoid(ga_ref[...].astype(F32)) * br_a)
    out = jnp.dot(y.astype(BF16), wo_ref[...], preferred_element_type=F32)
    o_ref[...] = h_ref[...] + _rms(out) * g_ref[...]


def _merge(hm, ha, zx, h, mw, aw, wo, g_post, *, tm):
    rows, d = h.shape
    const = lambda i: (0, 0)
    return pl.pallas_call(
        _merge_kernel,
        out_shape=jax.ShapeDtypeStruct((rows, d), F32),
        grid=(rows // tm,),
        in_specs=[
            pl.BlockSpec((tm, M_V), lambda i: (i, 0)),
            pl.BlockSpec((tm, A_V), lambda i: (i, 0)),
            pl.BlockSpec((tm, d), lambda i: (i, COL_GM // d)),
            pl.BlockSpec((tm, d), lambda i: (i, COL_GM // d + 1)),
            pl.BlockSpec((tm, d), lambda i: (i, 0)),
            pl.BlockSpec((M_V, d), const),
            pl.BlockSpec((A_V, d), const),
            pl.BlockSpec((d, d), const),
            pl.BlockSpec((1, d), const),
        ],
        out_specs=pl.BlockSpec((tm, d), lambda i: (i, 0)),
        compiler_params=_params(("parallel",)),
        name="merge",
    )(hm, ha, zx, zx, h, mw, aw, wo, g_post)


def _rope_tables(n_rows):
    half = ROT_DIM // 2
    f32 = np.float32
    inv_freq = np.power(f32(ROPE_THETA), -np.arange(0, ROT_DIM, 2, dtype=f32) / f32(ROT_DIM))
    ang = np.arange(n_rows, dtype=f32)[:, None] * inv_freq[None, :]
    cos, sin = np.cos(ang).astype(f32), np.sin(ang).astype(f32)
    zeros = np.zeros((n_rows, A_DH - ROT_DIM), f32)
    zh = np.zeros((n_rows, half), f32)
    c = np.concatenate([cos, cos, zeros + f32(1.0)], axis=1)
    sa = np.concatenate([-sin, zh, zeros], axis=1)
    sb = np.concatenate([zh, sin, zeros], axis=1)
    return tuple(np.concatenate([tab, tab], axis=1) for tab in (c, sa, sb))


def kernel(x, meta, ffn1_pre_g, ffn1_post_g, ffn1_w_gate, ffn1_w_up, ffn1_w_down, mix_pre_g, mix_post_g, w_in, b_in, m_conv_w, m_conv_b, m_norm_g, m_w_branch, a_lambda_q1, a_lambda_k1, a_lambda_q2, a_lambda_k2, a_norm_g, a_w_branch, w_out, ffn2_pre_g, ffn2_post_g, ffn2_w_gate, ffn2_w_up, ffn2_w_down):
    batch, seq, d = x.shape
    assert batch == 1 and meta.shape == (N_META, d)
    assert (2 * d) % A_QK == 0 and COL_GM % d == 0
    l = 0
    xr = x.reshape(seq, d)
    row = lambda v: v[l].reshape(1, -1)
    bf = lambda w: w[l].astype(BF16)

    w_in_l, b_in_l = w_in[l], b_in[l]
    gate_end = GATE_OFFSET + N_GATES
    w_in_t = w_in_l.T
    b_main = jnp.concatenate([b_in_l[:GATE_OFFSET], b_in_l[gate_end:]]).reshape(1, -1)
    b_gate = jnp.pad(b_in_l[GATE_OFFSET:gate_end], (0, LANES - N_GATES)).reshape(1, -1)
    rope = _rope_tables(N_META + seq)
    rope_m = tuple(tab[:N_META] for tab in rope)
    rope_x = tuple(tab[N_META:] for tab in rope)

    ffn1 = functools.partial(_ffn, g_pre=row(ffn1_pre_g), g_post=row(ffn1_post_g),
                             w_gate=bf(ffn1_w_gate), w_up=bf(ffn1_w_up), w_down=bf(ffn1_w_down),
                             g_next=row(mix_pre_g), tm=512, tf=512)
    inproj = functools.partial(_inproj, w_t=w_in_t, b_main=b_main, b_gate=b_gate, tm=1024)
    hx, ux = ffn1(xr)
    _, umeta = ffn1(meta)
    zx, gx = inproj(ux, rope_c=rope_x[0], rope_sa=rope_x[1], rope_sb=rope_x[2])
    zm, gm = inproj(umeta, rope_c=rope_m[0], rope_sa=rope_m[1], rope_sb=rope_m[2])

    hm = _mlstm(zx, gx, zm, gm, m_conv_w[l], row(m_conv_b), row(m_norm_g), t=256)
    ha = _attn(zx, zm, row(a_lambda_q1), row(a_lambda_k1), row(a_lambda_q2), row(a_lambda_k2),
               row(a_norm_g), tq=512)
    h2 = _merge(hm, ha, zx, hx, bf(m_w_branch), bf(a_w_branch), bf(w_out), row(mix_post_g), tm=512)
    out = _ffn(h2, row(ffn2_pre_g), row(ffn2_post_g), bf(ffn2_w_gate), bf(ffn2_w_up),
               bf(ffn2_w_down), tm=512, tf=512)
    return out.reshape(batch, seq, d)
```

```python
import functools

import jax
import jax.numpy as jnp
import numpy as np
from jax import lax
from jax.experimental import pallas as pl
from jax.experimental.pallas import tpu as pltpu

F32 = jnp.float32
BF16 = jnp.bfloat16

EPS = 1e-6
N_META = 16
M_HEADS = 4
M_DQK = 128
M_DV = 256
M_QK = M_HEADS * M_DQK
M_V = M_HEADS * M_DV
CONV_W = 4
A_HEADS = 8
A_DH = 64
A_DV = 2 * A_DH
A_QK = A_HEADS * 2 * A_DH
A_V = A_HEADS * A_DV
ROT_DIM = A_DH // 4
ROPE_THETA = 500000.0
LAMBDA_INIT = 0.8 - 0.6 * 1.0
Q_SCALE = A_DH ** -0.5 * 1.4426950408889634

LANES = 128
GATE_OFFSET = 2 * M_QK + 2 * M_V
N_GATES = 2 * M_HEADS
COL_QK = 0
COL_MV = 2 * M_QK
COL_MO = COL_MV + M_V
COL_AQ = COL_MO + M_V
COL_AK = COL_AQ + A_QK
COL_AV = COL_AK + A_QK
COL_GM = COL_AV + A_V

VMEM_LIMIT = 56 * 1024 * 1024
VMEM_PHYSICAL = 64 * 1024 * 1024


def _params(sem, vmem_limit=VMEM_LIMIT):
    assert vmem_limit < VMEM_PHYSICAL
    return pltpu.CompilerParams(dimension_semantics=sem, vmem_limit_bytes=vmem_limit)


def _rms(x):
    return x * lax.rsqrt(jnp.mean(x * x, axis=-1, keepdims=True) + EPS)


FFN_TM = 1024
FFN_TF = 256
FFN_VMEM_LIMIT = 62 * 1024 * 1024


def _ffn_kernel(h_ref, gpre_ref, gpost_ref, wg_ref, wu_ref, wd_ref, *rest, emit_norm):
    if emit_norm:
        gnext_ref, o_ref, un_ref, u_scr = rest
    else:
        o_ref, u_scr = rest
    f = pl.program_id(1)

    @pl.when(f == 0)
    def _():
        u_scr[...] = (_rms(h_ref[...]) * gpre_ref[...]).astype(BF16)
        o_ref[...] = jnp.zeros_like(o_ref)

    u = u_scr[...]
    g = jnp.dot(u, wg_ref[...].astype(BF16), preferred_element_type=F32)
    up = jnp.dot(u, wu_ref[...].astype(BF16), preferred_element_type=F32)
    a = (g * jax.nn.sigmoid(g) * up).astype(BF16)
    o_ref[...] += jnp.dot(a, wd_ref[...].astype(BF16), preferred_element_type=F32)

    @pl.when(f == pl.num_programs(1) - 1)
    def _():
        out = h_ref[...] + 0.5 * (_rms(o_ref[...]) * gpost_ref[...])
        o_ref[...] = out
        if emit_norm:
            un_ref[...] = (_rms(out) * gnext_ref[...]).astype(BF16)


def _ffn(h, g_pre, g_post, w_gate, w_up, w_down, g_next=None, *, layer, tm, tf):
    rows, d = h.shape
    d_ff = w_gate.shape[2]
    tm = min(tm, rows)
    emit_norm = g_next is not None
    vec = pl.BlockSpec((1, d), lambda i, f: (0, 0))
    tile = pl.BlockSpec((tm, d), lambda i, f: (i, 0))
    in_specs = [tile, vec, vec,
                pl.BlockSpec((None, d, tf), lambda i, f: (layer, 0, f)),
                pl.BlockSpec((None, d, tf), lambda i, f: (layer, 0, f)),
                pl.BlockSpec((None, tf, d), lambda i, f: (layer, f, 0))]
    args = [h, g_pre, g_post, w_gate, w_up, w_down]
    out_shape = jax.ShapeDtypeStruct((rows, d), F32)
    out_specs = tile
    if emit_norm:
        in_specs.append(vec)
        args.append(g_next)
        out_shape = (out_shape, jax.ShapeDtypeStruct((rows, d), BF16))
        out_specs = (tile, tile)
    return pl.pallas_call(
        functools.partial(_ffn_kernel, emit_norm=emit_norm),
        out_shape=out_shape,
        grid=(rows // tm, d_ff // tf),
        in_specs=in_specs,
        out_specs=out_specs,
        scratch_shapes=[pltpu.VMEM((tm, d), BF16)],
        compiler_params=_params(("parallel", "arbitrary"), vmem_limit=FFN_VMEM_LIMIT),
        name="ffn",
    )(*args)


def _rope(z, c, sa, sb):
    outs = []
    for grp in range(z.shape[1] // LANES):
        x = z[:, grp * LANES:(grp + 1) * LANES]
        outs.append(x * c + pltpu.roll(x, LANES - ROT_DIM // 2, 1) * sa
                    + pltpu.roll(x, ROT_DIM // 2, 1) * sb)
    return jnp.concatenate(outs, axis=1)


_NT = (((1,), (1,)), ((), ()))


def _inproj_kernel(h_ref, g_ref, wt_ref, b_ref, wgate_ref, bgate_ref, c_ref, sa_ref, sb_ref,
                   z_ref, gate_ref, u_scr, *, tn):
    j = pl.program_id(1)

    @pl.when(j == 0)
    def _():
        u = (_rms(h_ref[...]) * g_ref[...]).astype(BF16)
        u_scr[...] = u
        wg = wgate_ref[...].astype(BF16)
        wg = jnp.concatenate([wg, jnp.zeros((LANES - N_GATES, wg.shape[1]), BF16)], axis=0)
        gate_ref[...] = lax.dot_general(u, wg, _NT, preferred_element_type=F32) + bgate_ref[...]

    def project():
        w = wt_ref[...].astype(BF16)
        return lax.dot_general(u_scr[...], w, _NT, preferred_element_type=F32) + b_ref[...]

    is_q = j == COL_AQ // tn
    is_k = j == COL_AK // tn

    @pl.when(is_q)
    def _():
        z = _rope(project(), c_ref[...], sa_ref[...], sb_ref[...])
        z_ref[...] = (z * Q_SCALE).astype(BF16)

    @pl.when(is_k)
    def _():
        z_ref[...] = _rope(project(), c_ref[...], sa_ref[...], sb_ref[...]).astype(BF16)

    @pl.when(jnp.logical_not(jnp.logical_or(is_q, is_k)))
    def _():
        z_ref[...] = project().astype(BF16)


def _inproj(h, g, w_t, b_main, b_gate, rope_c, rope_sa, rope_sb, *, tm):
    rows, d = h.shape
    tn = A_QK
    n_a = GATE_OFFSET // tn
    n = w_t.shape[0] - N_GATES
    assert COL_AQ == GATE_OFFSET and GATE_OFFSET % tn == 0 and n % tn == 0
    tm = min(tm, rows)

    def w_row(i, j):
        return (pl.multiple_of(jnp.where(j < n_a, j * tn, j * tn + N_GATES), N_GATES), 0)

    return pl.pallas_call(
        functools.partial(_inproj_kernel, tn=tn),
        out_shape=(jax.ShapeDtypeStruct((rows, n), BF16),
                   jax.ShapeDtypeStruct((rows, LANES), F32)),
        grid=(rows // tm, n // tn),
        in_specs=[
            pl.BlockSpec((tm, d), lambda i, j: (i, 0)),
            pl.BlockSpec((1, d), lambda i, j: (0, 0)),
            pl.BlockSpec((pl.Element(tn), pl.Element(d)), w_row),
            pl.BlockSpec((1, tn), lambda i, j: (0, j)),
            pl.BlockSpec((N_GATES, d), lambda i, j: (GATE_OFFSET // N_GATES, 0)),
            pl.BlockSpec((1, LANES), lambda i, j: (0, 0)),
            pl.BlockSpec((tm, LANES), lambda i, j: (i, 0)),
            pl.BlockSpec((tm, LANES), lambda i, j: (i, 0)),
            pl.BlockSpec((tm, LANES), lambda i, j: (i, 0)),
        ],
        out_specs=(pl.BlockSpec((tm, tn), lambda i, j: (i, j)),
                   pl.BlockSpec((tm, LANES), lambda i, j: (i, 0))),
        scratch_shapes=[pltpu.VMEM((tm, d), BF16)],
        compiler_params=_params(("parallel", "arbitrary")),
        name="inproj",
    )(h, g, w_t, b_main, w_t, b_gate, rope_c, rope_sa, rope_sb)


CONV_PAD = 8


def _conv_silu(xbuf, n, cw_ref, cb_ref):
    y = cb_ref[...]
    for i in range(CONV_W):
        start = CONV_PAD + i - (CONV_W - 1)
        y = y + cw_ref[i:i + 1, :] * xbuf[start:start + n, :]
    return y * jax.nn.sigmoid(y)


def _log_sigmoid(x):
    return jnp.minimum(x, 0.0) - jnp.log1p(jnp.exp(-jnp.abs(x)))


def _cumsum_rows(x):
    n = x.shape[0]
    r = lax.broadcasted_iota(jnp.int32, (n, n), 0)
    c = lax.broadcasted_iota(jnp.int32, (n, n), 1)
    tril = (r >= c).astype(F32)
    return jnp.dot(tril, x, preferred_element_type=F32, precision=lax.Precision.HIGHEST)


def _state_update(c_scr, m_scr, hd, kf, vext, li_col, b_col, g_tot, m_prev):
    log_w = (g_tot - b_col) + li_col
    m_new = jnp.maximum(g_tot + m_prev, jnp.max(log_w, axis=0, keepdims=True))
    wk = jnp.exp(log_w - m_new)
    decay = jnp.exp(g_tot + m_prev - m_new)
    kw = (kf * wk).astype(BF16)
    upd = lax.dot_general(kw, vext, (((0,), (0,)), ((), ())), preferred_element_type=F32)
    c_scr[hd] = decay * c_scr[hd] + upd
    m_scr[hd] = jnp.broadcast_to(m_new, m_scr.shape[1:])


def _mlstm_kernel(qk_ref, mv_ref, mo_ref, gt_ref, qkm_ref, mvm_ref, gtm_ref, cw_ref, cb_ref,
                  ng_ref, o_ref, c_scr, m_scr, xbuf):
    i = pl.program_id(0)
    t = qk_ref.shape[0]
    kscale = M_DQK ** -0.5

    @pl.when(i == 0)
    def _():
        nm = qkm_ref.shape[0]
        c_scr[...] = jnp.zeros_like(c_scr)
        m_scr[...] = jnp.zeros_like(m_scr)
        xbuf[0:CONV_PAD, :] = jnp.zeros((CONV_PAD, xbuf.shape[1]), F32)
        xbuf[CONV_PAD:CONV_PAD + nm, :] = qkm_ref[...].astype(F32)
        qk = _conv_silu(xbuf, nm, cw_ref, cb_ref)
        xbuf[0:CONV_PAD, :] = xbuf[nm:nm + CONV_PAD, :]
        gts = gtm_ref[...]
        b_all = _cumsum_rows(_log_sigmoid(gts))
        ones = jnp.ones((nm, LANES), BF16)
        for hd in range(M_HEADS):
            kf = qk[:, M_QK + hd * M_DQK:M_QK + (hd + 1) * M_DQK] * kscale
            vext = jnp.concatenate([mvm_ref[:, hd * M_DV:(hd + 1) * M_DV], ones], axis=1)
            b_col = b_all[:, M_HEADS + hd:M_HEADS + hd + 1]
            _state_update(c_scr, m_scr, hd, kf, vext, gts[:, hd:hd + 1], b_col,
                          b_col[nm - 1:nm, :], m_scr[hd][0:1, 0:1])

    xbuf[CONV_PAD:CONV_PAD + t, :] = qk_ref[...].astype(F32)
    qk = _conv_silu(xbuf, t, cw_ref, cb_ref)
    xbuf[0:CONV_PAD, :] = xbuf[t:t + CONV_PAD, :]

    gts = gt_ref[...]
    b_all = _cumsum_rows(_log_sigmoid(gts))
    lane = lax.broadcasted_iota(jnp.int32, gts.shape, 1)
    rowform = jnp.where(lane < M_HEADS, gts, b_all).T
    r = lax.broadcasted_iota(jnp.int32, (t, t), 0)
    c = lax.broadcasted_iota(jnp.int32, (t, t), 1)
    causal = r >= c
    ones = jnp.ones((t, LANES), BF16)

    for hd in range(M_HEADS):
        qh = qk[:, hd * M_DQK:(hd + 1) * M_DQK].astype(BF16)
        kf = qk[:, M_QK + hd * M_DQK:M_QK + (hd + 1) * M_DQK] * kscale
        kh = kf.astype(BF16)
        vext = jnp.concatenate([mv_ref[:, hd * M_DV:(hd + 1) * M_DV], ones], axis=1)
        li_col = gts[:, hd:hd + 1]
        b_col = b_all[:, M_HEADS + hd:M_HEADS + hd + 1]
        li_row = rowform[hd:hd + 1, :]
        b_row = rowform[M_HEADS + hd:M_HEADS + hd + 1, :]
        g_tot = b_col[t - 1:t, :]
        m_prev = m_scr[hd][0:1, 0:1]

        log_d = jnp.where(causal, (b_col - b_row) + li_row, -jnp.inf)
        m_inter = b_col + m_prev
        m_row = jnp.maximum(m_inter, jnp.max(log_d, axis=1, keepdims=True))
        s = lax.dot_general(qh, kh, (((1,), (1,)), ((), ())), preferred_element_type=F32)
        sd = (s * jnp.exp(log_d - m_row)).astype(BF16)
        inter = jnp.dot(qh, c_scr[hd].astype(BF16), preferred_element_type=F32)
        numden = jnp.exp(m_inter - m_row) * inter + jnp.dot(sd, vext, preferred_element_type=F32)
        den = jnp.maximum(jnp.abs(numden[:, M_DV:M_DV + 1]), jnp.exp(-m_row))
        hh = numden[:, :M_DV] / den
        hn = _rms(hh) * ng_ref[:, hd * M_DV:(hd + 1) * M_DV]
        og = jax.nn.sigmoid(mo_ref[:, hd * M_DV:(hd + 1) * M_DV].astype(F32))
        o_ref[:, hd * M_DV:(hd + 1) * M_DV] = (og * hn).astype(BF16)

        _state_update(c_scr, m_scr, hd, kf, vext, li_col, b_col, g_tot, m_prev)


def _mlstm(zx, gx, zm, gm, conv_w, conv_b, norm_g, *, t):
    rows = zx.shape[0]
    nm = zm.shape[0]
    w_qk = 2 * M_QK
    return pl.pallas_call(
        _mlstm_kernel,
        out_shape=jax.ShapeDtypeStruct((rows, M_V), BF16),
        grid=(rows // t,),
        in_specs=[
            pl.BlockSpec((t, w_qk), lambda i: (i, COL_QK // w_qk)),
            pl.BlockSpec((t, M_V), lambda i: (i, COL_MV // M_V)),
            pl.BlockSpec((t, M_V), lambda i: (i, COL_MO // M_V)),
            pl.BlockSpec((t, LANES), lambda i: (i, 0)),
            pl.BlockSpec((nm, w_qk), lambda i: (0, COL_QK // w_qk)),
            pl.BlockSpec((nm, M_V), lambda i: (0, COL_MV // M_V)),
            pl.BlockSpec((nm, LANES), lambda i: (0, 0)),
            pl.BlockSpec((CONV_W, w_qk), lambda i: (0, 0)),
            pl.BlockSpec((1, w_qk), lambda i: (0, 0)),
            pl.BlockSpec((1, M_V), lambda i: (0, 0)),
        ],
        out_specs=pl.BlockSpec((t, M_V), lambda i: (i, 0)),
        scratch_shapes=[
            pltpu.VMEM((M_HEADS, M_DQK, M_DV + LANES), F32),
            pltpu.VMEM((M_HEADS, 8, LANES), F32),
            pltpu.VMEM((t + CONV_PAD, w_qk), F32),
        ],
        compiler_params=_params(("arbitrary",)),
        name="mlstm",
    )(zx, zx, zx, gx, zm, zm, gm, conv_w, conv_b, norm_g)


ATT_CHUNK = 64


ATT_STRIP = 64


def _attn_kernel(q_ref, k_ref, v_ref, km_ref, vm_ref, lq1_ref, lk1_ref, lq2_ref, lk2_ref,
                 g_ref, o_ref, q2_scr, s0_scr, s1_scr, p0_scr, p1_scr, m_scr, alpha_scr, acc_scr):
    i = pl.program_id(1)
    tq = q_ref.shape[0]
    tk = tq
    rows = 2 * tq
    ncol = tk // LANES
    nt = (((1,), (1,)), ((), ()))
    q = q_ref[...]
    lane = lax.broadcasted_iota(jnp.int32, q.shape, 1)
    zero = jnp.zeros_like(q)
    q2_scr[0:tq, :] = jnp.where(lane < A_DH, q, zero)
    q2_scr[tq:rows, :] = jnp.where(lane >= A_DH, q, zero)
    ones_k = jnp.ones((tk, LANES), BF16)

    def scores(blk):
        off = pl.multiple_of(blk * tk, tk)
        return lax.dot_general(q2_scr[...], k_ref[pl.ds(off, tk), :], nt, preferred_element_type=F32)

    def softmax_pv(s_ref, p_ref, blk, masked):
        for st in range(rows // ATT_STRIP):
            rs = slice(st * ATT_STRIP, (st + 1) * ATT_STRIP)
            if masked:
                qchunk = ((st * ATT_STRIP) % tq) // ATT_CHUNK
                nvis = (qchunk + 1) * ATT_CHUNK
                cvis = lax.broadcasted_iota(jnp.int32, (ATT_STRIP, LANES), 1)
                cols = []
                for c in range(-(-nvis // LANES)):
                    sc = s_ref[rs, c * LANES:(c + 1) * LANES]
                    if (c + 1) * LANES > nvis:
                        sc = jnp.where(cvis < nvis - c * LANES, sc, -jnp.inf)
                    cols.append(sc)
            else:
                cols = [s_ref[rs, c * LANES:(c + 1) * LANES] for c in range(ncol)]
            part = cols[0]
            for sc in cols[1:]:
                part = jnp.maximum(part, sc)
            m_old = m_scr[rs, :]
            m_new = jnp.maximum(m_old, jnp.max(part, axis=1, keepdims=True))
            m_scr[rs, :] = m_new
            alpha_scr[rs, :] = jnp.exp2(m_old - m_new)
            if not masked:
                cols = [s_ref[rs, c * LANES:(c + 1) * LANES] for c in range(ncol)]
            pcols = [jnp.exp2(sc - m_new) for sc in cols]
            pcols += [jnp.zeros_like(pcols[0])] * (ncol - len(cols))
            p_ref[rs, :] = jnp.concatenate(pcols, axis=1).astype(BF16)
        off = pl.multiple_of(blk * tk, tk)
        vext = jnp.concatenate([v_ref[pl.ds(off, tk), :], ones_k], axis=1)
        pv = jnp.dot(p_ref[...], vext, preferred_element_type=F32)
        alpha = alpha_scr[...]
        for c in range(2):
            sl = slice(c * LANES, (c + 1) * LANES)
            acc_scr[:, sl] = alpha * acc_scr[:, sl] + pv[:, sl]

    s = lax.dot_general(q2_scr[...], km_ref[...], nt, preferred_element_type=F32)
    m0 = jnp.max(s, axis=1, keepdims=True)
    p = jnp.exp2(s - m0).astype(BF16)
    vext = jnp.concatenate([vm_ref[...], jnp.ones((vm_ref.shape[0], LANES), BF16)], axis=1)
    acc_scr[...] = jnp.dot(p, vext, preferred_element_type=F32)
    m_scr[...] = jnp.broadcast_to(m0, m_scr.shape)

    s0_scr[...] = scores(0)

    def pair(pr, carry):
        s1_scr[...] = scores(2 * pr + 1)
        softmax_pv(s0_scr, p0_scr, 2 * pr, False)
        s0_scr[...] = scores(2 * pr + 2)
        softmax_pv(s1_scr, p1_scr, 2 * pr + 1, False)
        return carry

    lax.fori_loop(0, i // 2, pair, 0)

    @pl.when(i % 2 == 0)
    def _():
        softmax_pv(s0_scr, p0_scr, i, True)

    @pl.when(i % 2 == 1)
    def _():
        s1_scr[...] = scores(i)
        softmax_pv(s0_scr, p0_scr, i - 1, False)
        softmax_pv(s1_scr, p1_scr, i, True)

    lam = (jnp.exp(jnp.sum(lq1_ref[...] * lk1_ref[...], axis=1, keepdims=True))
           - jnp.exp(jnp.sum(lq2_ref[...] * lk2_ref[...], axis=1, keepdims=True)) + LAMBDA_INIT)
    o = acc_scr[:, 0:A_DV] / acc_scr[:, A_DV:A_DV + 1]
    o = o[:tq] - lam * o[tq:]
    o_ref[...] = (_rms(o) * g_ref[...] * (1.0 - LAMBDA_INIT)).astype(BF16)


def _attn(zx, zm, lq1, lk1, lq2, lk2, norm_g, *, tq):
    rows = zx.shape[0]
    nm = zm.shape[0]
    small = pl.BlockSpec((1, A_DH), lambda h, i: (0, 0))
    return pl.pallas_call(
        _attn_kernel,
        out_shape=jax.ShapeDtypeStruct((rows, A_V), BF16),
        grid=(A_HEADS, rows // tq),
        in_specs=[
            pl.BlockSpec((tq, A_DV), lambda h, i: (i, COL_AQ // A_DV + h)),
            pl.BlockSpec((rows, A_DV), lambda h, i: (0, COL_AK // A_DV + h)),
            pl.BlockSpec((rows, A_DV), lambda h, i: (0, COL_AV // A_DV + h)),
            pl.BlockSpec((nm, A_DV), lambda h, i: (0, COL_AK // A_DV + h)),
            pl.BlockSpec((nm, A_DV), lambda h, i: (0, COL_AV // A_DV + h)),
            small, small, small, small,
            pl.BlockSpec((1, A_DV), lambda h, i: (0, h)),
        ],
        out_specs=pl.BlockSpec((tq, A_DV), lambda h, i: (i, h)),
        scratch_shapes=[
            pltpu.VMEM((2 * tq, A_DV), BF16),
            pltpu.VMEM((2 * tq, tq), F32),
            pltpu.VMEM((2 * tq, tq), F32),
            pltpu.VMEM((2 * tq, tq), BF16),
            pltpu.VMEM((2 * tq, tq), BF16),
            pltpu.VMEM((2 * tq, LANES), F32),
            pltpu.VMEM((2 * tq, LANES), F32),
            pltpu.VMEM((2 * tq, 2 * LANES), F32),
        ],
        compiler_params=_params(("parallel", "arbitrary")),
        name="diffattn",
    )(zx, zx, zx, zm, zm, lq1, lk1, lq2, lk2, norm_g)


def _merge_kernel(hm_ref, ha_ref, gm_ref, ga_ref, h_ref, mw_ref, aw_ref, wo_ref, g_ref, o_ref):
    br_m = jnp.dot(hm_ref[...], mw_ref[...], preferred_element_type=F32)
    br_a = jnp.dot(ha_ref[...], aw_ref[...], preferred_element_type=F32)
    y = (jax.nn.sigmoid(gm_ref[...].astype(F32)) * br_m
         + jax.nn.sigmoid(ga_ref[...].astype(F32)) * br_a)
    out = jnp.dot(y.astype(BF16), wo_ref[...], preferred_element_type=F32)
    o_ref[...] = h_ref[...] + _rms(out) * g_ref[...]


def _merge(hm, ha, zx, h, mw, aw, wo, g_post, *, tm):
    rows, d = h.shape
    const = lambda i: (0, 0)
    return pl.pallas_call(
        _merge_kernel,
        out_shape=jax.ShapeDtypeStruct((rows, d), F32),
        grid=(rows // tm,),
        in_specs=[
            pl.BlockSpec((tm, M_V), lambda i: (i, 0)),
            pl.BlockSpec((tm, A_V), lambda i: (i, 0)),
            pl.BlockSpec((tm, d), lambda i: (i, COL_GM // d)),
            pl.BlockSpec((tm, d), lambda i: (i, COL_GM // d + 1)),
            pl.BlockSpec((tm, d), lambda i: (i, 0)),
            pl.BlockSpec((M_V, d), const),
            pl.BlockSpec((A_V, d), const),
            pl.BlockSpec((d, d), const),
            pl.BlockSpec((1, d), const),
        ],
        out_specs=pl.BlockSpec((tm, d), lambda i: (i, 0)),
        compiler_params=_params(("parallel",)),
        name="merge",
    )(hm, ha, zx, zx, h, mw, aw, wo, g_post)


def _rope_tables(n_rows):
    half = ROT_DIM // 2
    f32 = np.float32
    inv_freq = np.power(f32(ROPE_THETA), -np.arange(0, ROT_DIM, 2, dtype=f32) / f32(ROT_DIM))
    ang = np.arange(n_rows, dtype=f32)[:, None] * inv_freq[None, :]
    cos, sin = np.cos(ang).astype(f32), np.sin(ang).astype(f32)
    zeros = np.zeros((n_rows, A_DH - ROT_DIM), f32)
    zh = np.zeros((n_rows, half), f32)
    c = np.concatenate([cos, cos, zeros + f32(1.0)], axis=1)
    sa = np.concatenate([-sin, zh, zeros], axis=1)
    sb = np.concatenate([zh, sin, zeros], axis=1)
    return tuple(np.concatenate([tab, tab], axis=1) for tab in (c, sa, sb))


def kernel(x, meta, ffn1_pre_g, ffn1_post_g, ffn1_w_gate, ffn1_w_up, ffn1_w_down, mix_pre_g, mix_post_g, w_in, b_in, m_conv_w, m_conv_b, m_norm_g, m_w_branch, a_lambda_q1, a_lambda_k1, a_lambda_q2, a_lambda_k2, a_norm_g, a_w_branch, w_out, ffn2_pre_g, ffn2_post_g, ffn2_w_gate, ffn2_w_up, ffn2_w_down):
    batch, seq, d = x.shape
    assert batch == 1 and meta.shape == (N_META, d)
    assert (2 * d) % A_QK == 0 and COL_GM % d == 0
    l = 0
    xr = x.reshape(seq, d)
    row = lambda v: v[l].reshape(1, -1)
    bf = lambda w: w[l].astype(BF16)

    w_in_l, b_in_l = w_in[l], b_in[l]
    gate_end = GATE_OFFSET + N_GATES
    w_in_t = w_in_l.T
    b_main = jnp.concatenate([b_in_l[:GATE_OFFSET], b_in_l[gate_end:]]).reshape(1, -1)
    b_gate = jnp.pad(b_in_l[GATE_OFFSET:gate_end], (0, LANES - N_GATES)).reshape(1, -1)
    rope = _rope_tables(N_META + seq)
    rope_m = tuple(tab[:N_META] for tab in rope)
    rope_x = tuple(tab[N_META:] for tab in rope)

    ffn1 = functools.partial(_ffn, g_pre=row(ffn1_pre_g), g_post=row(ffn1_post_g),
                             w_gate=ffn1_w_gate, w_up=ffn1_w_up, w_down=ffn1_w_down,
                             layer=l, tm=FFN_TM, tf=FFN_TF)
    inproj = functools.partial(_inproj, g=row(mix_pre_g), w_t=w_in_t, b_main=b_main,
                               b_gate=b_gate, tm=1024)
    hx = ffn1(xr)
    hmeta = ffn1(meta)
    zx, gx = inproj(hx, rope_c=rope_x[0], rope_sa=rope_x[1], rope_sb=rope_x[2])
    zm, gm = inproj(hmeta, rope_c=rope_m[0], rope_sa=rope_m[1], rope_sb=rope_m[2])

    hm = _mlstm(zx, gx, zm, gm, m_conv_w[l], row(m_conv_b), row(m_norm_g), t=256)
    ha = _attn(zx, zm, row(a_lambda_q1), row(a_lambda_k1), row(a_lambda_q2), row(a_lambda_k2),
               row(a_norm_g), tq=512)
    h2 = _merge(hm, ha, zx, hx, bf(m_w_branch), bf(a_w_branch), bf(w_out), row(mix_post_g), tm=512)
    out = _ffn(h2, row(ffn2_pre_g), row(ffn2_post_g), ffn2_w_gate, ffn2_w_up, ffn2_w_down,
               layer=l, tm=FFN_TM, tf=FFN_TF)
    return out.reshape(batch, seq, d)
```

```python
import functools

import jax
import jax.numpy as jnp
import numpy as np
from jax import lax
from jax.experimental import pallas as pl
from jax.experimental.pallas import tpu as pltpu

F32 = jnp.float32
BF16 = jnp.bfloat16

EPS = 1e-6
N_META = 16
M_HEADS = 4
M_DQK = 128
M_DV = 256
M_QK = M_HEADS * M_DQK
M_V = M_HEADS * M_DV
CONV_W = 4
A_HEADS = 8
A_DH = 64
A_DV = 2 * A_DH
A_QK = A_HEADS * 2 * A_DH
A_V = A_HEADS * A_DV
ROT_DIM = A_DH // 4
ROPE_THETA = 500000.0
LAMBDA_INIT = 0.8 - 0.6 * 1.0
Q_SCALE = A_DH ** -0.5 * 1.4426950408889634

LANES = 128
GATE_OFFSET = 2 * M_QK + 2 * M_V
N_GATES = 2 * M_HEADS
COL_QK = 0
COL_MV = 2 * M_QK
COL_MO = COL_MV + M_V
COL_AQ = COL_MO + M_V
COL_AK = COL_AQ + A_QK
COL_AV = COL_AK + A_QK
COL_GM = COL_AV + A_V

VMEM_LIMIT = 56 * 1024 * 1024
VMEM_PHYSICAL = 64 * 1024 * 1024


def _params(sem, vmem_limit=VMEM_LIMIT):
    assert vmem_limit < VMEM_PHYSICAL
    return pltpu.CompilerParams(dimension_semantics=sem, vmem_limit_bytes=vmem_limit)


def _rms(x):
    return x * lax.rsqrt(jnp.mean(x * x, axis=-1, keepdims=True) + EPS)


FFN_TM = 1024
FFN_TF = 256
FFN_VMEM_LIMIT = 62 * 1024 * 1024


def _ffn_kernel(h_ref, gpre_ref, gpost_ref, wg_ref, wu_ref, wd_ref, *rest, n_extra):
    if n_extra:
        x_ref, o_ref, ox_ref, u_scr, accx_scr = rest
    else:
        o_ref, u_scr = rest
    i = pl.program_id(0)
    f = pl.program_id(1)
    last = pl.num_programs(1) - 1
    tm = h_ref.shape[0]

    @pl.when(f == 0)
    def _():
        u_scr[0:tm, :] = (_rms(h_ref[...]) * gpre_ref[...]).astype(BF16)
        o_ref[...] = jnp.zeros_like(o_ref)

    def swiglu_down(u):
        g = jnp.dot(u, wg_ref[...].astype(BF16), preferred_element_type=F32)
        up = jnp.dot(u, wu_ref[...].astype(BF16), preferred_element_type=F32)
        a = (g * jax.nn.sigmoid(g) * up).astype(BF16)
        return jnp.dot(a, wd_ref[...].astype(BF16), preferred_element_type=F32)

    if n_extra:
        @pl.when(jnp.logical_and(i == 0, f == 0))
        def _():
            u_scr[tm:tm + n_extra, :] = (_rms(x_ref[...]) * gpre_ref[...]).astype(BF16)
            accx_scr[...] = jnp.zeros_like(accx_scr)

        @pl.when(i == 0)
        def _():
            down = swiglu_down(u_scr[...])
            o_ref[...] += down[0:tm]
            accx_scr[...] += down[tm:tm + n_extra]

        @pl.when(i > 0)
        def _():
            o_ref[...] += swiglu_down(u_scr[0:tm, :])

        @pl.when(jnp.logical_and(i == 0, f == last))
        def _():
            ox_ref[...] = x_ref[...] + 0.5 * (_rms(accx_scr[...]) * gpost_ref[...])
    else:
        o_ref[...] += swiglu_down(u_scr[...])

    @pl.when(f == last)
    def _():
        o_ref[...] = h_ref[...] + 0.5 * (_rms(o_ref[...]) * gpost_ref[...])


def _ffn(h, g_pre, g_post, w_gate, w_up, w_down, extra=None, *, layer, tm, tf):
    rows, d = h.shape
    d_ff = w_gate.shape[2]
    n_extra = 0 if extra is None else extra.shape[0]
    vec = pl.BlockSpec((1, d), lambda i, f: (0, 0))
    tile = pl.BlockSpec((tm, d), lambda i, f: (i, 0))
    in_specs = [tile, vec, vec,
                pl.BlockSpec((None, d, tf), lambda i, f: (layer, 0, f)),
                pl.BlockSpec((None, d, tf), lambda i, f: (layer, 0, f)),
                pl.BlockSpec((None, tf, d), lambda i, f: (layer, f, 0))]
    args = [h, g_pre, g_post, w_gate, w_up, w_down]
    out_shape = jax.ShapeDtypeStruct((rows, d), F32)
    out_specs = tile
    scratch = [pltpu.VMEM((tm + n_extra, d), BF16)]
    if n_extra:
        whole = pl.BlockSpec((n_extra, d), lambda i, f: (0, 0))
        in_specs.append(whole)
        args.append(extra)
        out_shape = (out_shape, jax.ShapeDtypeStruct((n_extra, d), F32))
        out_specs = (tile, whole)
        scratch.append(pltpu.VMEM((n_extra, d), F32))
    return pl.pallas_call(
        functools.partial(_ffn_kernel, n_extra=n_extra),
        out_shape=out_shape,
        grid=(rows // tm, d_ff // tf),
        in_specs=in_specs,
        out_specs=out_specs,
        scratch_shapes=scratch,
        compiler_params=_params(("arbitrary", "arbitrary"), vmem_limit=FFN_VMEM_LIMIT),
        name="ffn",
    )(*args)


def _rope(z, c, sa, sb):
    outs = []
    for grp in range(z.shape[1] // LANES):
        x = z[:, grp * LANES:(grp + 1) * LANES]
        outs.append(x * c + pltpu.roll(x, LANES - ROT_DIM // 2, 1) * sa
                    + pltpu.roll(x, ROT_DIM // 2, 1) * sb)
    return jnp.concatenate(outs, axis=1)


_NT = (((1,), (1,)), ((), ()))


def _inproj_kernel(h_ref, g_ref, wt_ref, b_ref, wgate_ref, bgate_ref, c_ref, sa_ref, sb_ref,
                   xh_ref, xc_ref, xsa_ref, xsb_ref, z_ref, gate_ref, zx_ref, gatex_ref, u_scr,
                   *, tn):
    i = pl.program_id(0)
    j = pl.program_id(1)
    tm = h_ref.shape[0]
    nx = xh_ref.shape[0]

    def gates(u):
        wg = wgate_ref[...].astype(BF16)
        wg = jnp.concatenate([wg, jnp.zeros((LANES - N_GATES, wg.shape[1]), BF16)], axis=0)
        return lax.dot_general(u, wg, _NT, preferred_element_type=F32) + bgate_ref[...]

    @pl.when(j == 0)
    def _():
        u = (_rms(h_ref[...]) * g_ref[...]).astype(BF16)
        u_scr[0:tm, :] = u
        gate_ref[...] = gates(u)

    @pl.when(jnp.logical_and(i == 0, j == 0))
    def _():
        u = (_rms(xh_ref[...]) * g_ref[...]).astype(BF16)
        u_scr[tm:tm + nx, :] = u
        gatex_ref[...] = gates(u)

    def finish(z, kind, c, sa, sb):
        if kind == "plain":
            return z.astype(BF16)
        z = _rope(z, c[...], sa[...], sb[...])
        if kind == "q":
            z = z * Q_SCALE
        return z.astype(BF16)

    def tile_out(kind, with_extra):
        u = u_scr[...] if with_extra else u_scr[0:tm, :]
        w = wt_ref[...].astype(BF16)
        z = lax.dot_general(u, w, _NT, preferred_element_type=F32) + b_ref[...]
        z_ref[...] = finish(z[0:tm], kind, c_ref, sa_ref, sb_ref)
        if with_extra:
            zx_ref[...] = finish(z[tm:tm + nx], kind, xc_ref, xsa_ref, xsb_ref)

    is_q = j == COL_AQ // tn
    is_k = j == COL_AK // tn
    plain = jnp.logical_not(jnp.logical_or(is_q, is_k))
    for kind, cond in (("q", is_q), ("k", is_k), ("plain", plain)):
        pl.when(jnp.logical_and(cond, i == 0))(functools.partial(tile_out, kind, True))
        pl.when(jnp.logical_and(cond, i > 0))(functools.partial(tile_out, kind, False))


def _inproj(h, xh, g, w_t, b_main, b_gate, rope, rope_x, *, tm):
    rows, d = h.shape
    nx = xh.shape[0]
    tn = A_QK
    n_a = GATE_OFFSET // tn
    n = w_t.shape[0] - N_GATES
    nj = n // tn
    assert COL_AQ == GATE_OFFSET and GATE_OFFSET % tn == 0 and n % tn == 0

    def w_row(i, j):
        return (pl.multiple_of(jnp.where(j < n_a, j * tn, j * tn + N_GATES), N_GATES), 0)

    table = pl.BlockSpec((tm, LANES), lambda i, j: (i, 0))
    table_x = pl.BlockSpec((nx, LANES), lambda i, j: (0, 0))
    return pl.pallas_call(
        functools.partial(_inproj_kernel, tn=tn),
        out_shape=(jax.ShapeDtypeStruct((rows, n), BF16),
                   jax.ShapeDtypeStruct((rows, LANES), F32),
                   jax.ShapeDtypeStruct((nx, n), BF16),
                   jax.ShapeDtypeStruct((nx, LANES), F32)),
        grid=(rows // tm, nj),
        in_specs=[
            pl.BlockSpec((tm, d), lambda i, j: (i, 0)),
            pl.BlockSpec((1, d), lambda i, j: (0, 0)),
            pl.BlockSpec((pl.Element(tn), pl.Element(d)), w_row),
            pl.BlockSpec((1, tn), lambda i, j: (0, j)),
            pl.BlockSpec((N_GATES, d), lambda i, j: (GATE_OFFSET // N_GATES, 0)),
            pl.BlockSpec((1, LANES), lambda i, j: (0, 0)),
            table, table, table,
            pl.BlockSpec((nx, d), lambda i, j: (0, 0)),
            table_x, table_x, table_x,
        ],
        out_specs=(pl.BlockSpec((tm, tn), lambda i, j: (i, j)),
                   pl.BlockSpec((tm, LANES), lambda i, j: (i, 0)),
                   pl.BlockSpec((nx, tn), lambda i, j: (0, jnp.where(i == 0, j, nj - 1))),
                   pl.BlockSpec((nx, LANES), lambda i, j: (0, 0))),
        scratch_shapes=[pltpu.VMEM((tm + nx, d), BF16)],
        compiler_params=_params(("arbitrary", "arbitrary")),
        name="inproj",
    )(h, g, w_t, b_main, w_t, b_gate, *rope, xh, *rope_x)


CONV_PAD = 8


def _conv_silu(xbuf, n, cw_ref, cb_ref):
    y = cb_ref[...]
    for i in range(CONV_W):
        start = CONV_PAD + i - (CONV_W - 1)
        y = y + cw_ref[i:i + 1, :] * xbuf[start:start + n, :]
    return y * jax.nn.sigmoid(y)


def _log_sigmoid(x):
    return jnp.minimum(x, 0.0) - jnp.log1p(jnp.exp(-jnp.abs(x)))


def _cumsum_rows(x):
    n = x.shape[0]
    r = lax.broadcasted_iota(jnp.int32, (n, n), 0)
    c = lax.broadcasted_iota(jnp.int32, (n, n), 1)
    tril = (r >= c).astype(F32)
    return jnp.dot(tril, x, preferred_element_type=F32, precision=lax.Precision.HIGHEST)


def _state_update(c_scr, m_scr, hd, kf, vext, li_col, b_col, g_tot, m_prev):
    log_w = (g_tot - b_col) + li_col
    m_new = jnp.maximum(g_tot + m_prev, jnp.max(log_w, axis=0, keepdims=True))
    wk = jnp.exp(log_w - m_new)
    decay = jnp.exp(g_tot + m_prev - m_new)
    kw = (kf * wk).astype(BF16)
    upd = lax.dot_general(kw, vext, (((0,), (0,)), ((), ())), preferred_element_type=F32)
    c_scr[hd] = decay * c_scr[hd] + upd
    m_scr[hd] = jnp.broadcast_to(m_new, m_scr.shape[1:])


def _mlstm_kernel(qk_ref, mv_ref, mo_ref, gt_ref, qkm_ref, mvm_ref, gtm_ref, cw_ref, cb_ref,
                  ng_ref, o_ref, c_scr, m_scr, xbuf):
    i = pl.program_id(0)
    t = qk_ref.shape[0]
    kscale = M_DQK ** -0.5

    @pl.when(i == 0)
    def _():
        nm = qkm_ref.shape[0]
        c_scr[...] = jnp.zeros_like(c_scr)
        m_scr[...] = jnp.zeros_like(m_scr)
        xbuf[0:CONV_PAD, :] = jnp.zeros((CONV_PAD, xbuf.shape[1]), F32)
        xbuf[CONV_PAD:CONV_PAD + nm, :] = qkm_ref[...].astype(F32)
        qk = _conv_silu(xbuf, nm, cw_ref, cb_ref)
        xbuf[0:CONV_PAD, :] = xbuf[nm:nm + CONV_PAD, :]
        gts = gtm_ref[...]
        b_all = _cumsum_rows(_log_sigmoid(gts))
        ones = jnp.ones((nm, LANES), BF16)
        for hd in range(M_HEADS):
            kf = qk[:, M_QK + hd * M_DQK:M_QK + (hd + 1) * M_DQK] * kscale
            vext = jnp.concatenate([mvm_ref[:, hd * M_DV:(hd + 1) * M_DV], ones], axis=1)
            b_col = b_all[:, M_HEADS + hd:M_HEADS + hd + 1]
            _state_update(c_scr, m_scr, hd, kf, vext, gts[:, hd:hd + 1], b_col,
                          b_col[nm - 1:nm, :], m_scr[hd][0:1, 0:1])

    xbuf[CONV_PAD:CONV_PAD + t, :] = qk_ref[...].astype(F32)
    qk = _conv_silu(xbuf, t, cw_ref, cb_ref)
    xbuf[0:CONV_PAD, :] = xbuf[t:t + CONV_PAD, :]

    gts = gt_ref[...]
    b_all = _cumsum_rows(_log_sigmoid(gts))
    lane = lax.broadcasted_iota(jnp.int32, gts.shape, 1)
    rowform = jnp.where(lane < M_HEADS, gts, b_all).T
    r = lax.broadcasted_iota(jnp.int32, (t, t), 0)
    c = lax.broadcasted_iota(jnp.int32, (t, t), 1)
    causal = r >= c
    ones = jnp.ones((t, LANES), BF16)

    for hd in range(M_HEADS):
        qh = qk[:, hd * M_DQK:(hd + 1) * M_DQK].astype(BF16)
        kf = qk[:, M_QK + hd * M_DQK:M_QK + (hd + 1) * M_DQK] * kscale
        kh = kf.astype(BF16)
        vext = jnp.concatenate([mv_ref[:, hd * M_DV:(hd + 1) * M_DV], ones], axis=1)
        li_col = gts[:, hd:hd + 1]
        b_col = b_all[:, M_HEADS + hd:M_HEADS + hd + 1]
        li_row = rowform[hd:hd + 1, :]
        b_row = rowform[M_HEADS + hd:M_HEADS + hd + 1, :]
        g_tot = b_col[t - 1:t, :]
        m_prev = m_scr[hd][0:1, 0:1]

        log_d = jnp.where(causal, (b_col - b_row) + li_row, -jnp.inf)
        m_inter = b_col + m_prev
        m_row = jnp.maximum(m_inter, jnp.max(log_d, axis=1, keepdims=True))
        s = lax.dot_general(qh, kh, (((1,), (1,)), ((), ())), preferred_element_type=F32)
        sd = (s * jnp.exp(log_d - m_row)).astype(BF16)
        inter = jnp.dot(qh, c_scr[hd].astype(BF16), preferred_element_type=F32)
        numden = jnp.exp(m_inter - m_row) * inter + jnp.dot(sd, vext, preferred_element_type=F32)
        den = jnp.maximum(jnp.abs(numden[:, M_DV:M_DV + 1]), jnp.exp(-m_row))
        hh = numden[:, :M_DV] / den
        hn = _rms(hh) * ng_ref[:, hd * M_DV:(hd + 1) * M_DV]
        og = jax.nn.sigmoid(mo_ref[:, hd * M_DV:(hd + 1) * M_DV].astype(F32))
        o_ref[:, hd * M_DV:(hd + 1) * M_DV] = (og * hn).astype(BF16)

        _state_update(c_scr, m_scr, hd, kf, vext, li_col, b_col, g_tot, m_prev)


def _mlstm(zx, gx, zm, gm, conv_w, conv_b, norm_g, *, t):
    rows = zx.shape[0]
    nm = zm.shape[0]
    w_qk = 2 * M_QK
    return pl.pallas_call(
        _mlstm_kernel,
        out_shape=jax.ShapeDtypeStruct((rows, M_V), BF16),
        grid=(rows // t,),
        in_specs=[
            pl.BlockSpec((t, w_qk), lambda i: (i, COL_QK // w_qk)),
            pl.BlockSpec((t, M_V), lambda i: (i, COL_MV // M_V)),
            pl.BlockSpec((t, M_V), lambda i: (i, COL_MO // M_V)),
            pl.BlockSpec((t, LANES), lambda i: (i, 0)),
            pl.BlockSpec((nm, w_qk), lambda i: (0, COL_QK // w_qk)),
            pl.BlockSpec((nm, M_V), lambda i: (0, COL_MV // M_V)),
            pl.BlockSpec((nm, LANES), lambda i: (0, 0)),
            pl.BlockSpec((CONV_W, w_qk), lambda i: (0, 0)),
            pl.BlockSpec((1, w_qk), lambda i: (0, 0)),
            pl.BlockSpec((1, M_V), lambda i: (0, 0)),
        ],
        out_specs=pl.BlockSpec((t, M_V), lambda i: (i, 0)),
        scratch_shapes=[
            pltpu.VMEM((M_HEADS, M_DQK, M_DV + LANES), F32),
            pltpu.VMEM((M_HEADS, 8, LANES), F32),
            pltpu.VMEM((t + CONV_PAD, w_qk), F32),
        ],
        compiler_params=_params(("arbitrary",)),
        name="mlstm",
    )(zx, zx, zx, gx, zm, zm, gm, conv_w, conv_b, norm_g)


ATT_CHUNK = 64


ATT_STRIP = 64


def _attn_kernel(q_ref, k_ref, v_ref, km_ref, vm_ref, lq1_ref, lk1_ref, lq2_ref, lk2_ref,
                 g_ref, o_ref, q2_scr, s0_scr, s1_scr, p0_scr, p1_scr, m_scr, alpha_scr, acc_scr):
    i = pl.program_id(1)
    tq = q_ref.shape[0]
    tk = tq
    rows = 2 * tq
    ncol = tk // LANES
    nt = (((1,), (1,)), ((), ()))
    q = q_ref[...]
    lane = lax.broadcasted_iota(jnp.int32, q.shape, 1)
    zero = jnp.zeros_like(q)
    q2_scr[0:tq, :] = jnp.where(lane < A_DH, q, zero)
    q2_scr[tq:rows, :] = jnp.where(lane >= A_DH, q, zero)
    ones_k = jnp.ones((tk, LANES), BF16)

    def scores(blk):
        off = pl.multiple_of(blk * tk, tk)
        return lax.dot_general(q2_scr[...], k_ref[pl.ds(off, tk), :], nt, preferred_element_type=F32)

    def softmax_pv(s_ref, p_ref, blk, masked):
        for st in range(rows // ATT_STRIP):
            rs = slice(st * ATT_STRIP, (st + 1) * ATT_STRIP)
            if masked:
                qchunk = ((st * ATT_STRIP) % tq) // ATT_CHUNK
                nvis = (qchunk + 1) * ATT_CHUNK
                cvis = lax.broadcasted_iota(jnp.int32, (ATT_STRIP, LANES), 1)
                cols = []
                for c in range(-(-nvis // LANES)):
                    sc = s_ref[rs, c * LANES:(c + 1) * LANES]
                    if (c + 1) * LANES > nvis:
                        sc = jnp.where(cvis < nvis - c * LANES, sc, -jnp.inf)
                    cols.append(sc)
            else:
                cols = [s_ref[rs, c * LANES:(c + 1) * LANES] for c in range(ncol)]
            part = cols[0]
            for sc in cols[1:]:
                part = jnp.maximum(part, sc)
            m_old = m_scr[rs, :]
            m_new = jnp.maximum(m_old, jnp.max(part, axis=1, keepdims=True))
            m_scr[rs, :] = m_new
            alpha_scr[rs, :] = jnp.exp2(m_old - m_new)
            if not masked:
                cols = [s_ref[rs, c * LANES:(c + 1) * LANES] for c in range(ncol)]
            pcols = [jnp.exp2(sc - m_new) for sc in cols]
            pcols += [jnp.zeros_like(pcols[0])] * (ncol - len(cols))
            p_ref[rs, :] = jnp.concatenate(pcols, axis=1).astype(BF16)
        off = pl.multiple_of(blk * tk, tk)
        vext = jnp.concatenate([v_ref[pl.ds(off, tk), :], ones_k], axis=1)
        pv = jnp.dot(p_ref[...], vext, preferred_element_type=F32)
        alpha = alpha_scr[...]
        for c in range(2):
            sl = slice(c * LANES, (c + 1) * LANES)
            acc_scr[:, sl] = alpha * acc_scr[:, sl] + pv[:, sl]

    s = lax.dot_general(q2_scr[...], km_ref[...], nt, preferred_element_type=F32)
    m0 = jnp.max(s, axis=1, keepdims=True)
    p = jnp.exp2(s - m0).astype(BF16)
    vext = jnp.concatenate([vm_ref[...], jnp.ones((vm_ref.shape[0], LANES), BF16)], axis=1)
    acc_scr[...] = jnp.dot(p, vext, preferred_element_type=F32)
    m_scr[...] = jnp.broadcast_to(m0, m_scr.shape)

    s0_scr[...] = scores(0)

    def pair(pr, carry):
        s1_scr[...] = scores(2 * pr + 1)
        softmax_pv(s0_scr, p0_scr, 2 * pr, False)
        s0_scr[...] = scores(2 * pr + 2)
        softmax_pv(s1_scr, p1_scr, 2 * pr + 1, False)
        return carry

    lax.fori_loop(0, i // 2, pair, 0)

    @pl.when(i % 2 == 0)
    def _():
        softmax_pv(s0_scr, p0_scr, i, True)

    @pl.when(i % 2 == 1)
    def _():
        s1_scr[...] = scores(i)
        softmax_pv(s0_scr, p0_scr, i - 1, False)
        softmax_pv(s1_scr, p1_scr, i, True)

    lam = (jnp.exp(jnp.sum(lq1_ref[...] * lk1_ref[...], axis=1, keepdims=True))
           - jnp.exp(jnp.sum(lq2_ref[...] * lk2_ref[...], axis=1, keepdims=True)) + LAMBDA_INIT)
    o = acc_scr[:, 0:A_DV] / acc_scr[:, A_DV:A_DV + 1]
    o = o[:tq] - lam * o[tq:]
    o_ref[...] = (_rms(o) * g_ref[...] * (1.0 - LAMBDA_INIT)).astype(BF16)


def _attn(zx, zm, lq1, lk1, lq2, lk2, norm_g, *, tq):
    rows = zx.shape[0]
    nm = zm.shape[0]
    small = pl.BlockSpec((1, A_DH), lambda h, i: (0, 0))
    return pl.pallas_call(
        _attn_kernel,
        out_shape=jax.ShapeDtypeStruct((rows, A_V), BF16),
        grid=(A_HEADS, rows // tq),
        in_specs=[
            pl.BlockSpec((tq, A_DV), lambda h, i: (i, COL_AQ // A_DV + h)),
            pl.BlockSpec((rows, A_DV), lambda h, i: (0, COL_AK // A_DV + h)),
            pl.BlockSpec((rows, A_DV), lambda h, i: (0, COL_AV // A_DV + h)),
            pl.BlockSpec((nm, A_DV), lambda h, i: (0, COL_AK // A_DV + h)),
            pl.BlockSpec((nm, A_DV), lambda h, i: (0, COL_AV // A_DV + h)),
            small, small, small, small,
            pl.BlockSpec((1, A_DV), lambda h, i: (0, h)),
        ],
        out_specs=pl.BlockSpec((tq, A_DV), lambda h, i: (i, h)),
        scratch_shapes=[
            pltpu.VMEM((2 * tq, A_DV), BF16),
            pltpu.VMEM((2 * tq, tq), F32),
            pltpu.VMEM((2 * tq, tq), F32),
            pltpu.VMEM((2 * tq, tq), BF16),
            pltpu.VMEM((2 * tq, tq), BF16),
            pltpu.VMEM((2 * tq, LANES), F32),
            pltpu.VMEM((2 * tq, LANES), F32),
            pltpu.VMEM((2 * tq, 2 * LANES), F32),
        ],
        compiler_params=_params(("parallel", "arbitrary")),
        name="diffattn",
    )(zx, zx, zx, zm, zm, lq1, lk1, lq2, lk2, norm_g)


def _merge_kernel(hm_ref, ha_ref, gm_ref, ga_ref, h_ref, mw_ref, aw_ref, wo_ref, g_ref, o_ref):
    br_m = jnp.dot(hm_ref[...], mw_ref[...], preferred_element_type=F32)
    br_a = jnp.dot(ha_ref[...], aw_ref[...], preferred_element_type=F32)
    y = (jax.nn.sigmoid(gm_ref[...].astype(F32)) * br_m
         + jax.nn.sigmoid(ga_ref[...].astype(F32)) * br_a)
    out = jnp.dot(y.astype(BF16), wo_ref[...], preferred_element_type=F32)
    o_ref[...] = h_ref[...] + _rms(out) * g_ref[...]


def _merge(hm, ha, zx, h, mw, aw, wo, g_post, *, tm):
    rows, d = h.shape
    const = lambda i: (0, 0)
    return pl.pallas_call(
        _merge_kernel,
        out_shape=jax.ShapeDtypeStruct((rows, d), F32),
        grid=(rows // tm,),
        in_specs=[
            pl.BlockSpec((tm, M_V), lambda i: (i, 0)),
            pl.BlockSpec((tm, A_V), lambda i: (i, 0)),
            pl.BlockSpec((tm, d), lambda i: (i, COL_GM // d)),
            pl.BlockSpec((tm, d), lambda i: (i, COL_GM // d + 1)),
            pl.BlockSpec((tm, d), lambda i: (i, 0)),
            pl.BlockSpec((M_V, d), const),
            pl.BlockSpec((A_V, d), const),
            pl.BlockSpec((d, d), const),
            pl.BlockSpec((1, d), const),
        ],
        out_specs=pl.BlockSpec((tm, d), lambda i: (i, 0)),
        compiler_params=_params(("parallel",)),
        name="merge",
    )(hm, ha, zx, zx, h, mw, aw, wo, g_post)


def _rope_tables(n_rows):
    half = ROT_DIM // 2
    f32 = np.float32
    inv_freq = np.power(f32(ROPE_THETA), -np.arange(0, ROT_DIM, 2, dtype=f32) / f32(ROT_DIM))
    ang = np.arange(n_rows, dtype=f32)[:, None] * inv_freq[None, :]
    cos, sin = np.cos(ang).astype(f32), np.sin(ang).astype(f32)
    zeros = np.zeros((n_rows, A_DH - ROT_DIM), f32)
    zh = np.zeros((n_rows, half), f32)
    c = np.concatenate([cos, cos, zeros + f32(1.0)], axis=1)
    sa = np.concatenate([-sin, zh, zeros], axis=1)
    sb = np.concatenate([zh, sin, zeros], axis=1)
    return tuple(np.concatenate([tab, tab], axis=1) for tab in (c, sa, sb))


def kernel(x, meta, ffn1_pre_g, ffn1_post_g, ffn1_w_gate, ffn1_w_up, ffn1_w_down, mix_pre_g, mix_post_g, w_in, b_in, m_conv_w, m_conv_b, m_norm_g, m_w_branch, a_lambda_q1, a_lambda_k1, a_lambda_q2, a_lambda_k2, a_norm_g, a_w_branch, w_out, ffn2_pre_g, ffn2_post_g, ffn2_w_gate, ffn2_w_up, ffn2_w_down):
    batch, seq, d = x.shape
    assert batch == 1 and meta.shape == (N_META, d)
    assert (2 * d) % A_QK == 0 and COL_GM % d == 0
    l = 0
    xr = x.reshape(seq, d)
    row = lambda v: v[l].reshape(1, -1)
    bf = lambda w: w[l].astype(BF16)

    w_in_l, b_in_l = w_in[l], b_in[l]
    gate_end = GATE_OFFSET + N_GATES
    w_in_t = w_in_l.T
    b_main = jnp.concatenate([b_in_l[:GATE_OFFSET], b_in_l[gate_end:]]).reshape(1, -1)
    b_gate = jnp.pad(b_in_l[GATE_OFFSET:gate_end], (0, LANES - N_GATES)).reshape(1, -1)
    rope = _rope_tables(N_META + seq)
    rope_m = tuple(tab[:N_META] for tab in rope)
    rope_x = tuple(tab[N_META:] for tab in rope)

    hx, hmeta = _ffn(xr, row(ffn1_pre_g), row(ffn1_post_g), ffn1_w_gate, ffn1_w_up, ffn1_w_down,
                     extra=meta, layer=l, tm=FFN_TM, tf=FFN_TF)
    zx, gx, zm, gm = _inproj(hx, hmeta, row(mix_pre_g), w_in_t, b_main, b_gate, rope_x, rope_m,
                             tm=1024)

    hm = _mlstm(zx, gx, zm, gm, m_conv_w[l], row(m_conv_b), row(m_norm_g), t=256)
    ha = _attn(zx, zm, row(a_lambda_q1), row(a_lambda_k1), row(a_lambda_q2), row(a_lambda_k2),
               row(a_norm_g), tq=512)
    h2 = _merge(hm, ha, zx, hx, bf(m_w_branch), bf(a_w_branch), bf(w_out), row(mix_post_g), tm=512)
    out = _ffn(h2, row(ffn2_pre_g), row(ffn2_post_g), ffn2_w_gate, ffn2_w_up, ffn2_w_down,
               layer=l, tm=FFN_TM, tf=FFN_TF)
    return out.reshape(batch, seq, d)
```

```python
import functools

import jax
import jax.numpy as jnp
import numpy as np
from jax import lax
from jax.experimental import pallas as pl
from jax.experimental.pallas import tpu as pltpu

F32 = jnp.float32
BF16 = jnp.bfloat16

EPS = 1e-6
N_META = 16
M_HEADS = 4
M_DQK = 128
M_DV = 256
M_QK = M_HEADS * M_DQK
M_V = M_HEADS * M_DV
CONV_W = 4
A_HEADS = 8
A_DH = 64
A_DV = 2 * A_DH
A_QK = A_HEADS * 2 * A_DH
A_V = A_HEADS * A_DV
ROT_DIM = A_DH // 4
ROPE_THETA = 500000.0
LAMBDA_INIT = 0.8 - 0.6 * 1.0
Q_SCALE = A_DH ** -0.5 * 1.4426950408889634

LANES = 128
GATE_OFFSET = 2 * M_QK + 2 * M_V
N_GATES = 2 * M_HEADS
COL_QK = 0
COL_MV = 2 * M_QK
COL_MO = COL_MV + M_V
COL_AQ = COL_MO + M_V
COL_AK = COL_AQ + A_QK
COL_AV = COL_AK + A_QK
COL_GM = COL_AV + A_V

VMEM_LIMIT = 56 * 1024 * 1024
VMEM_PHYSICAL = 64 * 1024 * 1024


def _params(sem, vmem_limit=VMEM_LIMIT):
    assert vmem_limit < VMEM_PHYSICAL
    return pltpu.CompilerParams(dimension_semantics=sem, vmem_limit_bytes=vmem_limit)


def _rms(x):
    return x * lax.rsqrt(jnp.mean(x * x, axis=-1, keepdims=True) + EPS)


NORM_STRIP = 16


FFN_TM = 1024
FFN_TF = 256
FFN_VMEM_LIMIT = 62 * 1024 * 1024


def _ffn_kernel(h_ref, gpre_ref, gpost_ref, wg_ref, wu_ref, wd_ref, *rest, n_extra):
    if n_extra:
        x_ref, o_ref, ox_ref, u_scr, accx_scr = rest
    else:
        o_ref, u_scr = rest
    i = pl.program_id(0)
    f = pl.program_id(1)
    last = pl.num_programs(1) - 1
    tm = h_ref.shape[0]

    @pl.when(f == 0)
    def _():
        for r0 in range(0, tm, NORM_STRIP):
            rs = slice(r0, r0 + NORM_STRIP)
            u_scr[rs, :] = (_rms(h_ref[rs, :]) * gpre_ref[...]).astype(BF16)
        o_ref[...] = jnp.zeros_like(o_ref)

    def swiglu_down(u):
        g = jnp.dot(u, wg_ref[...].astype(BF16), preferred_element_type=F32)
        up = jnp.dot(u, wu_ref[...].astype(BF16), preferred_element_type=F32)
        a = (g * jax.nn.sigmoid(g) * up).astype(BF16)
        return jnp.dot(a, wd_ref[...].astype(BF16), preferred_element_type=F32)

    if n_extra:
        @pl.when(jnp.logical_and(i == 0, f == 0))
        def _():
            u_scr[tm:tm + n_extra, :] = (_rms(x_ref[...]) * gpre_ref[...]).astype(BF16)
            accx_scr[...] = jnp.zeros_like(accx_scr)

        @pl.when(i == 0)
        def _():
            down = swiglu_down(u_scr[...])
            o_ref[...] += down[0:tm]
            accx_scr[...] += down[tm:tm + n_extra]

        @pl.when(i > 0)
        def _():
            o_ref[...] += swiglu_down(u_scr[0:tm, :])

        @pl.when(jnp.logical_and(i == 0, f == last))
        def _():
            ox_ref[...] = x_ref[...] + 0.5 * (_rms(accx_scr[...]) * gpost_ref[...])
    else:
        o_ref[...] += swiglu_down(u_scr[...])

    @pl.when(f == last)
    def _():
        for r0 in range(0, tm, NORM_STRIP):
            rs = slice(r0, r0 + NORM_STRIP)
            o_ref[rs, :] = h_ref[rs, :] + 0.5 * (_rms(o_ref[rs, :]) * gpost_ref[...])


def _ffn(h, g_pre, g_post, w_gate, w_up, w_down, extra=None, *, layer, tm, tf):
    rows, d = h.shape
    d_ff = w_gate.shape[2]
    n_extra = 0 if extra is None else extra.shape[0]
    vec = pl.BlockSpec((1, d), lambda i, f: (0, 0))
    tile = pl.BlockSpec((tm, d), lambda i, f: (i, 0))
    in_specs = [tile, vec, vec,
                pl.BlockSpec((None, d, tf), lambda i, f: (layer, 0, f)),
                pl.BlockSpec((None, d, tf), lambda i, f: (layer, 0, f)),
                pl.BlockSpec((None, tf, d), lambda i, f: (layer, f, 0))]
    args = [h, g_pre, g_post, w_gate, w_up, w_down]
    out_shape = jax.ShapeDtypeStruct((rows, d), F32)
    out_specs = tile
    scratch = [pltpu.VMEM((tm + n_extra, d), BF16)]
    if n_extra:
        whole = pl.BlockSpec((n_extra, d), lambda i, f: (0, 0))
        in_specs.append(whole)
        args.append(extra)
        out_shape = (out_shape, jax.ShapeDtypeStruct((n_extra, d), F32))
        out_specs = (tile, whole)
        scratch.append(pltpu.VMEM((n_extra, d), F32))
    return pl.pallas_call(
        functools.partial(_ffn_kernel, n_extra=n_extra),
        out_shape=out_shape,
        grid=(rows // tm, d_ff // tf),
        in_specs=in_specs,
        out_specs=out_specs,
        scratch_shapes=scratch,
        compiler_params=_params(("arbitrary", "arbitrary"), vmem_limit=FFN_VMEM_LIMIT),
        name="ffn",
    )(*args)


def _rope(z, c, sa, sb):
    outs = []
    for grp in range(z.shape[1] // LANES):
        x = z[:, grp * LANES:(grp + 1) * LANES]
        outs.append(x * c + pltpu.roll(x, LANES - ROT_DIM // 2, 1) * sa
                    + pltpu.roll(x, ROT_DIM // 2, 1) * sb)
    return jnp.concatenate(outs, axis=1)


_NT = (((1,), (1,)), ((), ()))


def _inproj_kernel(h_ref, g_ref, wt_ref, b_ref, wgate_ref, bgate_ref, c_ref, sa_ref, sb_ref,
                   xh_ref, xc_ref, xsa_ref, xsb_ref, z_ref, gate_ref, zx_ref, gatex_ref, u_scr,
                   *, tn):
    i = pl.program_id(0)
    j = pl.program_id(1)
    tm = h_ref.shape[0]
    nx = xh_ref.shape[0]

    def gates(u):
        wg = wgate_ref[...].astype(BF16)
        wg = jnp.concatenate([wg, jnp.zeros((LANES - N_GATES, wg.shape[1]), BF16)], axis=0)
        return lax.dot_general(u, wg, _NT, preferred_element_type=F32) + bgate_ref[...]

    @pl.when(j == 0)
    def _():
        u = (_rms(h_ref[...]) * g_ref[...]).astype(BF16)
        u_scr[0:tm, :] = u
        gate_ref[...] = gates(u)

    @pl.when(jnp.logical_and(i == 0, j == 0))
    def _():
        u = (_rms(xh_ref[...]) * g_ref[...]).astype(BF16)
        u_scr[tm:tm + nx, :] = u
        gatex_ref[...] = gates(u)

    def finish(z, kind, c, sa, sb):
        if kind == "plain":
            return z.astype(BF16)
        z = _rope(z, c[...], sa[...], sb[...])
        if kind == "q":
            z = z * Q_SCALE
        return z.astype(BF16)

    def tile_out(kind, with_extra):
        u = u_scr[...] if with_extra else u_scr[0:tm, :]
        w = wt_ref[...].astype(BF16)
        z = lax.dot_general(u, w, _NT, preferred_element_type=F32) + b_ref[...]
        z_ref[...] = finish(z[0:tm], kind, c_ref, sa_ref, sb_ref)
        if with_extra:
            zx_ref[...] = finish(z[tm:tm + nx], kind, xc_ref, xsa_ref, xsb_ref)

    is_q = j == COL_AQ // tn
    is_k = j == COL_AK // tn
    plain = jnp.logical_not(jnp.logical_or(is_q, is_k))
    for kind, cond in (("q", is_q), ("k", is_k), ("plain", plain)):
        pl.when(jnp.logical_and(cond, i == 0))(functools.partial(tile_out, kind, True))
        pl.when(jnp.logical_and(cond, i > 0))(functools.partial(tile_out, kind, False))


def _inproj(h, xh, g, w_t, b_main, b_gate, rope, rope_x, *, tm):
    rows, d = h.shape
    nx = xh.shape[0]
    tn = A_QK
    n_a = GATE_OFFSET // tn
    n = w_t.shape[0] - N_GATES
    nj = n // tn
    assert COL_AQ == GATE_OFFSET and GATE_OFFSET % tn == 0 and n % tn == 0

    def w_row(i, j):
        return (pl.multiple_of(jnp.where(j < n_a, j * tn, j * tn + N_GATES), N_GATES), 0)

    table = pl.BlockSpec((tm, LANES), lambda i, j: (i, 0))
    table_x = pl.BlockSpec((nx, LANES), lambda i, j: (0, 0))
    return pl.pallas_call(
        functools.partial(_inproj_kernel, tn=tn),
        out_shape=(jax.ShapeDtypeStruct((rows, n), BF16),
                   jax.ShapeDtypeStruct((rows, LANES), F32),
                   jax.ShapeDtypeStruct((nx, n), BF16),
                   jax.ShapeDtypeStruct((nx, LANES), F32)),
        grid=(rows // tm, nj),
        in_specs=[
            pl.BlockSpec((tm, d), lambda i, j: (i, 0)),
            pl.BlockSpec((1, d), lambda i, j: (0, 0)),
            pl.BlockSpec((pl.Element(tn), pl.Element(d)), w_row),
            pl.BlockSpec((1, tn), lambda i, j: (0, j)),
            pl.BlockSpec((N_GATES, d), lambda i, j: (GATE_OFFSET // N_GATES, 0)),
            pl.BlockSpec((1, LANES), lambda i, j: (0, 0)),
            table, table, table,
            pl.BlockSpec((nx, d), lambda i, j: (0, 0)),
            table_x, table_x, table_x,
        ],
        out_specs=(pl.BlockSpec((tm, tn), lambda i, j: (i, j)),
                   pl.BlockSpec((tm, LANES), lambda i, j: (i, 0)),
                   pl.BlockSpec((nx, tn), lambda i, j: (0, jnp.where(i == 0, j, nj - 1))),
                   pl.BlockSpec((nx, LANES), lambda i, j: (0, 0))),
        scratch_shapes=[pltpu.VMEM((tm + nx, d), BF16)],
        compiler_params=_params(("arbitrary", "arbitrary")),
        name="inproj",
    )(h, g, w_t, b_main, w_t, b_gate, *rope, xh, *rope_x)


CONV_PAD = 8


def _conv_silu(xbuf, n, cw_ref, cb_ref):
    y = cb_ref[...]
    for i in range(CONV_W):
        start = CONV_PAD + i - (CONV_W - 1)
        y = y + cw_ref[i:i + 1, :] * xbuf[start:start + n, :]
    return y * jax.nn.sigmoid(y)


def _log_sigmoid(x):
    return jnp.minimum(x, 0.0) - jnp.log1p(jnp.exp(-jnp.abs(x)))


def _cumsum_rows(x):
    n = x.shape[0]
    r = lax.broadcasted_iota(jnp.int32, (n, n), 0)
    c = lax.broadcasted_iota(jnp.int32, (n, n), 1)
    tril = (r >= c).astype(F32)
    return jnp.dot(tril, x, preferred_element_type=F32, precision=lax.Precision.HIGHEST)


def _state_update(c_scr, m_scr, hd, kf, vext, li_col, b_col, g_tot, m_prev):
    log_w = (g_tot - b_col) + li_col
    m_new = jnp.maximum(g_tot + m_prev, jnp.max(log_w, axis=0, keepdims=True))
    wk = jnp.exp(log_w - m_new)
    decay = jnp.exp(g_tot + m_prev - m_new)
    kw = (kf * wk).astype(BF16)
    upd = lax.dot_general(kw, vext, (((0,), (0,)), ((), ())), preferred_element_type=F32)
    c_scr[hd] = decay * c_scr[hd] + upd
    m_scr[hd] = jnp.broadcast_to(m_new, m_scr.shape[1:])


def _mlstm_kernel(qk_ref, mv_ref, mo_ref, gt_ref, qkm_ref, mvm_ref, gtm_ref, cw_ref, cb_ref,
                  ng_ref, o_ref, c_scr, m_scr, xbuf):
    i = pl.program_id(0)
    t = qk_ref.shape[0]
    kscale = M_DQK ** -0.5

    @pl.when(i == 0)
    def _():
        nm = qkm_ref.shape[0]
        c_scr[...] = jnp.zeros_like(c_scr)
        m_scr[...] = jnp.zeros_like(m_scr)
        xbuf[0:CONV_PAD, :] = jnp.zeros((CONV_PAD, xbuf.shape[1]), F32)
        xbuf[CONV_PAD:CONV_PAD + nm, :] = qkm_ref[...].astype(F32)
        qk = _conv_silu(xbuf, nm, cw_ref, cb_ref)
        xbuf[0:CONV_PAD, :] = xbuf[nm:nm + CONV_PAD, :]
        gts = gtm_ref[...]
        b_all = _cumsum_rows(_log_sigmoid(gts))
        ones = jnp.ones((nm, LANES), BF16)
        for hd in range(M_HEADS):
            kf = qk[:, M_QK + hd * M_DQK:M_QK + (hd + 1) * M_DQK] * kscale
            vext = jnp.concatenate([mvm_ref[:, hd * M_DV:(hd + 1) * M_DV], ones], axis=1)
            b_col = b_all[:, M_HEADS + hd:M_HEADS + hd + 1]
            _state_update(c_scr, m_scr, hd, kf, vext, gts[:, hd:hd + 1], b_col,
                          b_col[nm - 1:nm, :], m_scr[hd][0:1, 0:1])

    xbuf[CONV_PAD:CONV_PAD + t, :] = qk_ref[...].astype(F32)
    qk = _conv_silu(xbuf, t, cw_ref, cb_ref)
    xbuf[0:CONV_PAD, :] = xbuf[t:t + CONV_PAD, :]

    gts = gt_ref[...]
    b_all = _cumsum_rows(_log_sigmoid(gts))
    lane = lax.broadcasted_iota(jnp.int32, gts.shape, 1)
    rowform = jnp.where(lane < M_HEADS, gts, b_all).T
    r = lax.broadcasted_iota(jnp.int32, (t, t), 0)
    c = lax.broadcasted_iota(jnp.int32, (t, t), 1)
    causal = r >= c
    ones = jnp.ones((t, LANES), BF16)

    for hd in range(M_HEADS):
        qh = qk[:, hd * M_DQK:(hd + 1) * M_DQK].astype(BF16)
        kf = qk[:, M_QK + hd * M_DQK:M_QK + (hd + 1) * M_DQK] * kscale
        kh = kf.astype(BF16)
        vext = jnp.concatenate([mv_ref[:, hd * M_DV:(hd + 1) * M_DV], ones], axis=1)
        li_col = gts[:, hd:hd + 1]
        b_col = b_all[:, M_HEADS + hd:M_HEADS + hd + 1]
        li_row = rowform[hd:hd + 1, :]
        b_row = rowform[M_HEADS + hd:M_HEADS + hd + 1, :]
        g_tot = b_col[t - 1:t, :]
        m_prev = m_scr[hd][0:1, 0:1]

        log_d = jnp.where(causal, (b_col - b_row) + li_row, -jnp.inf)
        m_inter = b_col + m_prev
        m_row = jnp.maximum(m_inter, jnp.max(log_d, axis=1, keepdims=True))
        s = lax.dot_general(qh, kh, (((1,), (1,)), ((), ())), preferred_element_type=F32)
        sd = (s * jnp.exp(log_d - m_row)).astype(BF16)
        inter = jnp.dot(qh, c_scr[hd].astype(BF16), preferred_element_type=F32)
        numden = jnp.exp(m_inter - m_row) * inter + jnp.dot(sd, vext, preferred_element_type=F32)
        den = jnp.maximum(jnp.abs(numden[:, M_DV:M_DV + 1]), jnp.exp(-m_row))
        hh = numden[:, :M_DV] / den
        hn = _rms(hh) * ng_ref[:, hd * M_DV:(hd + 1) * M_DV]
        og = jax.nn.sigmoid(mo_ref[:, hd * M_DV:(hd + 1) * M_DV].astype(F32))
        o_ref[:, hd * M_DV:(hd + 1) * M_DV] = (og * hn).astype(BF16)

        _state_update(c_scr, m_scr, hd, kf, vext, li_col, b_col, g_tot, m_prev)


def _mlstm(zx, gx, zm, gm, conv_w, conv_b, norm_g, *, t):
    rows = zx.shape[0]
    nm = zm.shape[0]
    w_qk = 2 * M_QK
    return pl.pallas_call(
        _mlstm_kernel,
        out_shape=jax.ShapeDtypeStruct((rows, M_V), BF16),
        grid=(rows // t,),
        in_specs=[
            pl.BlockSpec((t, w_qk), lambda i: (i, COL_QK // w_qk)),
            pl.BlockSpec((t, M_V), lambda i: (i, COL_MV // M_V)),
            pl.BlockSpec((t, M_V), lambda i: (i, COL_MO // M_V)),
            pl.BlockSpec((t, LANES), lambda i: (i, 0)),
            pl.BlockSpec((nm, w_qk), lambda i: (0, COL_QK // w_qk)),
            pl.BlockSpec((nm, M_V), lambda i: (0, COL_MV // M_V)),
            pl.BlockSpec((nm, LANES), lambda i: (0, 0)),
            pl.BlockSpec((CONV_W, w_qk), lambda i: (0, 0)),
            pl.BlockSpec((1, w_qk), lambda i: (0, 0)),
            pl.BlockSpec((1, M_V), lambda i: (0, 0)),
        ],
        out_specs=pl.BlockSpec((t, M_V), lambda i: (i, 0)),
        scratch_shapes=[
            pltpu.VMEM((M_HEADS, M_DQK, M_DV + LANES), F32),
            pltpu.VMEM((M_HEADS, 8, LANES), F32),
            pltpu.VMEM((t + CONV_PAD, w_qk), F32),
        ],
        compiler_params=_params(("arbitrary",)),
        name="mlstm",
    )(zx, zx, zx, gx, zm, zm, gm, conv_w, conv_b, norm_g)


ATT_CHUNK = 64


ATT_STRIP = 64


def _attn_kernel(q_ref, k_ref, v_ref, km_ref, vm_ref, lq1_ref, lk1_ref, lq2_ref, lk2_ref,
                 g_ref, o_ref, q2_scr, s0_scr, s1_scr, p0_scr, p1_scr, m_scr, alpha_scr, acc_scr):
    i = pl.program_id(1)
    tq = q_ref.shape[0]
    tk = tq
    rows = 2 * tq
    ncol = tk // LANES
    nt = (((1,), (1,)), ((), ()))
    q = q_ref[...]
    lane = lax.broadcasted_iota(jnp.int32, q.shape, 1)
    zero = jnp.zeros_like(q)
    q2_scr[0:tq, :] = jnp.where(lane < A_DH, q, zero)
    q2_scr[tq:rows, :] = jnp.where(lane >= A_DH, q, zero)
    ones_k = jnp.ones((tk, LANES), BF16)

    def scores(blk):
        off = pl.multiple_of(blk * tk, tk)
        return lax.dot_general(q2_scr[...], k_ref[pl.ds(off, tk), :], nt, preferred_element_type=F32)

    def softmax_pv(s_ref, p_ref, blk, masked):
        for st in range(rows // ATT_STRIP):
            rs = slice(st * ATT_STRIP, (st + 1) * ATT_STRIP)
            if masked:
                qchunk = ((st * ATT_STRIP) % tq) // ATT_CHUNK
                nvis = (qchunk + 1) * ATT_CHUNK
                cvis = lax.broadcasted_iota(jnp.int32, (ATT_STRIP, LANES), 1)
                cols = []
                for c in range(-(-nvis // LANES)):
                    sc = s_ref[rs, c * LANES:(c + 1) * LANES]
                    if (c + 1) * LANES > nvis:
                        sc = jnp.where(cvis < nvis - c * LANES, sc, -jnp.inf)
                    cols.append(sc)
            else:
                cols = [s_ref[rs, c * LANES:(c + 1) * LANES] for c in range(ncol)]
            part = cols[0]
            for sc in cols[1:]:
                part = jnp.maximum(part, sc)
            m_old = m_scr[rs, :]
            m_new = jnp.maximum(m_old, jnp.max(part, axis=1, keepdims=True))
            m_scr[rs, :] = m_new
            alpha_scr[rs, :] = jnp.exp2(m_old - m_new)
            if not masked:
                cols = [s_ref[rs, c * LANES:(c + 1) * LANES] for c in range(ncol)]
            pcols = [jnp.exp2(sc - m_new) for sc in cols]
            pcols += [jnp.zeros_like(pcols[0])] * (ncol - len(cols))
            p_ref[rs, :] = jnp.concatenate(pcols, axis=1).astype(BF16)
        off = pl.multiple_of(blk * tk, tk)
        vext = jnp.concatenate([v_ref[pl.ds(off, tk), :], ones_k], axis=1)
        pv = jnp.dot(p_ref[...], vext, preferred_element_type=F32)
        alpha = alpha_scr[...]
        for c in range(2):
            sl = slice(c * LANES, (c + 1) * LANES)
            acc_scr[:, sl] = alpha * acc_scr[:, sl] + pv[:, sl]

    s = lax.dot_general(q2_scr[...], km_ref[...], nt, preferred_element_type=F32)
    m0 = jnp.max(s, axis=1, keepdims=True)
    p = jnp.exp2(s - m0).astype(BF16)
    vext = jnp.concatenate([vm_ref[...], jnp.ones((vm_ref.shape[0], LANES), BF16)], axis=1)
    acc_scr[...] = jnp.dot(p, vext, preferred_element_type=F32)
    m_scr[...] = jnp.broadcast_to(m0, m_scr.shape)

    s0_scr[...] = scores(0)

    def pair(pr):
        s1_scr[...] = scores(2 * pr + 1)
        softmax_pv(s0_scr, p0_scr, 2 * pr, False)
        s0_scr[...] = scores(2 * pr + 2)
        softmax_pv(s1_scr, p1_scr, 2 * pr + 1, False)

    npairs = i // 2

    def two_pairs(t, carry):
        pair(2 * t)
        pair(2 * t + 1)
        return carry

    lax.fori_loop(0, npairs // 2, two_pairs, 0)

    @pl.when(npairs % 2 == 1)
    def _():
        pair(npairs - 1)

    @pl.when(i % 2 == 0)
    def _():
        softmax_pv(s0_scr, p0_scr, i, True)

    @pl.when(i % 2 == 1)
    def _():
        s1_scr[...] = scores(i)
        softmax_pv(s0_scr, p0_scr, i - 1, False)
        softmax_pv(s1_scr, p1_scr, i, True)

    lam = (jnp.exp(jnp.sum(lq1_ref[...] * lk1_ref[...], axis=1, keepdims=True))
           - jnp.exp(jnp.sum(lq2_ref[...] * lk2_ref[...], axis=1, keepdims=True)) + LAMBDA_INIT)
    o = acc_scr[:, 0:A_DV] / acc_scr[:, A_DV:A_DV + 1]
    o = o[:tq] - lam * o[tq:]
    o_ref[...] = (_rms(o) * g_ref[...] * (1.0 - LAMBDA_INIT)).astype(BF16)


def _attn(zx, zm, lq1, lk1, lq2, lk2, norm_g, *, tq):
    rows = zx.shape[0]
    nm = zm.shape[0]
    small = pl.BlockSpec((1, A_DH), lambda h, i: (0, 0))
    return pl.pallas_call(
        _attn_kernel,
        out_shape=jax.ShapeDtypeStruct((rows, A_V), BF16),
        grid=(A_HEADS, rows // tq),
        in_specs=[
            pl.BlockSpec((tq, A_DV), lambda h, i: (i, COL_AQ // A_DV + h)),
            pl.BlockSpec((rows, A_DV), lambda h, i: (0, COL_AK // A_DV + h)),
            pl.BlockSpec((rows, A_DV), lambda h, i: (0, COL_AV // A_DV + h)),
            pl.BlockSpec((nm, A_DV), lambda h, i: (0, COL_AK // A_DV + h)),
            pl.BlockSpec((nm, A_DV), lambda h, i: (0, COL_AV // A_DV + h)),
            small, small, small, small,
            pl.BlockSpec((1, A_DV), lambda h, i: (0, h)),
        ],
        out_specs=pl.BlockSpec((tq, A_DV), lambda h, i: (i, h)),
        scratch_shapes=[
            pltpu.VMEM((2 * tq, A_DV), BF16),
            pltpu.VMEM((2 * tq, tq), F32),
            pltpu.VMEM((2 * tq, tq), F32),
            pltpu.VMEM((2 * tq, tq), BF16),
            pltpu.VMEM((2 * tq, tq), BF16),
            pltpu.VMEM((2 * tq, LANES), F32),
            pltpu.VMEM((2 * tq, LANES), F32),
            pltpu.VMEM((2 * tq, 2 * LANES), F32),
        ],
        compiler_params=_params(("parallel", "arbitrary")),
        name="diffattn",
    )(zx, zx, zx, zm, zm, lq1, lk1, lq2, lk2, norm_g)


def _merge_kernel(hm_ref, ha_ref, gm_ref, ga_ref, h_ref, mw_ref, aw_ref, wo_ref, g_ref, o_ref):
    br_m = jnp.dot(hm_ref[...], mw_ref[...], preferred_element_type=F32)
    br_a = jnp.dot(ha_ref[...], aw_ref[...], preferred_element_type=F32)
    y = (jax.nn.sigmoid(gm_ref[...].astype(F32)) * br_m
         + jax.nn.sigmoid(ga_ref[...].astype(F32)) * br_a)
    out = jnp.dot(y.astype(BF16), wo_ref[...], preferred_element_type=F32)
    o_ref[...] = h_ref[...] + _rms(out) * g_ref[...]


def _merge(hm, ha, zx, h, mw, aw, wo, g_post, *, tm):
    rows, d = h.shape
    const = lambda i: (0, 0)
    return pl.pallas_call(
        _merge_kernel,
        out_shape=jax.ShapeDtypeStruct((rows, d), F32),
        grid=(rows // tm,),
        in_specs=[
            pl.BlockSpec((tm, M_V), lambda i: (i, 0)),
            pl.BlockSpec((tm, A_V), lambda i: (i, 0)),
            pl.BlockSpec((tm, d), lambda i: (i, COL_GM // d)),
            pl.BlockSpec((tm, d), lambda i: (i, COL_GM // d + 1)),
            pl.BlockSpec((tm, d), lambda i: (i, 0)),
            pl.BlockSpec((M_V, d), const),
            pl.BlockSpec((A_V, d), const),
            pl.BlockSpec((d, d), const),
            pl.BlockSpec((1, d), const),
        ],
        out_specs=pl.BlockSpec((tm, d), lambda i: (i, 0)),
        compiler_params=_params(("parallel",)),
        name="merge",
    )(hm, ha, zx, zx, h, mw, aw, wo, g_post)


def _rope_tables(n_rows):
    half = ROT_DIM // 2
    f32 = np.float32
    inv_freq = np.power(f32(ROPE_THETA), -np.arange(0, ROT_DIM, 2, dtype=f32) / f32(ROT_DIM))
    ang = np.arange(n_rows, dtype=f32)[:, None] * inv_freq[None, :]
    cos, sin = np.cos(ang).astype(f32), np.sin(ang).astype(f32)
    zeros = np.zeros((n_rows, A_DH - ROT_DIM), f32)
    zh = np.zeros((n_rows, half), f32)
    c = np.concatenate([cos, cos, zeros + f32(1.0)], axis=1)
    sa = np.concatenate([-sin, zh, zeros], axis=1)
    sb = np.concatenate([zh, sin, zeros], axis=1)
    return tuple(np.concatenate([tab, tab], axis=1) for tab in (c, sa, sb))


def kernel(x, meta, ffn1_pre_g, ffn1_post_g, ffn1_w_gate, ffn1_w_up, ffn1_w_down, mix_pre_g, mix_post_g, w_in, b_in, m_conv_w, m_conv_b, m_norm_g, m_w_branch, a_lambda_q1, a_lambda_k1, a_lambda_q2, a_lambda_k2, a_norm_g, a_w_branch, w_out, ffn2_pre_g, ffn2_post_g, ffn2_w_gate, ffn2_w_up, ffn2_w_down):
    batch, seq, d = x.shape
    assert batch == 1 and meta.shape == (N_META, d)
    assert (2 * d) % A_QK == 0 and COL_GM % d == 0
    l = 0
    xr = x.reshape(seq, d)
    row = lambda v: v[l].reshape(1, -1)
    bf = lambda w: w[l].astype(BF16)

    w_in_l, b_in_l = w_in[l], b_in[l]
    gate_end = GATE_OFFSET + N_GATES
    w_in_t = w_in_l.T
    b_main = jnp.concatenate([b_in_l[:GATE_OFFSET], b_in_l[gate_end:]]).reshape(1, -1)
    b_gate = jnp.pad(b_in_l[GATE_OFFSET:gate_end], (0, LANES - N_GATES)).reshape(1, -1)
    rope = _rope_tables(N_META + seq)
    rope_m = tuple(tab[:N_META] for tab in rope)
    rope_x = tuple(tab[N_META:] for tab in rope)

    hx, hmeta = _ffn(xr, row(ffn1_pre_g), row(ffn1_post_g), ffn1_w_gate, ffn1_w_up, ffn1_w_down,
                     extra=meta, layer=l, tm=FFN_TM, tf=FFN_TF)
    zx, gx, zm, gm = _inproj(hx, hmeta, row(mix_pre_g), w_in_t, b_main, b_gate, rope_x, rope_m,
                             tm=1024)

    hm = _mlstm(zx, gx, zm, gm, m_conv_w[l], row(m_conv_b), row(m_norm_g), t=256)
    ha = _attn(zx, zm, row(a_lambda_q1), row(a_lambda_k1), row(a_lambda_q2), row(a_lambda_k2),
               row(a_norm_g), tq=512)
    h2 = _merge(hm, ha, zx, hx, bf(m_w_branch), bf(a_w_branch), bf(w_out), row(mix_post_g), tm=512)
    out = _ffn(h2, row(ffn2_pre_g), row(ffn2_post_g), ffn2_w_gate, ffn2_w_up, ffn2_w_down,
               layer=l, tm=FFN_TM, tf=FFN_TF)
    return out.reshape(batch, seq, d)
```

```python
import functools

import jax
import jax.numpy as jnp
import numpy as np
from jax import lax
from jax.experimental import pallas as pl
from jax.experimental.pallas import tpu as pltpu

F32 = jnp.float32
BF16 = jnp.bfloat16

EPS = 1e-6
N_META = 16
M_HEADS = 4
M_DQK = 128
M_DV = 256
M_QK = M_HEADS * M_DQK
M_V = M_HEADS * M_DV
CONV_W = 4
A_HEADS = 8
A_DH = 64
A_DV = 2 * A_DH
A_QK = A_HEADS * 2 * A_DH
A_V = A_HEADS * A_DV
ROT_DIM = A_DH // 4
ROPE_THETA = 500000.0
LAMBDA_INIT = 0.8 - 0.6 * 1.0
Q_SCALE = A_DH ** -0.5 * 1.4426950408889634

LANES = 128
GATE_OFFSET = 2 * M_QK + 2 * M_V
N_GATES = 2 * M_HEADS
COL_QK = 0
COL_MV = 2 * M_QK
COL_MO = COL_MV + M_V
COL_AQ = COL_MO + M_V
COL_AK = COL_AQ + A_QK
COL_AV = COL_AK + A_QK
COL_GM = COL_AV + A_V

VMEM_LIMIT = 56 * 1024 * 1024
VMEM_PHYSICAL = 64 * 1024 * 1024


def _params(sem, vmem_limit=VMEM_LIMIT):
    assert vmem_limit < VMEM_PHYSICAL
    return pltpu.CompilerParams(dimension_semantics=sem, vmem_limit_bytes=vmem_limit)


def _rms(x):
    return x * lax.rsqrt(jnp.mean(x * x, axis=-1, keepdims=True) + EPS)


NORM_STRIP = 16


FFN_TM = 1024
FFN_TF = 256
FFN_VMEM_LIMIT = 62 * 1024 * 1024


def _ffn_kernel(h_ref, gpre_ref, gpost_ref, wg_ref, wu_ref, wd_ref, *rest, n_extra):
    if n_extra:
        x_ref, o_ref, ox_ref, u_scr, accx_scr = rest
    else:
        o_ref, u_scr = rest
    i = pl.program_id(0)
    f = pl.program_id(1)
    last = pl.num_programs(1) - 1
    tm = h_ref.shape[0]

    @pl.when(f == 0)
    def _():
        for r0 in range(0, tm, NORM_STRIP):
            rs = slice(r0, r0 + NORM_STRIP)
            u_scr[rs, :] = (_rms(h_ref[rs, :]) * gpre_ref[...]).astype(BF16)
        o_ref[...] = jnp.zeros_like(o_ref)

    def swiglu_down(u):
        g = jnp.dot(u, wg_ref[...].astype(BF16), preferred_element_type=F32)
        up = jnp.dot(u, wu_ref[...].astype(BF16), preferred_element_type=F32)
        a = (g * jax.nn.sigmoid(g) * up).astype(BF16)
        return jnp.dot(a, wd_ref[...].astype(BF16), preferred_element_type=F32)

    if n_extra:
        @pl.when(jnp.logical_and(i == 0, f == 0))
        def _():
            u_scr[tm:tm + n_extra, :] = (_rms(x_ref[...]) * gpre_ref[...]).astype(BF16)
            accx_scr[...] = jnp.zeros_like(accx_scr)

        @pl.when(i == 0)
        def _():
            down = swiglu_down(u_scr[...])
            o_ref[...] += down[0:tm]
            accx_scr[...] += down[tm:tm + n_extra]

        @pl.when(i > 0)
        def _():
            o_ref[...] += swiglu_down(u_scr[0:tm, :])

        @pl.when(jnp.logical_and(i == 0, f == last))
        def _():
            ox_ref[...] = x_ref[...] + 0.5 * (_rms(accx_scr[...]) * gpost_ref[...])
    else:
        o_ref[...] += swiglu_down(u_scr[...])

    @pl.when(f == last)
    def _():
        for r0 in range(0, tm, NORM_STRIP):
            rs = slice(r0, r0 + NORM_STRIP)
            o_ref[rs, :] = h_ref[rs, :] + 0.5 * (_rms(o_ref[rs, :]) * gpost_ref[...])


def _ffn(h, g_pre, g_post, w_gate, w_up, w_down, extra=None, *, layer, tm, tf):
    rows, d = h.shape
    d_ff = w_gate.shape[2]
    n_extra = 0 if extra is None else extra.shape[0]
    vec = pl.BlockSpec((1, d), lambda i, f: (0, 0))
    tile = pl.BlockSpec((tm, d), lambda i, f: (i, 0))
    in_specs = [tile, vec, vec,
                pl.BlockSpec((None, d, tf), lambda i, f: (layer, 0, f)),
                pl.BlockSpec((None, d, tf), lambda i, f: (layer, 0, f)),
                pl.BlockSpec((None, tf, d), lambda i, f: (layer, f, 0))]
    args = [h, g_pre, g_post, w_gate, w_up, w_down]
    out_shape = jax.ShapeDtypeStruct((rows, d), F32)
    out_specs = tile
    scratch = [pltpu.VMEM((tm + n_extra, d), BF16)]
    if n_extra:
        whole = pl.BlockSpec((n_extra, d), lambda i, f: (0, 0))
        in_specs.append(whole)
        args.append(extra)
        out_shape = (out_shape, jax.ShapeDtypeStruct((n_extra, d), F32))
        out_specs = (tile, whole)
        scratch.append(pltpu.VMEM((n_extra, d), F32))
    return pl.pallas_call(
        functools.partial(_ffn_kernel, n_extra=n_extra),
        out_shape=out_shape,
        grid=(rows // tm, d_ff // tf),
        in_specs=in_specs,
        out_specs=out_specs,
        scratch_shapes=scratch,
        compiler_params=_params(("arbitrary", "arbitrary"), vmem_limit=FFN_VMEM_LIMIT),
        name="ffn",
    )(*args)


def _rope(z, c, sa, sb):
    outs = []
    for grp in range(z.shape[1] // LANES):
        x = z[:, grp * LANES:(grp + 1) * LANES]
        outs.append(x * c + pltpu.roll(x, LANES - ROT_DIM // 2, 1) * sa
                    + pltpu.roll(x, ROT_DIM // 2, 1) * sb)
    return jnp.concatenate(outs, axis=1)


_NT = (((1,), (1,)), ((), ()))


def _inproj_kernel(h_ref, g_ref, wt_ref, b_ref, wgate_ref, bgate_ref, c_ref, sa_ref, sb_ref,
                   xh_ref, xc_ref, xsa_ref, xsb_ref, z_ref, gate_ref, zx_ref, gatex_ref, u_scr,
                   *, tn):
    i = pl.program_id(0)
    j = pl.program_id(1)
    tm = h_ref.shape[0]
    nx = xh_ref.shape[0]

    def gates(u):
        wg = wgate_ref[...].astype(BF16)
        wg = jnp.concatenate([wg, jnp.zeros((LANES - N_GATES, wg.shape[1]), BF16)], axis=0)
        return lax.dot_general(u, wg, _NT, preferred_element_type=F32) + bgate_ref[...]

    @pl.when(j == 0)
    def _():
        u = (_rms(h_ref[...]) * g_ref[...]).astype(BF16)
        u_scr[0:tm, :] = u
        gate_ref[...] = gates(u)

    @pl.when(jnp.logical_and(i == 0, j == 0))
    def _():
        u = (_rms(xh_ref[...]) * g_ref[...]).astype(BF16)
        u_scr[tm:tm + nx, :] = u
        gatex_ref[...] = gates(u)

    def finish(z, kind, c, sa, sb):
        if kind == "plain":
            return z.astype(BF16)
        z = _rope(z, c[...], sa[...], sb[...])
        if kind == "q":
            z = z * Q_SCALE
        return z.astype(BF16)

    def tile_out(kind, with_extra):
        u = u_scr[...] if with_extra else u_scr[0:tm, :]
        w = wt_ref[...].astype(BF16)
        z = lax.dot_general(u, w, _NT, preferred_element_type=F32) + b_ref[...]
        z_ref[...] = finish(z[0:tm], kind, c_ref, sa_ref, sb_ref)
        if with_extra:
            zx_ref[...] = finish(z[tm:tm + nx], kind, xc_ref, xsa_ref, xsb_ref)

    is_q = j == COL_AQ // tn
    is_k = j == COL_AK // tn
    plain = jnp.logical_not(jnp.logical_or(is_q, is_k))
    for kind, cond in (("q", is_q), ("k", is_k), ("plain", plain)):
        pl.when(jnp.logical_and(cond, i == 0))(functools.partial(tile_out, kind, True))
        pl.when(jnp.logical_and(cond, i > 0))(functools.partial(tile_out, kind, False))


def _inproj(h, xh, g, w_t, b_main, b_gate, rope, rope_x, *, tm):
    rows, d = h.shape
    nx = xh.shape[0]
    tn = A_QK
    n_a = GATE_OFFSET // tn
    n = w_t.shape[0] - N_GATES
    nj = n // tn
    assert COL_AQ == GATE_OFFSET and GATE_OFFSET % tn == 0 and n % tn == 0

    def w_row(i, j):
        return (pl.multiple_of(jnp.where(j < n_a, j * tn, j * tn + N_GATES), N_GATES), 0)

    table = pl.BlockSpec((tm, LANES), lambda i, j: (i, 0))
    table_x = pl.BlockSpec((nx, LANES), lambda i, j: (0, 0))
    return pl.pallas_call(
        functools.partial(_inproj_kernel, tn=tn),
        out_shape=(jax.ShapeDtypeStruct((rows, n), BF16),
                   jax.ShapeDtypeStruct((rows, LANES), F32),
                   jax.ShapeDtypeStruct((nx, n), BF16),
                   jax.ShapeDtypeStruct((nx, LANES), F32)),
        grid=(rows // tm, nj),
        in_specs=[
            pl.BlockSpec((tm, d), lambda i, j: (i, 0)),
            pl.BlockSpec((1, d), lambda i, j: (0, 0)),
            pl.BlockSpec((pl.Element(tn), pl.Element(d)), w_row),
            pl.BlockSpec((1, tn), lambda i, j: (0, j)),
            pl.BlockSpec((N_GATES, d), lambda i, j: (GATE_OFFSET // N_GATES, 0)),
            pl.BlockSpec((1, LANES), lambda i, j: (0, 0)),
            table, table, table,
            pl.BlockSpec((nx, d), lambda i, j: (0, 0)),
            table_x, table_x, table_x,
        ],
        out_specs=(pl.BlockSpec((tm, tn), lambda i, j: (i, j)),
                   pl.BlockSpec((tm, LANES), lambda i, j: (i, 0)),
                   pl.BlockSpec((nx, tn), lambda i, j: (0, jnp.where(i == 0, j, nj - 1))),
                   pl.BlockSpec((nx, LANES), lambda i, j: (0, 0))),
        scratch_shapes=[pltpu.VMEM((tm + nx, d), BF16)],
        compiler_params=_params(("arbitrary", "arbitrary")),
        name="inproj",
    )(h, g, w_t, b_main, w_t, b_gate, *rope, xh, *rope_x)


CONV_PAD = 8


def _conv_silu(xbuf, n, cw_ref, cb_ref):
    y = cb_ref[...]
    for i in range(CONV_W):
        start = CONV_PAD + i - (CONV_W - 1)
        y = y + cw_ref[i:i + 1, :] * xbuf[start:start + n, :]
    return y * jax.nn.sigmoid(y)


def _log_sigmoid(x):
    return jnp.minimum(x, 0.0) - jnp.log1p(jnp.exp(-jnp.abs(x)))


def _cumsum_rows(x):
    n = x.shape[0]
    r = lax.broadcasted_iota(jnp.int32, (n, n), 0)
    c = lax.broadcasted_iota(jnp.int32, (n, n), 1)
    tril = (r >= c).astype(F32)
    return jnp.dot(tril, x, preferred_element_type=F32, precision=lax.Precision.HIGHEST)


def _state_update(c_scr, m_scr, hd, kf, vext, li_col, b_col, g_tot, m_prev):
    log_w = (g_tot - b_col) + li_col
    m_new = jnp.maximum(g_tot + m_prev, jnp.max(log_w, axis=0, keepdims=True))
    wk = jnp.exp(log_w - m_new)
    decay = jnp.exp(g_tot + m_prev - m_new)
    kw = (kf * wk).astype(BF16)
    upd = lax.dot_general(kw, vext, (((0,), (0,)), ((), ())), preferred_element_type=F32)
    c_scr[hd] = decay * c_scr[hd] + upd
    m_scr[hd] = jnp.broadcast_to(m_new, m_scr.shape[1:])


M_KSCALE = M_DQK ** -0.5


def _mlstm_meta_init(qkm_ref, mvm_ref, gtm_ref, cw_ref, cb_ref, c_scr, m_scr, xbuf):
    nm = qkm_ref.shape[0]
    c_scr[...] = jnp.zeros_like(c_scr)
    m_scr[...] = jnp.zeros_like(m_scr)
    xbuf[0:CONV_PAD, :] = jnp.zeros((CONV_PAD, xbuf.shape[1]), F32)
    xbuf[CONV_PAD:CONV_PAD + nm, :] = qkm_ref[...].astype(F32)
    qk = _conv_silu(xbuf, nm, cw_ref, cb_ref)
    xbuf[0:CONV_PAD, :] = xbuf[nm:nm + CONV_PAD, :]
    gts = gtm_ref[...]
    b_all = _cumsum_rows(_log_sigmoid(gts))
    ones = jnp.ones((nm, LANES), BF16)
    for hd in range(M_HEADS):
        kf = qk[:, M_QK + hd * M_DQK:M_QK + (hd + 1) * M_DQK] * M_KSCALE
        vext = jnp.concatenate([mvm_ref[:, hd * M_DV:(hd + 1) * M_DV], ones], axis=1)
        b_col = b_all[:, M_HEADS + hd:M_HEADS + hd + 1]
        _state_update(c_scr, m_scr, hd, kf, vext, gts[:, hd:hd + 1], b_col,
                      b_col[nm - 1:nm, :], m_scr[hd][0:1, 0:1])


def _mlstm_chunk(qk_ref, mv_ref, mo_ref, gt_ref, cw_ref, cb_ref, ng_ref, o_ref, c_scr, m_scr, xbuf):
    t = qk_ref.shape[0]
    kscale = M_KSCALE
    xbuf[CONV_PAD:CONV_PAD + t, :] = qk_ref[...].astype(F32)
    qk = _conv_silu(xbuf, t, cw_ref, cb_ref)
    xbuf[0:CONV_PAD, :] = xbuf[t:t + CONV_PAD, :]

    gts = gt_ref[...]
    b_all = _cumsum_rows(_log_sigmoid(gts))
    lane = lax.broadcasted_iota(jnp.int32, gts.shape, 1)
    rowform = jnp.where(lane < M_HEADS, gts, b_all).T
    r = lax.broadcasted_iota(jnp.int32, (t, t), 0)
    c = lax.broadcasted_iota(jnp.int32, (t, t), 1)
    causal = r >= c
    ones = jnp.ones((t, LANES), BF16)

    for hd in range(M_HEADS):
        yield
        qh = qk[:, hd * M_DQK:(hd + 1) * M_DQK].astype(BF16)
        kf = qk[:, M_QK + hd * M_DQK:M_QK + (hd + 1) * M_DQK] * kscale
        kh = kf.astype(BF16)
        vext = jnp.concatenate([mv_ref[:, hd * M_DV:(hd + 1) * M_DV], ones], axis=1)
        li_col = gts[:, hd:hd + 1]
        b_col = b_all[:, M_HEADS + hd:M_HEADS + hd + 1]
        li_row = rowform[hd:hd + 1, :]
        b_row = rowform[M_HEADS + hd:M_HEADS + hd + 1, :]
        g_tot = b_col[t - 1:t, :]
        m_prev = m_scr[hd][0:1, 0:1]

        log_d = jnp.where(causal, (b_col - b_row) + li_row, -jnp.inf)
        m_inter = b_col + m_prev
        m_row = jnp.maximum(m_inter, jnp.max(log_d, axis=1, keepdims=True))
        s = lax.dot_general(qh, kh, (((1,), (1,)), ((), ())), preferred_element_type=F32)
        sd = (s * jnp.exp(log_d - m_row)).astype(BF16)
        inter = jnp.dot(qh, c_scr[hd].astype(BF16), preferred_element_type=F32)
        numden = jnp.exp(m_inter - m_row) * inter + jnp.dot(sd, vext, preferred_element_type=F32)
        den = jnp.maximum(jnp.abs(numden[:, M_DV:M_DV + 1]), jnp.exp(-m_row))
        hh = numden[:, :M_DV] / den
        hn = _rms(hh) * ng_ref[:, hd * M_DV:(hd + 1) * M_DV]
        og = jax.nn.sigmoid(mo_ref[:, hd * M_DV:(hd + 1) * M_DV].astype(F32))
        o_ref[:, hd * M_DV:(hd + 1) * M_DV] = (og * hn).astype(BF16)

        _state_update(c_scr, m_scr, hd, kf, vext, li_col, b_col, g_tot, m_prev)


ATT_CHUNK = 64


ATT_STRIP = 64


def _attn_kernel(q_ref, k_ref, v_ref, km_ref, vm_ref, lq1_ref, lk1_ref, lq2_ref, lk2_ref,
                 g_ref, o_ref, q2_scr, s0_scr, s1_scr, p0_scr, p1_scr, m_scr, alpha_scr, acc_scr):
    i = pl.program_id(1)
    tq = q_ref.shape[0]
    tk = tq
    rows = 2 * tq
    ncol = tk // LANES
    nt = (((1,), (1,)), ((), ()))
    q = q_ref[...]
    lane = lax.broadcasted_iota(jnp.int32, q.shape, 1)
    zero = jnp.zeros_like(q)
    q2_scr[0:tq, :] = jnp.where(lane < A_DH, q, zero)
    q2_scr[tq:rows, :] = jnp.where(lane >= A_DH, q, zero)
    ones_k = jnp.ones((tk, LANES), BF16)

    def scores(blk):
        off = pl.multiple_of(blk * tk, tk)
        return lax.dot_general(q2_scr[...], k_ref[pl.ds(off, tk), :], nt, preferred_element_type=F32)

    def softmax_pv(s_ref, p_ref, blk, masked):
        for st in range(rows // ATT_STRIP):
            rs = slice(st * ATT_STRIP, (st + 1) * ATT_STRIP)
            if masked:
                qchunk = ((st * ATT_STRIP) % tq) // ATT_CHUNK
                nvis = (qchunk + 1) * ATT_CHUNK
                cvis = lax.broadcasted_iota(jnp.int32, (ATT_STRIP, LANES), 1)
                cols = []
                for c in range(-(-nvis // LANES)):
                    sc = s_ref[rs, c * LANES:(c + 1) * LANES]
                    if (c + 1) * LANES > nvis:
                        sc = jnp.where(cvis < nvis - c * LANES, sc, -jnp.inf)
                    cols.append(sc)
            else:
                cols = [s_ref[rs, c * LANES:(c + 1) * LANES] for c in range(ncol)]
            part = cols[0]
            for sc in cols[1:]:
                part = jnp.maximum(part, sc)
            m_old = m_scr[rs, :]
            m_new = jnp.maximum(m_old, jnp.max(part, axis=1, keepdims=True))
            m_scr[rs, :] = m_new
            alpha_scr[rs, :] = jnp.exp2(m_old - m_new)
            if not masked:
                cols = [s_ref[rs, c * LANES:(c + 1) * LANES] for c in range(ncol)]
            pcols = [jnp.exp2(sc - m_new) for sc in cols]
            pcols += [jnp.zeros_like(pcols[0])] * (ncol - len(cols))
            p_ref[rs, :] = jnp.concatenate(pcols, axis=1).astype(BF16)
        off = pl.multiple_of(blk * tk, tk)
        vext = jnp.concatenate([v_ref[pl.ds(off, tk), :], ones_k], axis=1)
        pv = jnp.dot(p_ref[...], vext, preferred_element_type=F32)
        alpha = alpha_scr[...]
        for c in range(2):
            sl = slice(c * LANES, (c + 1) * LANES)
            acc_scr[:, sl] = alpha * acc_scr[:, sl] + pv[:, sl]

    s = lax.dot_general(q2_scr[...], km_ref[...], nt, preferred_element_type=F32)
    m0 = jnp.max(s, axis=1, keepdims=True)
    p = jnp.exp2(s - m0).astype(BF16)
    vext = jnp.concatenate([vm_ref[...], jnp.ones((vm_ref.shape[0], LANES), BF16)], axis=1)
    acc_scr[...] = jnp.dot(p, vext, preferred_element_type=F32)
    m_scr[...] = jnp.broadcast_to(m0, m_scr.shape)

    s0_scr[...] = scores(0)

    def pair(pr):
        s1_scr[...] = scores(2 * pr + 1)
        softmax_pv(s0_scr, p0_scr, 2 * pr, False)
        s0_scr[...] = scores(2 * pr + 2)
        softmax_pv(s1_scr, p1_scr, 2 * pr + 1, False)

    npairs = i // 2

    def two_pairs(t, carry):
        pair(2 * t)
        pair(2 * t + 1)
        return carry

    lax.fori_loop(0, npairs // 2, two_pairs, 0)

    @pl.when(npairs % 2 == 1)
    def _():
        pair(npairs - 1)

    @pl.when(i % 2 == 0)
    def _():
        softmax_pv(s0_scr, p0_scr, i, True)

    @pl.when(i % 2 == 1)
    def _():
        s1_scr[...] = scores(i)
        softmax_pv(s0_scr, p0_scr, i - 1, False)
        softmax_pv(s1_scr, p1_scr, i, True)

    lam = (jnp.exp(jnp.sum(lq1_ref[...] * lk1_ref[...], axis=1, keepdims=True))
           - jnp.exp(jnp.sum(lq2_ref[...] * lk2_ref[...], axis=1, keepdims=True)) + LAMBDA_INIT)
    o = acc_scr[:, 0:A_DV] / acc_scr[:, A_DV:A_DV + 1]
    o = o[:tq] - lam * o[tq:]
    o_ref[...] = (_rms(o) * g_ref[...] * (1.0 - LAMBDA_INIT)).astype(BF16)


def _attn(zx, zm, lq1, lk1, lq2, lk2, norm_g, *, tq):
    rows = zx.shape[0]
    nm = zm.shape[0]
    small = pl.BlockSpec((1, A_DH), lambda h, i: (0, 0))
    return pl.pallas_call(
        _attn_kernel,
        out_shape=jax.ShapeDtypeStruct((rows, A_V), BF16),
        grid=(A_HEADS, rows // tq),
        in_specs=[
            pl.BlockSpec((tq, A_DV), lambda h, i: (i, COL_AQ // A_DV + h)),
            pl.BlockSpec((rows, A_DV), lambda h, i: (0, COL_AK // A_DV + h)),
            pl.BlockSpec((rows, A_DV), lambda h, i: (0, COL_AV // A_DV + h)),
            pl.BlockSpec((nm, A_DV), lambda h, i: (0, COL_AK // A_DV + h)),
            pl.BlockSpec((nm, A_DV), lambda h, i: (0, COL_AV // A_DV + h)),
            small, small, small, small,
            pl.BlockSpec((1, A_DV), lambda h, i: (0, h)),
        ],
        out_specs=pl.BlockSpec((tq, A_DV), lambda h, i: (i, h)),
        scratch_shapes=[
            pltpu.VMEM((2 * tq, A_DV), BF16),
            pltpu.VMEM((2 * tq, tq), F32),
            pltpu.VMEM((2 * tq, tq), F32),
            pltpu.VMEM((2 * tq, tq), BF16),
            pltpu.VMEM((2 * tq, tq), BF16),
            pltpu.VMEM((2 * tq, LANES), F32),
            pltpu.VMEM((2 * tq, LANES), F32),
            pltpu.VMEM((2 * tq, 2 * LANES), F32),
        ],
        compiler_params=_params(("parallel", "arbitrary")),
        name="diffattn",
    )(zx, zx, zx, zm, zm, lq1, lk1, lq2, lk2, norm_g)


def _mixout_kernel(ha_ref, gm_ref, ga_ref, h_ref, mw_ref, aw_ref, wo_ref, gpost_ref,
                   qk_ref, mv_ref, mo_ref, gt_ref, qkm_ref, mvm_ref, gtm_ref, cw_ref, cb_ref, ng_ref,
                   o_ref, hm_scr, hm_prev_scr, c_scr, m_scr, xbuf):
    s = pl.program_id(0)
    last = pl.num_programs(0) - 1

    def merge():
        hm_prev_scr[...] = hm_scr[...]
        br_m = jnp.dot(hm_prev_scr[...], mw_ref[...], preferred_element_type=F32)
        yield
        br_a = jnp.dot(ha_ref[...], aw_ref[...], preferred_element_type=F32)
        yield
        y = (jax.nn.sigmoid(gm_ref[...].astype(F32)) * br_m
             + jax.nn.sigmoid(ga_ref[...].astype(F32)) * br_a)
        yield
        out = jnp.dot(y.astype(BF16), wo_ref[...], preferred_element_type=F32)
        yield
        o_ref[...] = h_ref[...] + _rms(out) * gpost_ref[...]

    def mlstm():
        return _mlstm_chunk(qk_ref, mv_ref, mo_ref, gt_ref, cw_ref, cb_ref, ng_ref, hm_scr,
                            c_scr, m_scr, xbuf)

    def run(*gens):
        gens = list(gens)
        while gens:
            for g in list(gens):
                try:
                    next(g)
                except StopIteration:
                    gens.remove(g)

    @pl.when(s == 0)
    def _():
        _mlstm_meta_init(qkm_ref, mvm_ref, gtm_ref, cw_ref, cb_ref, c_scr, m_scr, xbuf)
        run(mlstm())

    @pl.when(jnp.logical_and(s > 0, s < last))
    def _():
        run(merge(), mlstm())

    @pl.when(s == last)
    def _():
        run(merge())


def _mixout(zx, gx, zm, gm, ha, h, conv_w, conv_b, m_norm_g, mw, aw, wo, g_post, *, t):
    rows, d = h.shape
    nm = zm.shape[0]
    n = rows // t
    w_qk = 2 * M_QK
    const = lambda s: (0, 0)
    prev = lambda col: (lambda s: (jnp.maximum(s - 1, 0), col))
    cur = lambda col: (lambda s: (jnp.minimum(s, n - 1), col))
    return pl.pallas_call(
        _mixout_kernel,
        out_shape=jax.ShapeDtypeStruct((rows, d), F32),
        grid=(n + 1,),
        in_specs=[
            pl.BlockSpec((t, A_V), prev(0)),
            pl.BlockSpec((t, d), prev(COL_GM // d)),
            pl.BlockSpec((t, d), prev(COL_GM // d + 1)),
            pl.BlockSpec((t, d), prev(0)),
            pl.BlockSpec((M_V, d), const),
            pl.BlockSpec((A_V, d), const),
            pl.BlockSpec((d, d), const),
            pl.BlockSpec((1, d), const),
            pl.BlockSpec((t, w_qk), cur(COL_QK // w_qk)),
            pl.BlockSpec((t, M_V), cur(COL_MV // M_V)),
            pl.BlockSpec((t, M_V), cur(COL_MO // M_V)),
            pl.BlockSpec((t, LANES), cur(0)),
            pl.BlockSpec((nm, w_qk), lambda s: (0, COL_QK // w_qk)),
            pl.BlockSpec((nm, M_V), lambda s: (0, COL_MV // M_V)),
            pl.BlockSpec((nm, LANES), const),
            pl.BlockSpec((CONV_W, w_qk), const),
            pl.BlockSpec((1, w_qk), const),
            pl.BlockSpec((1, M_V), const),
        ],
        out_specs=pl.BlockSpec((t, d), prev(0)),
        scratch_shapes=[
            pltpu.VMEM((t, M_V), BF16),
            pltpu.VMEM((t, M_V), BF16),
            pltpu.VMEM((M_HEADS, M_DQK, M_DV + LANES), F32),
            pltpu.VMEM((M_HEADS, 8, LANES), F32),
            pltpu.VMEM((t + CONV_PAD, w_qk), F32),
        ],
        compiler_params=_params(("arbitrary",)),
        name="mixout",
    )(ha, zx, zx, h, mw, aw, wo, g_post, zx, zx, zx, gx, zm, zm, gm, conv_w, conv_b, m_norm_g)


def _rope_tables(n_rows):
    half = ROT_DIM // 2
    f32 = np.float32
    inv_freq = np.power(f32(ROPE_THETA), -np.arange(0, ROT_DIM, 2, dtype=f32) / f32(ROT_DIM))
    ang = np.arange(n_rows, dtype=f32)[:, None] * inv_freq[None, :]
    cos, sin = np.cos(ang).astype(f32), np.sin(ang).astype(f32)
    zeros = np.zeros((n_rows, A_DH - ROT_DIM), f32)
    zh = np.zeros((n_rows, half), f32)
    c = np.concatenate([cos, cos, zeros + f32(1.0)], axis=1)
    sa = np.concatenate([-sin, zh, zeros], axis=1)
    sb = np.concatenate([zh, sin, zeros], axis=1)
    return tuple(np.concatenate([tab, tab], axis=1) for tab in (c, sa, sb))


def kernel(x, meta, ffn1_pre_g, ffn1_post_g, ffn1_w_gate, ffn1_w_up, ffn1_w_down, mix_pre_g, mix_post_g, w_in, b_in, m_conv_w, m_conv_b, m_norm_g, m_w_branch, a_lambda_q1, a_lambda_k1, a_lambda_q2, a_lambda_k2, a_norm_g, a_w_branch, w_out, ffn2_pre_g, ffn2_post_g, ffn2_w_gate, ffn2_w_up, ffn2_w_down):
    batch, seq, d = x.shape
    assert batch == 1 and meta.shape == (N_META, d)
    assert (2 * d) % A_QK == 0 and COL_GM % d == 0
    l = 0
    xr = x.reshape(seq, d)
    row = lambda v: v[l].reshape(1, -1)
    bf = lambda w: w[l].astype(BF16)

    w_in_l, b_in_l = w_in[l], b_in[l]
    gate_end = GATE_OFFSET + N_GATES
    w_in_t = w_in_l.T
    b_main = jnp.concatenate([b_in_l[:GATE_OFFSET], b_in_l[gate_end:]]).reshape(1, -1)
    b_gate = jnp.pad(b_in_l[GATE_OFFSET:gate_end], (0, LANES - N_GATES)).reshape(1, -1)
    rope = _rope_tables(N_META + seq)
    rope_m = tuple(tab[:N_META] for tab in rope)
    rope_x = tuple(tab[N_META:] for tab in rope)

    hx, hmeta = _ffn(xr, row(ffn1_pre_g), row(ffn1_post_g), ffn1_w_gate, ffn1_w_up, ffn1_w_down,
                     extra=meta, layer=l, tm=FFN_TM, tf=FFN_TF)
    zx, gx, zm, gm = _inproj(hx, hmeta, row(mix_pre_g), w_in_t, b_main, b_gate, rope_x, rope_m,
                             tm=1024)

    ha = _attn(zx, zm, row(a_lambda_q1), row(a_lambda_k1), row(a_lambda_q2), row(a_lambda_k2),
               row(a_norm_g), tq=512)
    h2 = _mixout(zx, gx, zm, gm, ha, hx, m_conv_w[l], row(m_conv_b), row(m_norm_g),
                 bf(m_w_branch), bf(a_w_branch), bf(w_out), row(mix_post_g), t=256)
    out = _ffn(h2, row(ffn2_pre_g), row(ffn2_post_g), ffn2_w_gate, ffn2_w_up, ffn2_w_down,
               layer=l, tm=FFN_TM, tf=FFN_TF)
    return out.reshape(batch, seq, d)
```

```python
import functools

import jax
import jax.numpy as jnp
import numpy as np
from jax import lax
from jax.experimental import pallas as pl
from jax.experimental.pallas import tpu as pltpu

F32 = jnp.float32
BF16 = jnp.bfloat16

EPS = 1e-6
N_META = 16
M_HEADS = 4
M_DQK = 128
M_DV = 256
M_QK = M_HEADS * M_DQK
M_V = M_HEADS * M_DV
CONV_W = 4
A_HEADS = 8
A_DH = 64
A_DV = 2 * A_DH
A_QK = A_HEADS * 2 * A_DH
A_V = A_HEADS * A_DV
ROT_DIM = A_DH // 4
ROPE_THETA = 500000.0
LAMBDA_INIT = 0.8 - 0.6 * 1.0
Q_SCALE = A_DH ** -0.5 * 1.4426950408889634

LANES = 128
GATE_OFFSET = 2 * M_QK + 2 * M_V
N_GATES = 2 * M_HEADS
COL_QK = 0
COL_MV = 2 * M_QK
COL_MO = COL_MV + M_V
COL_AQ = COL_MO + M_V
COL_AK = COL_AQ + A_QK
COL_AV = COL_AK + A_QK
COL_GM = COL_AV + A_V

VMEM_LIMIT = 56 * 1024 * 1024
VMEM_PHYSICAL = 64 * 1024 * 1024


def _params(sem, vmem_limit=VMEM_LIMIT):
    assert vmem_limit < VMEM_PHYSICAL
    return pltpu.CompilerParams(dimension_semantics=sem, vmem_limit_bytes=vmem_limit)


def _rms(x):
    return x * lax.rsqrt(jnp.mean(x * x, axis=-1, keepdims=True) + EPS)


NORM_STRIP = 16


FFN_TM = 1024
FFN_TF = 256
FFN_VMEM_LIMIT = 62 * 1024 * 1024


def _ffn_kernel(h_ref, gpre_ref, gpost_ref, wg_ref, wu_ref, wd_ref, *rest, n_extra):
    if n_extra:
        x_ref, o_ref, ox_ref, u_scr, accx_scr = rest
    else:
        o_ref, u_scr = rest
    i = pl.program_id(0)
    f = pl.program_id(1)
    last = pl.num_programs(1) - 1
    tm = h_ref.shape[0]

    @pl.when(f == 0)
    def _():
        for r0 in range(0, tm, NORM_STRIP):
            rs = slice(r0, r0 + NORM_STRIP)
            u_scr[rs, :] = (_rms(h_ref[rs, :]) * gpre_ref[...]).astype(BF16)
        o_ref[...] = jnp.zeros_like(o_ref)

    def swiglu_down(u):
        g = jnp.dot(u, wg_ref[...].astype(BF16), preferred_element_type=F32)
        up = jnp.dot(u, wu_ref[...].astype(BF16), preferred_element_type=F32)
        a = (g * jax.nn.sigmoid(g) * up).astype(BF16)
        return jnp.dot(a, wd_ref[...].astype(BF16), preferred_element_type=F32)

    if n_extra:
        @pl.when(jnp.logical_and(i == 0, f == 0))
        def _():
            u_scr[tm:tm + n_extra, :] = (_rms(x_ref[...]) * gpre_ref[...]).astype(BF16)
            accx_scr[...] = jnp.zeros_like(accx_scr)

        @pl.when(i == 0)
        def _():
            down = swiglu_down(u_scr[...])
            o_ref[...] += down[0:tm]
            accx_scr[...] += down[tm:tm + n_extra]

        @pl.when(i > 0)
        def _():
            o_ref[...] += swiglu_down(u_scr[0:tm, :])

        @pl.when(jnp.logical_and(i == 0, f == last))
        def _():
            ox_ref[...] = x_ref[...] + 0.5 * (_rms(accx_scr[...]) * gpost_ref[...])
    else:
        o_ref[...] += swiglu_down(u_scr[...])

    @pl.when(f == last)
    def _():
        for r0 in range(0, tm, NORM_STRIP):
            rs = slice(r0, r0 + NORM_STRIP)
            o_ref[rs, :] = h_ref[rs, :] + 0.5 * (_rms(o_ref[rs, :]) * gpost_ref[...])


def _ffn(h, g_pre, g_post, w_gate, w_up, w_down, extra=None, *, layer, tm, tf):
    rows, d = h.shape
    d_ff = w_gate.shape[2]
    n_extra = 0 if extra is None else extra.shape[0]
    vec = pl.BlockSpec((1, d), lambda i, f: (0, 0))
    tile = pl.BlockSpec((tm, d), lambda i, f: (i, 0))
    in_specs = [tile, vec, vec,
                pl.BlockSpec((None, d, tf), lambda i, f: (layer, 0, f)),
                pl.BlockSpec((None, d, tf), lambda i, f: (layer, 0, f)),
                pl.BlockSpec((None, tf, d), lambda i, f: (layer, f, 0))]
    args = [h, g_pre, g_post, w_gate, w_up, w_down]
    out_shape = jax.ShapeDtypeStruct((rows, d), F32)
    out_specs = tile
    scratch = [pltpu.VMEM((tm + n_extra, d), BF16)]
    if n_extra:
        whole = pl.BlockSpec((n_extra, d), lambda i, f: (0, 0))
        in_specs.append(whole)
        args.append(extra)
        out_shape = (out_shape, jax.ShapeDtypeStruct((n_extra, d), F32))
        out_specs = (tile, whole)
        scratch.append(pltpu.VMEM((n_extra, d), F32))
    return pl.pallas_call(
        functools.partial(_ffn_kernel, n_extra=n_extra),
        out_shape=out_shape,
        grid=(rows // tm, d_ff // tf),
        in_specs=in_specs,
        out_specs=out_specs,
        scratch_shapes=scratch,
        compiler_params=_params(("arbitrary", "arbitrary"), vmem_limit=FFN_VMEM_LIMIT),
        name="ffn",
    )(*args)


def _rope(z, c, sa, sb):
    outs = []
    for grp in range(z.shape[1] // LANES):
        x = z[:, grp * LANES:(grp + 1) * LANES]
        outs.append(x * c + pltpu.roll(x, LANES - ROT_DIM // 2, 1) * sa
                    + pltpu.roll(x, ROT_DIM // 2, 1) * sb)
    return jnp.concatenate(outs, axis=1)


_NT = (((1,), (1,)), ((), ()))
ROPE_GROUP = 256


def _inproj_kernel(h_ref, g_ref, wt_ref, b_ref, wgate_ref, bgate_ref, c_ref, sa_ref, sb_ref,
                   xh_ref, xc_ref, xsa_ref, xsb_ref, z_ref, gate_ref, zx_ref, gatex_ref, u_scr,
                   *, tn):
    i = pl.program_id(0)
    j = pl.program_id(1)
    tm = h_ref.shape[0]
    nx = xh_ref.shape[0]

    def gates(u):
        wg = wgate_ref[...].astype(BF16)
        wg = jnp.concatenate([wg, jnp.zeros((LANES - N_GATES, wg.shape[1]), BF16)], axis=0)
        return lax.dot_general(u, wg, _NT, preferred_element_type=F32) + bgate_ref[...]

    @pl.when(j == 0)
    def _():
        for r0 in range(0, tm, NORM_STRIP):
            rs = slice(r0, r0 + NORM_STRIP)
            u_scr[rs, :] = (_rms(h_ref[rs, :]) * g_ref[...]).astype(BF16)
        gate_ref[...] = gates(u_scr[0:tm, :])

    @pl.when(jnp.logical_and(i == 0, j == 0))
    def _():
        u = (_rms(xh_ref[...]) * g_ref[...]).astype(BF16)
        u_scr[tm:tm + nx, :] = u
        gatex_ref[...] = gates(u)

    def finish(z, kind, c, sa, sb):
        if kind == "plain":
            return z.astype(BF16)
        z = _rope(z, c[...], sa[...], sb[...])
        if kind == "q":
            z = z * Q_SCALE
        return z.astype(BF16)

    def tile_out(kind, with_extra):
        w = wt_ref[...].astype(BF16)
        if kind != "plain" and tm % ROPE_GROUP == 0:
            for r0 in range(0, tm, ROPE_GROUP):
                rs = slice(r0, r0 + ROPE_GROUP)
                z = lax.dot_general(u_scr[rs, :], w, _NT, preferred_element_type=F32) + b_ref[...]
                z_ref[rs, :] = finish(z, kind, c_ref.at[rs, :], sa_ref.at[rs, :], sb_ref.at[rs, :])
            if with_extra:
                z = (lax.dot_general(u_scr[tm:tm + nx, :], w, _NT, preferred_element_type=F32)
                     + b_ref[...])
                zx_ref[...] = finish(z, kind, xc_ref, xsa_ref, xsb_ref)
            return
        u = u_scr[...] if with_extra else u_scr[0:tm, :]
        z = lax.dot_general(u, w, _NT, preferred_element_type=F32) + b_ref[...]
        z_ref[...] = finish(z[0:tm], kind, c_ref, sa_ref, sb_ref)
        if with_extra:
            zx_ref[...] = finish(z[tm:tm + nx], kind, xc_ref, xsa_ref, xsb_ref)

    is_q = j == COL_AQ // tn
    is_k = j == COL_AK // tn
    plain = jnp.logical_not(jnp.logical_or(is_q, is_k))
    for kind, cond in (("q", is_q), ("k", is_k), ("plain", plain)):
        pl.when(jnp.logical_and(cond, i == 0))(functools.partial(tile_out, kind, True))
        pl.when(jnp.logical_and(cond, i > 0))(functools.partial(tile_out, kind, False))


def _inproj(h, xh, g, w_t, b_main, b_gate, rope, rope_x, *, tm):
    rows, d = h.shape
    nx = xh.shape[0]
    tn = A_QK
    n_a = GATE_OFFSET // tn
    n = w_t.shape[0] - N_GATES
    nj = n // tn
    assert COL_AQ == GATE_OFFSET and GATE_OFFSET % tn == 0 and n % tn == 0

    def w_row(i, j):
        return (pl.multiple_of(jnp.where(j < n_a, j * tn, j * tn + N_GATES), N_GATES), 0)

    table = pl.BlockSpec((tm, LANES), lambda i, j: (i, 0))
    table_x = pl.BlockSpec((nx, LANES), lambda i, j: (0, 0))
    return pl.pallas_call(
        functools.partial(_inproj_kernel, tn=tn),
        out_shape=(jax.ShapeDtypeStruct((rows, n), BF16),
                   jax.ShapeDtypeStruct((rows, LANES), F32),
                   jax.ShapeDtypeStruct((nx, n), BF16),
                   jax.ShapeDtypeStruct((nx, LANES), F32)),
        grid=(rows // tm, nj),
        in_specs=[
            pl.BlockSpec((tm, d), lambda i, j: (i, 0)),
            pl.BlockSpec((1, d), lambda i, j: (0, 0)),
            pl.BlockSpec((pl.Element(tn), pl.Element(d)), w_row),
            pl.BlockSpec((1, tn), lambda i, j: (0, j)),
            pl.BlockSpec((N_GATES, d), lambda i, j: (GATE_OFFSET // N_GATES, 0)),
            pl.BlockSpec((1, LANES), lambda i, j: (0, 0)),
            table, table, table,
            pl.BlockSpec((nx, d), lambda i, j: (0, 0)),
            table_x, table_x, table_x,
        ],
        out_specs=(pl.BlockSpec((tm, tn), lambda i, j: (i, j)),
                   pl.BlockSpec((tm, LANES), lambda i, j: (i, 0)),
                   pl.BlockSpec((nx, tn), lambda i, j: (0, jnp.where(i == 0, j, nj - 1))),
                   pl.BlockSpec((nx, LANES), lambda i, j: (0, 0))),
        scratch_shapes=[pltpu.VMEM((tm + nx, d), BF16)],
        compiler_params=_params(("arbitrary", "arbitrary")),
        name="inproj",
    )(h, g, w_t, b_main, w_t, b_gate, *rope, xh, *rope_x)


CONV_PAD = 8


def _conv_silu(xbuf, n, cw_ref, cb_ref):
    y = cb_ref[...]
    for i in range(CONV_W):
        start = CONV_PAD + i - (CONV_W - 1)
        y = y + cw_ref[i:i + 1, :] * xbuf[start:start + n, :]
    return y * jax.nn.sigmoid(y)


def _log_sigmoid(x):
    return jnp.minimum(x, 0.0) - jnp.log1p(jnp.exp(-jnp.abs(x)))


def _cumsum_rows(x):
    n = x.shape[0]
    r = lax.broadcasted_iota(jnp.int32, (n, n), 0)
    c = lax.broadcasted_iota(jnp.int32, (n, n), 1)
    tril = (r >= c).astype(F32)
    return jnp.dot(tril, x, preferred_element_type=F32, precision=lax.Precision.HIGHEST)


def _state_update(c_scr, m_scr, hd, kf, vext, li_col, b_col, g_tot, m_prev):
    log_w = (g_tot - b_col) + li_col
    m_new = jnp.maximum(g_tot + m_prev, jnp.max(log_w, axis=0, keepdims=True))
    wk = jnp.exp(log_w - m_new)
    decay = jnp.exp(g_tot + m_prev - m_new)
    kw = (kf * wk).astype(BF16)
    upd = lax.dot_general(kw, vext, (((0,), (0,)), ((), ())), preferred_element_type=F32)
    c_scr[hd] = decay * c_scr[hd] + upd
    m_scr[hd] = jnp.broadcast_to(m_new, m_scr.shape[1:])


M_KSCALE = M_DQK ** -0.5


def _mlstm_meta_init(qkm_ref, mvm_ref, gtm_ref, cw_ref, cb_ref, c_scr, m_scr, xbuf):
    nm = qkm_ref.shape[0]
    c_scr[...] = jnp.zeros_like(c_scr)
    m_scr[...] = jnp.zeros_like(m_scr)
    xbuf[0:CONV_PAD, :] = jnp.zeros((CONV_PAD, xbuf.shape[1]), F32)
    xbuf[CONV_PAD:CONV_PAD + nm, :] = qkm_ref[...].astype(F32)
    qk = _conv_silu(xbuf, nm, cw_ref, cb_ref)
    xbuf[0:CONV_PAD, :] = xbuf[nm:nm + CONV_PAD, :]
    gts = gtm_ref[...]
    b_all = _cumsum_rows(_log_sigmoid(gts))
    ones = jnp.ones((nm, LANES), BF16)
    for hd in range(M_HEADS):
        kf = qk[:, M_QK + hd * M_DQK:M_QK + (hd + 1) * M_DQK] * M_KSCALE
        vext = jnp.concatenate([mvm_ref[:, hd * M_DV:(hd + 1) * M_DV], ones], axis=1)
        b_col = b_all[:, M_HEADS + hd:M_HEADS + hd + 1]
        _state_update(c_scr, m_scr, hd, kf, vext, gts[:, hd:hd + 1], b_col,
                      b_col[nm - 1:nm, :], m_scr[hd][0:1, 0:1])


def _mlstm_chunk(qk_ref, mv_ref, mo_ref, gt_ref, cw_ref, cb_ref, ng_ref, o_ref, c_scr, m_scr, xbuf):
    t = qk_ref.shape[0]
    kscale = M_KSCALE
    xbuf[CONV_PAD:CONV_PAD + t, :] = qk_ref[...].astype(F32)
    qk = _conv_silu(xbuf, t, cw_ref, cb_ref)
    xbuf[0:CONV_PAD, :] = xbuf[t:t + CONV_PAD, :]

    gts = gt_ref[...]
    b_all = _cumsum_rows(_log_sigmoid(gts))
    lane = lax.broadcasted_iota(jnp.int32, gts.shape, 1)
    rowform = jnp.where(lane < M_HEADS, gts, b_all).T
    r = lax.broadcasted_iota(jnp.int32, (t, t), 0)
    c = lax.broadcasted_iota(jnp.int32, (t, t), 1)
    causal = r >= c
    ones = jnp.ones((t, LANES), BF16)

    for hd in range(M_HEADS):
        yield
        qh = qk[:, hd * M_DQK:(hd + 1) * M_DQK].astype(BF16)
        kf = qk[:, M_QK + hd * M_DQK:M_QK + (hd + 1) * M_DQK] * kscale
        kh = kf.astype(BF16)
        vext = jnp.concatenate([mv_ref[:, hd * M_DV:(hd + 1) * M_DV], ones], axis=1)
        li_col = gts[:, hd:hd + 1]
        b_col = b_all[:, M_HEADS + hd:M_HEADS + hd + 1]
        li_row = rowform[hd:hd + 1, :]
        b_row = rowform[M_HEADS + hd:M_HEADS + hd + 1, :]
        g_tot = b_col[t - 1:t, :]
        m_prev = m_scr[hd][0:1, 0:1]

        log_d = jnp.where(causal, (b_col - b_row) + li_row, -jnp.inf)
        m_inter = b_col + m_prev
        m_row = jnp.maximum(m_inter, jnp.max(log_d, axis=1, keepdims=True))
        s = lax.dot_general(qh, kh, (((1,), (1,)), ((), ())), preferred_element_type=F32)
        sd = (s * jnp.exp(log_d - m_row)).astype(BF16)
        inter = jnp.dot(qh, c_scr[hd].astype(BF16), preferred_element_type=F32)
        numden = jnp.exp(m_inter - m_row) * inter + jnp.dot(sd, vext, preferred_element_type=F32)
        den = jnp.maximum(jnp.abs(numden[:, M_DV:M_DV + 1]), jnp.exp(-m_row))
        hh = numden[:, :M_DV] / den
        hn = _rms(hh) * ng_ref[:, hd * M_DV:(hd + 1) * M_DV]
        og = jax.nn.sigmoid(mo_ref[:, hd * M_DV:(hd + 1) * M_DV].astype(F32))
        o_ref[:, hd * M_DV:(hd + 1) * M_DV] = (og * hn).astype(BF16)

        _state_update(c_scr, m_scr, hd, kf, vext, li_col, b_col, g_tot, m_prev)


ATT_CHUNK = 64


ATT_STRIP = 64
ATT_GROUP = 256


def _attn_kernel(q_ref, k_ref, v_ref, km_ref, vm_ref, lq1_ref, lk1_ref, lq2_ref, lk2_ref,
                 g_ref, o_ref, q2_scr, s0_scr, s1_scr, p0_scr, p1_scr, m_scr, alpha_scr, acc_scr):
    i = pl.program_id(1)
    tq = q_ref.shape[0]
    tk = tq
    rows = 2 * tq
    ncol = tk // LANES
    nt = (((1,), (1,)), ((), ()))
    q = q_ref[...]
    lane = lax.broadcasted_iota(jnp.int32, q.shape, 1)
    zero = jnp.zeros_like(q)
    q2_scr[0:tq, :] = jnp.where(lane < A_DH, q, zero)
    q2_scr[tq:rows, :] = jnp.where(lane >= A_DH, q, zero)
    ones_k = jnp.ones((tk, LANES), BF16)

    def scores(blk):
        off = pl.multiple_of(blk * tk, tk)
        return lax.dot_general(q2_scr[...], k_ref[pl.ds(off, tk), :], nt, preferred_element_type=F32)

    def softmax_pv(s_ref, p_ref, blk, masked, nxt=None):
        off = pl.multiple_of(blk * tk, tk)
        vext = jnp.concatenate([v_ref[pl.ds(off, tk), :], ones_k], axis=1)
        for g0 in range(0, rows, ATT_GROUP):
            gs = slice(g0, g0 + ATT_GROUP)
            if nxt is not None:
                nxt_ref, nxt_blk = nxt
                noff = pl.multiple_of(nxt_blk * tk, tk)
                nxt_ref[gs, :] = lax.dot_general(q2_scr[gs, :], k_ref[pl.ds(noff, tk), :], nt,
                                                 preferred_element_type=F32)
            softmax_rows(s_ref, p_ref, masked, g0)
            pv = jnp.dot(p_ref[gs, :], vext, preferred_element_type=F32)
            alpha = alpha_scr[gs, :]
            for c in range(2):
                sl = slice(c * LANES, (c + 1) * LANES)
                acc_scr[gs, sl] = alpha * acc_scr[gs, sl] + pv[:, sl]

    def softmax_rows(s_ref, p_ref, masked, g0):
        for st in range(g0 // ATT_STRIP, (g0 + ATT_GROUP) // ATT_STRIP):
            rs = slice(st * ATT_STRIP, (st + 1) * ATT_STRIP)
            if masked:
                qchunk = ((st * ATT_STRIP) % tq) // ATT_CHUNK
                nvis = (qchunk + 1) * ATT_CHUNK
                cvis = lax.broadcasted_iota(jnp.int32, (ATT_STRIP, LANES), 1)
                cols = []
                for c in range(-(-nvis // LANES)):
                    sc = s_ref[rs, c * LANES:(c + 1) * LANES]
                    if (c + 1) * LANES > nvis:
                        sc = jnp.where(cvis < nvis - c * LANES, sc, -jnp.inf)
                    cols.append(sc)
            else:
                cols = [s_ref[rs, c * LANES:(c + 1) * LANES] for c in range(ncol)]
            part = cols[0]
            for sc in cols[1:]:
                part = jnp.maximum(part, sc)
            m_old = m_scr[rs, :]
            m_new = jnp.maximum(m_old, jnp.max(part, axis=1, keepdims=True))
            m_scr[rs, :] = m_new
            alpha_scr[rs, :] = jnp.exp2(m_old - m_new)
            if not masked:
                cols = [s_ref[rs, c * LANES:(c + 1) * LANES] for c in range(ncol)]
            pcols = [jnp.exp2(sc - m_new) for sc in cols]
            pcols += [jnp.zeros_like(pcols[0])] * (ncol - len(cols))
            p_ref[rs, :] = jnp.concatenate(pcols, axis=1).astype(BF16)

    s = lax.dot_general(q2_scr[...], km_ref[...], nt, preferred_element_type=F32)
    m0 = jnp.max(s, axis=1, keepdims=True)
    p = jnp.exp2(s - m0).astype(BF16)
    vext = jnp.concatenate([vm_ref[...], jnp.ones((vm_ref.shape[0], LANES), BF16)], axis=1)
    acc_scr[...] = jnp.dot(p, vext, preferred_element_type=F32)
    m_scr[...] = jnp.broadcast_to(m0, m_scr.shape)

    s0_scr[...] = scores(0)

    def pair(pr):
        softmax_pv(s0_scr, p0_scr, 2 * pr, False, nxt=(s1_scr, 2 * pr + 1))
        softmax_pv(s1_scr, p1_scr, 2 * pr + 1, False, nxt=(s0_scr, 2 * pr + 2))

    npairs = i // 2

    def two_pairs(t, carry):
        pair(2 * t)
        pair(2 * t + 1)
        return carry

    lax.fori_loop(0, npairs // 2, two_pairs, 0)

    @pl.when(npairs % 2 == 1)
    def _():
        pair(npairs - 1)

    @pl.when(i % 2 == 0)
    def _():
        softmax_pv(s0_scr, p0_scr, i, True)

    @pl.when(i % 2 == 1)
    def _():
        softmax_pv(s0_scr, p0_scr, i - 1, False, nxt=(s1_scr, i))
        softmax_pv(s1_scr, p1_scr, i, True)

    lam = (jnp.exp(jnp.sum(lq1_ref[...] * lk1_ref[...], axis=1, keepdims=True))
           - jnp.exp(jnp.sum(lq2_ref[...] * lk2_ref[...], axis=1, keepdims=True)) + LAMBDA_INIT)
    o = acc_scr[:, 0:A_DV] / acc_scr[:, A_DV:A_DV + 1]
    o = o[:tq] - lam * o[tq:]
    o_ref[...] = (_rms(o) * g_ref[...] * (1.0 - LAMBDA_INIT)).astype(BF16)


def _attn(zx, zm, lq1, lk1, lq2, lk2, norm_g, *, tq):
    rows = zx.shape[0]
    nm = zm.shape[0]
    small = pl.BlockSpec((1, A_DH), lambda h, i: (0, 0))
    return pl.pallas_call(
        _attn_kernel,
        out_shape=jax.ShapeDtypeStruct((rows, A_V), BF16),
        grid=(A_HEADS, rows // tq),
        in_specs=[
            pl.BlockSpec((tq, A_DV), lambda h, i: (i, COL_AQ // A_DV + h)),
            pl.BlockSpec((rows, A_DV), lambda h, i: (0, COL_AK // A_DV + h)),
            pl.BlockSpec((rows, A_DV), lambda h, i: (0, COL_AV // A_DV + h)),
            pl.BlockSpec((nm, A_DV), lambda h, i: (0, COL_AK // A_DV + h)),
            pl.BlockSpec((nm, A_DV), lambda h, i: (0, COL_AV // A_DV + h)),
            small, small, small, small,
            pl.BlockSpec((1, A_DV), lambda h, i: (0, h)),
        ],
        out_specs=pl.BlockSpec((tq, A_DV), lambda h, i: (i, h)),
        scratch_shapes=[
            pltpu.VMEM((2 * tq, A_DV), BF16),
            pltpu.VMEM((2 * tq, tq), F32),
            pltpu.VMEM((2 * tq, tq), F32),
            pltpu.VMEM((2 * tq, tq), BF16),
            pltpu.VMEM((2 * tq, tq), BF16),
            pltpu.VMEM((2 * tq, LANES), F32),
            pltpu.VMEM((2 * tq, LANES), F32),
            pltpu.VMEM((2 * tq, 2 * LANES), F32),
        ],
        compiler_params=_params(("parallel", "arbitrary")),
        name="diffattn",
    )(zx, zx, zx, zm, zm, lq1, lk1, lq2, lk2, norm_g)


def _mixout_kernel(ha_ref, gm_ref, ga_ref, h_ref, mw_ref, aw_ref, wo_ref, gpost_ref,
                   qk_ref, mv_ref, mo_ref, gt_ref, qkm_ref, mvm_ref, gtm_ref, cw_ref, cb_ref, ng_ref,
                   o_ref, hm_scr, hm_prev_scr, c_scr, m_scr, xbuf):
    s = pl.program_id(0)
    last = pl.num_programs(0) - 1

    def merge():
        hm_prev_scr[...] = hm_scr[...]
        br_m = jnp.dot(hm_prev_scr[...], mw_ref[...], preferred_element_type=F32)
        yield
        br_a = jnp.dot(ha_ref[...], aw_ref[...], preferred_element_type=F32)
        yield
        y = (jax.nn.sigmoid(gm_ref[...].astype(F32)) * br_m
             + jax.nn.sigmoid(ga_ref[...].astype(F32)) * br_a)
        yield
        out = jnp.dot(y.astype(BF16), wo_ref[...], preferred_element_type=F32)
        yield
        o_ref[...] = h_ref[...] + _rms(out) * gpost_ref[...]

    def mlstm():
        return _mlstm_chunk(qk_ref, mv_ref, mo_ref, gt_ref, cw_ref, cb_ref, ng_ref, hm_scr,
                            c_scr, m_scr, xbuf)

    def run(*gens):
        gens = list(gens)
        while gens:
            for g in list(gens):
                try:
                    next(g)
                except StopIteration:
                    gens.remove(g)

    @pl.when(s == 0)
    def _():
        _mlstm_meta_init(qkm_ref, mvm_ref, gtm_ref, cw_ref, cb_ref, c_scr, m_scr, xbuf)
        run(mlstm())

    @pl.when(jnp.logical_and(s > 0, s < last))
    def _():
        run(merge(), mlstm())

    @pl.when(s == last)
    def _():
        run(merge())


def _mixout(zx, gx, zm, gm, ha, h, conv_w, conv_b, m_norm_g, mw, aw, wo, g_post, *, t):
    rows, d = h.shape
    nm = zm.shape[0]
    n = rows // t
    w_qk = 2 * M_QK
    const = lambda s: (0, 0)
    prev = lambda col: (lambda s: (jnp.maximum(s - 1, 0), col))
    cur = lambda col: (lambda s: (jnp.minimum(s, n - 1), col))
    return pl.pallas_call(
        _mixout_kernel,
        out_shape=jax.ShapeDtypeStruct((rows, d), F32),
        grid=(n + 1,),
        in_specs=[
            pl.BlockSpec((t, A_V), prev(0)),
            pl.BlockSpec((t, d), prev(COL_GM // d)),
            pl.BlockSpec((t, d), prev(COL_GM // d + 1)),
            pl.BlockSpec((t, d), prev(0)),
            pl.BlockSpec((M_V, d), const),
            pl.BlockSpec((A_V, d), const),
            pl.BlockSpec((d, d), const),
            pl.BlockSpec((1, d), const),
            pl.BlockSpec((t, w_qk), cur(COL_QK // w_qk)),
            pl.BlockSpec((t, M_V), cur(COL_MV // M_V)),
            pl.BlockSpec((t, M_V), cur(COL_MO // M_V)),
            pl.BlockSpec((t, LANES), cur(0)),
            pl.BlockSpec((nm, w_qk), lambda s: (0, COL_QK // w_qk)),
            pl.BlockSpec((nm, M_V), lambda s: (0, COL_MV // M_V)),
            pl.BlockSpec((nm, LANES), const),
            pl.BlockSpec((CONV_W, w_qk), const),
            pl.BlockSpec((1, w_qk), const),
            pl.BlockSpec((1, M_V), const),
        ],
        out_specs=pl.BlockSpec((t, d), prev(0)),
        scratch_shapes=[
            pltpu.VMEM((t, M_V), BF16),
            pltpu.VMEM((t, M_V), BF16),
            pltpu.VMEM((M_HEADS, M_DQK, M_DV + LANES), F32),
            pltpu.VMEM((M_HEADS, 8, LANES), F32),
            pltpu.VMEM((t + CONV_PAD, w_qk), F32),
        ],
        compiler_params=_params(("arbitrary",)),
        name="mixout",
    )(ha, zx, zx, h, mw, aw, wo, g_post, zx, zx, zx, gx, zm, zm, gm, conv_w, conv_b, m_norm_g)


def _rope_tables(n_rows):
    half = ROT_DIM // 2
    f32 = np.float32
    inv_freq = np.power(f32(ROPE_THETA), -np.arange(0, ROT_DIM, 2, dtype=f32) / f32(ROT_DIM))
    ang = np.arange(n_rows, dtype=f32)[:, None] * inv_freq[None, :]
    cos, sin = np.cos(ang).astype(f32), np.sin(ang).astype(f32)
    zeros = np.zeros((n_rows, A_DH - ROT_DIM), f32)
    zh = np.zeros((n_rows, half), f32)
    c = np.concatenate([cos, cos, zeros + f32(1.0)], axis=1)
    sa = np.concatenate([-sin, zh, zeros], axis=1)
    sb = np.concatenate([zh, sin, zeros], axis=1)
    return tuple(np.concatenate([tab, tab], axis=1) for tab in (c, sa, sb))


def kernel(x, meta, ffn1_pre_g, ffn1_post_g, ffn1_w_gate, ffn1_w_up, ffn1_w_down, mix_pre_g, mix_post_g, w_in, b_in, m_conv_w, m_conv_b, m_norm_g, m_w_branch, a_lambda_q1, a_lambda_k1, a_lambda_q2, a_lambda_k2, a_norm_g, a_w_branch, w_out, ffn2_pre_g, ffn2_post_g, ffn2_w_gate, ffn2_w_up, ffn2_w_down):
    batch, seq, d = x.shape
    assert batch == 1 and meta.shape == (N_META, d)
    assert (2 * d) % A_QK == 0 and COL_GM % d == 0
    l = 0
    xr = x.reshape(seq, d)
    row = lambda v: v[l].reshape(1, -1)
    bf = lambda w: w[l].astype(BF16)

    w_in_l, b_in_l = w_in[l], b_in[l]
    gate_end = GATE_OFFSET + N_GATES
    w_in_t = w_in_l.T
    b_main = jnp.concatenate([b_in_l[:GATE_OFFSET], b_in_l[gate_end:]]).reshape(1, -1)
    b_gate = jnp.pad(b_in_l[GATE_OFFSET:gate_end], (0, LANES - N_GATES)).reshape(1, -1)
    rope = _rope_tables(N_META + seq)
    rope_m = tuple(tab[:N_META] for tab in rope)
    rope_x = tuple(tab[N_META:] for tab in rope)

    hx, hmeta = _ffn(xr, row(ffn1_pre_g), row(ffn1_post_g), ffn1_w_gate, ffn1_w_up, ffn1_w_down,
                     extra=meta, layer=l, tm=FFN_TM, tf=FFN_TF)
    zx, gx, zm, gm = _inproj(hx, hmeta, row(mix_pre_g), w_in_t, b_main, b_gate, rope_x, rope_m,
                             tm=1024)

    ha = _attn(zx, zm, row(a_lambda_q1), row(a_lambda_k1), row(a_lambda_q2), row(a_lambda_k2),
               row(a_norm_g), tq=512)
    h2 = _mixout(zx, gx, zm, gm, ha, hx, m_conv_w[l], row(m_conv_b), row(m_norm_g),
                 bf(m_w_branch), bf(a_w_branch), bf(w_out), row(mix_post_g), t=256)
    out = _ffn(h2, row(ffn2_pre_g), row(ffn2_post_g), ffn2_w_gate, ffn2_w_up, ffn2_w_down,
               layer=l, tm=FFN_TM, tf=FFN_TF)
    return out.reshape(batch, seq, d)
```

```python
import functools

import jax
import jax.numpy as jnp
import numpy as np
from jax import lax
from jax.experimental import pallas as pl
from jax.experimental.pallas import tpu as pltpu

F32 = jnp.float32
BF16 = jnp.bfloat16

EPS = 1e-6
N_META = 16
M_HEADS = 4
M_DQK = 128
M_DV = 256
M_QK = M_HEADS * M_DQK
M_V = M_HEADS * M_DV
CONV_W = 4
A_HEADS = 8
A_DH = 64
A_DV = 2 * A_DH
A_QK = A_HEADS * 2 * A_DH
A_V = A_HEADS * A_DV
ROT_DIM = A_DH // 4
ROPE_THETA = 500000.0
LAMBDA_INIT = 0.8 - 0.6 * 1.0
Q_SCALE = A_DH ** -0.5 * 1.4426950408889634

LANES = 128
GATE_OFFSET = 2 * M_QK + 2 * M_V
N_GATES = 2 * M_HEADS
COL_QK = 0
COL_MV = 2 * M_QK
COL_MO = COL_MV + M_V
COL_AQ = COL_MO + M_V
COL_AK = COL_AQ + A_QK
COL_AV = COL_AK + A_QK
COL_GM = COL_AV + A_V

VMEM_LIMIT = 56 * 1024 * 1024
VMEM_PHYSICAL = 64 * 1024 * 1024


def _params(sem, vmem_limit=VMEM_LIMIT):
    assert vmem_limit < VMEM_PHYSICAL
    return pltpu.CompilerParams(dimension_semantics=sem, vmem_limit_bytes=vmem_limit)


def _rms(x):
    return x * lax.rsqrt(jnp.mean(x * x, axis=-1, keepdims=True) + EPS)


NORM_STRIP = 16


FFN_TM = 1024
FFN_TF = 256
FFN_VMEM_LIMIT = 62 * 1024 * 1024


def _ffn_kernel(h_ref, gpre_ref, gpost_ref, wg_ref, wu_ref, wd_ref, *rest, n_extra):
    if n_extra:
        x_ref, o_ref, ox_ref, u_scr, accx_scr = rest
    else:
        o_ref, u_scr = rest
    i = pl.program_id(0)
    f = pl.program_id(1)
    last = pl.num_programs(1) - 1
    tm = h_ref.shape[0]

    @pl.when(f == 0)
    def _():
        for r0 in range(0, tm, NORM_STRIP):
            rs = slice(r0, r0 + NORM_STRIP)
            u_scr[rs, :] = (_rms(h_ref[rs, :]) * gpre_ref[...]).astype(BF16)
        o_ref[...] = jnp.zeros_like(o_ref)

    def swiglu_down(u):
        g = jnp.dot(u, wg_ref[...].astype(BF16), preferred_element_type=F32)
        up = jnp.dot(u, wu_ref[...].astype(BF16), preferred_element_type=F32)
        a = (g * jax.nn.sigmoid(g) * up).astype(BF16)
        return jnp.dot(a, wd_ref[...].astype(BF16), preferred_element_type=F32)

    if n_extra:
        @pl.when(jnp.logical_and(i == 0, f == 0))
        def _():
            u_scr[tm:tm + n_extra, :] = (_rms(x_ref[...]) * gpre_ref[...]).astype(BF16)
            accx_scr[...] = jnp.zeros_like(accx_scr)

        @pl.when(i == 0)
        def _():
            down = swiglu_down(u_scr[...])
            o_ref[...] += down[0:tm]
            accx_scr[...] += down[tm:tm + n_extra]

        @pl.when(i > 0)
        def _():
            o_ref[...] += swiglu_down(u_scr[0:tm, :])

        @pl.when(jnp.logical_and(i == 0, f == last))
        def _():
            ox_ref[...] = x_ref[...] + 0.5 * (_rms(accx_scr[...]) * gpost_ref[...])
    else:
        o_ref[...] += swiglu_down(u_scr[...])

    @pl.when(f == last)
    def _():
        for r0 in range(0, tm, NORM_STRIP):
            rs = slice(r0, r0 + NORM_STRIP)
            o_ref[rs, :] = h_ref[rs, :] + 0.5 * (_rms(o_ref[rs, :]) * gpost_ref[...])


def _ffn(h, g_pre, g_post, w_gate, w_up, w_down, extra=None, *, layer, tm, tf):
    rows, d = h.shape
    d_ff = w_gate.shape[2]
    n_extra = 0 if extra is None else extra.shape[0]
    vec = pl.BlockSpec((1, d), lambda i, f: (0, 0))
    tile = pl.BlockSpec((tm, d), lambda i, f: (i, 0))
    in_specs = [tile, vec, vec,
                pl.BlockSpec((None, d, tf), lambda i, f: (layer, 0, f)),
                pl.BlockSpec((None, d, tf), lambda i, f: (layer, 0, f)),
                pl.BlockSpec((None, tf, d), lambda i, f: (layer, f, 0))]
    args = [h, g_pre, g_post, w_gate, w_up, w_down]
    out_shape = jax.ShapeDtypeStruct((rows, d), F32)
    out_specs = tile
    scratch = [pltpu.VMEM((tm + n_extra, d), BF16)]
    if n_extra:
        whole = pl.BlockSpec((n_extra, d), lambda i, f: (0, 0))
        in_specs.append(whole)
        args.append(extra)
        out_shape = (out_shape, jax.ShapeDtypeStruct((n_extra, d), F32))
        out_specs = (tile, whole)
        scratch.append(pltpu.VMEM((n_extra, d), F32))
    return pl.pallas_call(
        functools.partial(_ffn_kernel, n_extra=n_extra),
        out_shape=out_shape,
        grid=(rows // tm, d_ff // tf),
        in_specs=in_specs,
        out_specs=out_specs,
        scratch_shapes=scratch,
        compiler_params=_params(("arbitrary", "arbitrary"), vmem_limit=FFN_VMEM_LIMIT),
        name="ffn",
    )(*args)


def _rope(z, c, sa, sb):
    outs = []
    for grp in range(z.shape[1] // LANES):
        x = z[:, grp * LANES:(grp + 1) * LANES]
        outs.append(x * c + pltpu.roll(x, LANES - ROT_DIM // 2, 1) * sa
                    + pltpu.roll(x, ROT_DIM // 2, 1) * sb)
    return jnp.concatenate(outs, axis=1)


_NT = (((1,), (1,)), ((), ()))
ROPE_GROUP = 256


def _inproj_kernel(h_ref, g_ref, wt_ref, b_ref, wgate_ref, bgate_ref, c_ref, sa_ref, sb_ref,
                   xh_ref, xc_ref, xsa_ref, xsb_ref, z_ref, gate_ref, zx_ref, gatex_ref, u_scr,
                   *, tn):
    i = pl.program_id(0)
    j = pl.program_id(1)
    tm = h_ref.shape[0]
    nx = xh_ref.shape[0]

    def gates(u):
        wg = wgate_ref[...].astype(BF16)
        wg = jnp.concatenate([wg, jnp.zeros((LANES - N_GATES, wg.shape[1]), BF16)], axis=0)
        return lax.dot_general(u, wg, _NT, preferred_element_type=F32) + bgate_ref[...]

    @pl.when(j == 0)
    def _():
        for r0 in range(0, tm, NORM_STRIP):
            rs = slice(r0, r0 + NORM_STRIP)
            u_scr[rs, :] = (_rms(h_ref[rs, :]) * g_ref[...]).astype(BF16)
        gate_ref[...] = gates(u_scr[0:tm, :])

    @pl.when(jnp.logical_and(i == 0, j == 0))
    def _():
        u = (_rms(xh_ref[...]) * g_ref[...]).astype(BF16)
        u_scr[tm:tm + nx, :] = u
        gatex_ref[...] = gates(u)

    def finish(z, kind, c, sa, sb):
        if kind == "plain":
            return z.astype(BF16)
        z = _rope(z, c[...], sa[...], sb[...])
        if kind == "q":
            z = z * Q_SCALE
        return z.astype(BF16)

    def tile_out(kind, with_extra):
        w = wt_ref[...].astype(BF16)
        if kind != "plain" and tm % ROPE_GROUP == 0:
            for r0 in range(0, tm, ROPE_GROUP):
                rs = slice(r0, r0 + ROPE_GROUP)
                z = lax.dot_general(u_scr[rs, :], w, _NT, preferred_element_type=F32) + b_ref[...]
                z_ref[rs, :] = finish(z, kind, c_ref.at[rs, :], sa_ref.at[rs, :], sb_ref.at[rs, :])
            if with_extra:
                z = (lax.dot_general(u_scr[tm:tm + nx, :], w, _NT, preferred_element_type=F32)
                     + b_ref[...])
                zx_ref[...] = finish(z, kind, xc_ref, xsa_ref, xsb_ref)
            return
        u = u_scr[...] if with_extra else u_scr[0:tm, :]
        z = lax.dot_general(u, w, _NT, preferred_element_type=F32) + b_ref[...]
        z_ref[...] = finish(z[0:tm], kind, c_ref, sa_ref, sb_ref)
        if with_extra:
            zx_ref[...] = finish(z[tm:tm + nx], kind, xc_ref, xsa_ref, xsb_ref)

    is_q = j == COL_AQ // tn
    is_k = j == COL_AK // tn
    plain = jnp.logical_not(jnp.logical_or(is_q, is_k))
    for kind, cond in (("q", is_q), ("k", is_k), ("plain", plain)):
        pl.when(jnp.logical_and(cond, i == 0))(functools.partial(tile_out, kind, True))
        pl.when(jnp.logical_and(cond, i > 0))(functools.partial(tile_out, kind, False))


def _inproj(h, xh, g, w_t, b_main, b_gate, rope, rope_x, *, tm):
    rows, d = h.shape
    nx = xh.shape[0]
    tn = A_QK
    n_a = GATE_OFFSET // tn
    n = w_t.shape[0] - N_GATES
    nj = n // tn
    assert COL_AQ == GATE_OFFSET and GATE_OFFSET % tn == 0 and n % tn == 0

    def w_row(i, j):
        return (pl.multiple_of(jnp.where(j < n_a, j * tn, j * tn + N_GATES), N_GATES), 0)

    table = pl.BlockSpec((tm, LANES), lambda i, j: (i, 0))
    table_x = pl.BlockSpec((nx, LANES), lambda i, j: (0, 0))
    return pl.pallas_call(
        functools.partial(_inproj_kernel, tn=tn),
        out_shape=(jax.ShapeDtypeStruct((rows, n), BF16),
                   jax.ShapeDtypeStruct((rows, LANES), F32),
                   jax.ShapeDtypeStruct((nx, n), BF16),
                   jax.ShapeDtypeStruct((nx, LANES), F32)),
        grid=(rows // tm, nj),
        in_specs=[
            pl.BlockSpec((tm, d), lambda i, j: (i, 0)),
            pl.BlockSpec((1, d), lambda i, j: (0, 0)),
            pl.BlockSpec((pl.Element(tn), pl.Element(d)), w_row),
            pl.BlockSpec((1, tn), lambda i, j: (0, j)),
            pl.BlockSpec((N_GATES, d), lambda i, j: (GATE_OFFSET // N_GATES, 0)),
            pl.BlockSpec((1, LANES), lambda i, j: (0, 0)),
            table, table, table,
            pl.BlockSpec((nx, d), lambda i, j: (0, 0)),
            table_x, table_x, table_x,
        ],
        out_specs=(pl.BlockSpec((tm, tn), lambda i, j: (i, j)),
                   pl.BlockSpec((tm, LANES), lambda i, j: (i, 0)),
                   pl.BlockSpec((nx, tn), lambda i, j: (0, jnp.where(i == 0, j, nj - 1))),
                   pl.BlockSpec((nx, LANES), lambda i, j: (0, 0))),
        scratch_shapes=[pltpu.VMEM((tm + nx, d), BF16)],
        compiler_params=_params(("arbitrary", "arbitrary")),
        name="inproj",
    )(h, g, w_t, b_main, w_t, b_gate, *rope, xh, *rope_x)


CONV_PAD = 8


def _conv_silu(xbuf, n, cw_ref, cb_ref):
    y = cb_ref[...]
    for i in range(CONV_W):
        start = CONV_PAD + i - (CONV_W - 1)
        y = y + cw_ref[i:i + 1, :] * xbuf[start:start + n, :]
    return y * jax.nn.sigmoid(y)


def _log_sigmoid(x):
    return jnp.minimum(x, 0.0) - jnp.log1p(jnp.exp(-jnp.abs(x)))


def _cumsum_rows(x):
    n = x.shape[0]
    r = lax.broadcasted_iota(jnp.int32, (n, n), 0)
    c = lax.broadcasted_iota(jnp.int32, (n, n), 1)
    tril = (r >= c).astype(F32)
    return jnp.dot(tril, x, preferred_element_type=F32, precision=lax.Precision.HIGHEST)


def _state_update(c_scr, m_scr, hd, kf, vext, li_col, b_col, g_tot, m_prev):
    log_w = (g_tot - b_col) + li_col
    m_new = jnp.maximum(g_tot + m_prev, jnp.max(log_w, axis=0, keepdims=True))
    wk = jnp.exp(log_w - m_new)
    decay = jnp.exp(g_tot + m_prev - m_new)
    kw = (kf * wk).astype(BF16)
    upd = lax.dot_general(kw, vext, (((0,), (0,)), ((), ())), preferred_element_type=F32)
    c_scr[hd] = decay * c_scr[hd] + upd
    m_scr[hd] = jnp.broadcast_to(m_new, m_scr.shape[1:])


M_KSCALE = M_DQK ** -0.5


def _mlstm_meta_init(qkm_ref, mvm_ref, gtm_ref, cw_ref, cb_ref, c_scr, m_scr, xbuf):
    nm = qkm_ref.shape[0]
    c_scr[...] = jnp.zeros_like(c_scr)
    m_scr[...] = jnp.zeros_like(m_scr)
    xbuf[0:CONV_PAD, :] = jnp.zeros((CONV_PAD, xbuf.shape[1]), F32)
    xbuf[CONV_PAD:CONV_PAD + nm, :] = qkm_ref[...].astype(F32)
    qk = _conv_silu(xbuf, nm, cw_ref, cb_ref)
    xbuf[0:CONV_PAD, :] = xbuf[nm:nm + CONV_PAD, :]
    gts = gtm_ref[...]
    b_all = _cumsum_rows(_log_sigmoid(gts))
    ones = jnp.ones((nm, LANES), BF16)
    for hd in range(M_HEADS):
        kf = qk[:, M_QK + hd * M_DQK:M_QK + (hd + 1) * M_DQK] * M_KSCALE
        vext = jnp.concatenate([mvm_ref[:, hd * M_DV:(hd + 1) * M_DV], ones], axis=1)
        b_col = b_all[:, M_HEADS + hd:M_HEADS + hd + 1]
        _state_update(c_scr, m_scr, hd, kf, vext, gts[:, hd:hd + 1], b_col,
                      b_col[nm - 1:nm, :], m_scr[hd][0:1, 0:1])


def _mlstm_chunk(qk_ref, mv_ref, mo_ref, gt_ref, cw_ref, cb_ref, ng_ref, o_ref, c_scr, m_scr, xbuf):
    t = qk_ref.shape[0]
    kscale = M_KSCALE
    xbuf[CONV_PAD:CONV_PAD + t, :] = qk_ref[...].astype(F32)
    qk = _conv_silu(xbuf, t, cw_ref, cb_ref)
    xbuf[0:CONV_PAD, :] = xbuf[t:t + CONV_PAD, :]

    gts = gt_ref[...]
    b_all = _cumsum_rows(_log_sigmoid(gts))
    lane = lax.broadcasted_iota(jnp.int32, gts.shape, 1)
    rowform = jnp.where(lane < M_HEADS, gts, b_all).T
    r = lax.broadcasted_iota(jnp.int32, (t, t), 0)
    c = lax.broadcasted_iota(jnp.int32, (t, t), 1)
    causal = r >= c
    ones = jnp.ones((t, LANES), BF16)

    for hd in range(M_HEADS):
        yield
        qh = qk[:, hd * M_DQK:(hd + 1) * M_DQK].astype(BF16)
        kf = qk[:, M_QK + hd * M_DQK:M_QK + (hd + 1) * M_DQK] * kscale
        kh = kf.astype(BF16)
        vext = jnp.concatenate([mv_ref[:, hd * M_DV:(hd + 1) * M_DV], ones], axis=1)
        li_col = gts[:, hd:hd + 1]
        b_col = b_all[:, M_HEADS + hd:M_HEADS + hd + 1]
        li_row = rowform[hd:hd + 1, :]
        b_row = rowform[M_HEADS + hd:M_HEADS + hd + 1, :]
        g_tot = b_col[t - 1:t, :]
        m_prev = m_scr[hd][0:1, 0:1]

        log_d = jnp.where(causal, (b_col - b_row) + li_row, -jnp.inf)
        m_inter = b_col + m_prev
        m_row = jnp.maximum(m_inter, jnp.max(log_d, axis=1, keepdims=True))
        s = lax.dot_general(qh, kh, (((1,), (1,)), ((), ())), preferred_element_type=F32)
        sd = (s * jnp.exp(log_d - m_row)).astype(BF16)
        inter = jnp.dot(qh, c_scr[hd].astype(BF16), preferred_element_type=F32)
        numden = jnp.exp(m_inter - m_row) * inter + jnp.dot(sd, vext, preferred_element_type=F32)
        den = jnp.maximum(jnp.abs(numden[:, M_DV:M_DV + 1]), jnp.exp(-m_row))
        hh = numden[:, :M_DV] / den
        hn = _rms(hh) * ng_ref[:, hd * M_DV:(hd + 1) * M_DV]
        og = jax.nn.sigmoid(mo_ref[:, hd * M_DV:(hd + 1) * M_DV].astype(F32))
        o_ref[:, hd * M_DV:(hd + 1) * M_DV] = (og * hn).astype(BF16)

        _state_update(c_scr, m_scr, hd, kf, vext, li_col, b_col, g_tot, m_prev)


ATT_CHUNK = 64


ATT_STRIP = 64
ATT_GROUP = 256
ATT_HEADS_PER_STEP = 2


def _attn_kernel(q_ref, k_ref, v_ref, km_ref, vm_ref, lq1_ref, lk1_ref, lq2_ref, lk2_ref,
                 g_ref, o_ref, q2_scr, s0_scr, s1_scr, p0_scr, p1_scr, m_scr, alpha_scr, acc_scr):
    i = pl.program_id(1)
    tq = q_ref.shape[0]
    tk = tq
    rows = 2 * tq
    ncol = tk // LANES
    nheads = q_ref.shape[1] // A_DV
    nt = (((1,), (1,)), ((), ()))
    ones_k = jnp.ones((tk, LANES), BF16)
    s_slots = (s0_scr, s1_scr)
    p_slots = (p0_scr, p1_scr)

    def cols_of(hh):
        return slice(hh * A_DV, (hh + 1) * A_DV)

    def kblock(hh, blk):
        off = pl.multiple_of(blk * tk, tk)
        return k_ref[pl.ds(off, tk), cols_of(hh)]

    def softmax_pv(hh, slot, blk, masked, nxt=None):
        s_ref, p_ref = s_slots[slot].at[hh], p_slots[slot].at[hh]
        m_ref, alpha_ref, acc_ref = m_scr.at[hh], alpha_scr.at[hh], acc_scr.at[hh]
        off = pl.multiple_of(blk * tk, tk)
        vext = jnp.concatenate([v_ref[pl.ds(off, tk), cols_of(hh)], ones_k], axis=1)
        for g0 in range(0, rows, ATT_GROUP):
            gs = slice(g0, g0 + ATT_GROUP)
            if nxt is not None:
                nxt_slot, nxt_blk = nxt
                s_slots[nxt_slot].at[hh][gs, :] = lax.dot_general(
                    q2_scr.at[hh][gs, :], kblock(hh, nxt_blk), nt, preferred_element_type=F32)
            softmax_rows(s_ref, p_ref, m_ref, alpha_ref, masked, g0)
            pv = jnp.dot(p_ref[gs, :], vext, preferred_element_type=F32)
            alpha = alpha_ref[gs, :]
            for c in range(2):
                sl = slice(c * LANES, (c + 1) * LANES)
                acc_ref[gs, sl] = alpha * acc_ref[gs, sl] + pv[:, sl]
            yield

    def softmax_rows(s_ref, p_ref, m_scr, alpha_scr, masked, g0):
        for st in range(g0 // ATT_STRIP, (g0 + ATT_GROUP) // ATT_STRIP):
            rs = slice(st * ATT_STRIP, (st + 1) * ATT_STRIP)
            if masked:
                qchunk = ((st * ATT_STRIP) % tq) // ATT_CHUNK
                nvis = (qchunk + 1) * ATT_CHUNK
                cvis = lax.broadcasted_iota(jnp.int32, (ATT_STRIP, LANES), 1)
                cols = []
                for c in range(-(-nvis // LANES)):
                    sc = s_ref[rs, c * LANES:(c + 1) * LANES]
                    if (c + 1) * LANES > nvis:
                        sc = jnp.where(cvis < nvis - c * LANES, sc, -jnp.inf)
                    cols.append(sc)
            else:
                cols = [s_ref[rs, c * LANES:(c + 1) * LANES] for c in range(ncol)]
            part = cols[0]
            for sc in cols[1:]:
                part = jnp.maximum(part, sc)
            m_old = m_scr[rs, :]
            m_new = jnp.maximum(m_old, jnp.max(part, axis=1, keepdims=True))
            m_scr[rs, :] = m_new
            alpha_scr[rs, :] = jnp.exp2(m_old - m_new)
            if not masked:
                cols = [s_ref[rs, c * LANES:(c + 1) * LANES] for c in range(ncol)]
            pcols = [jnp.exp2(sc - m_new) for sc in cols]
            pcols += [jnp.zeros_like(pcols[0])] * (ncol - len(cols))
            p_ref[rs, :] = jnp.concatenate(pcols, axis=1).astype(BF16)

    def prologue(hh):
        q = q_ref[:, cols_of(hh)]
        lane = lax.broadcasted_iota(jnp.int32, q.shape, 1)
        zero = jnp.zeros_like(q)
        q2 = q2_scr.at[hh]
        q2[0:tq, :] = jnp.where(lane < A_DH, q, zero)
        q2[tq:rows, :] = jnp.where(lane >= A_DH, q, zero)
        yield
        s = lax.dot_general(q2[...], km_ref[:, cols_of(hh)], nt, preferred_element_type=F32)
        m0 = jnp.max(s, axis=1, keepdims=True)
        p = jnp.exp2(s - m0).astype(BF16)
        vext = jnp.concatenate([vm_ref[:, cols_of(hh)],
                                jnp.ones((vm_ref.shape[0], LANES), BF16)], axis=1)
        acc_scr[hh] = jnp.dot(p, vext, preferred_element_type=F32)
        m_scr[hh] = jnp.broadcast_to(m0, m_scr.shape[1:])
        yield
        s0_scr[hh] = lax.dot_general(q2[...], kblock(hh, 0), nt, preferred_element_type=F32)

    def chain(*gens):
        for g in gens:
            yield from g

    def run(make):
        gens = [make(hh) for hh in range(nheads)]
        while gens:
            for g in list(gens):
                try:
                    next(g)
                except StopIteration:
                    gens.remove(g)

    run(prologue)

    def pair(pr):
        run(lambda hh: chain(softmax_pv(hh, 0, 2 * pr, False, nxt=(1, 2 * pr + 1)),
                             softmax_pv(hh, 1, 2 * pr + 1, False, nxt=(0, 2 * pr + 2))))

    npairs = i // 2

    def two_pairs(t, carry):
        pair(2 * t)
        pair(2 * t + 1)
        return carry

    lax.fori_loop(0, npairs // 2, two_pairs, 0)

    @pl.when(npairs % 2 == 1)
    def _():
        pair(npairs - 1)

    @pl.when(i % 2 == 0)
    def _():
        run(lambda hh: softmax_pv(hh, 0, i, True))

    @pl.when(i % 2 == 1)
    def _():
        run(lambda hh: chain(softmax_pv(hh, 0, i - 1, False, nxt=(1, i)),
                             softmax_pv(hh, 1, i, True)))

    lam = (jnp.exp(jnp.sum(lq1_ref[...] * lk1_ref[...], axis=1, keepdims=True))
           - jnp.exp(jnp.sum(lq2_ref[...] * lk2_ref[...], axis=1, keepdims=True)) + LAMBDA_INIT)
    for hh in range(nheads):
        o = acc_scr[hh, :, 0:A_DV] / acc_scr[hh, :, A_DV:A_DV + 1]
        o = o[:tq] - lam * o[tq:]
        o_ref[:, cols_of(hh)] = (_rms(o) * g_ref[:, cols_of(hh)] * (1.0 - LAMBDA_INIT)).astype(BF16)


def _attn(zx, zm, lq1, lk1, lq2, lk2, norm_g, *, tq):
    rows = zx.shape[0]
    nm = zm.shape[0]
    nh = ATT_HEADS_PER_STEP
    w = nh * A_DV
    small = pl.BlockSpec((1, A_DH), lambda h, i: (0, 0))
    return pl.pallas_call(
        _attn_kernel,
        out_shape=jax.ShapeDtypeStruct((rows, A_V), BF16),
        grid=(A_HEADS // nh, rows // tq),
        in_specs=[
            pl.BlockSpec((tq, w), lambda h, i: (i, COL_AQ // w + h)),
            pl.BlockSpec((rows, w), lambda h, i: (0, COL_AK // w + h)),
            pl.BlockSpec((rows, w), lambda h, i: (0, COL_AV // w + h)),
            pl.BlockSpec((nm, w), lambda h, i: (0, COL_AK // w + h)),
            pl.BlockSpec((nm, w), lambda h, i: (0, COL_AV // w + h)),
            small, small, small, small,
            pl.BlockSpec((1, w), lambda h, i: (0, h)),
        ],
        out_specs=pl.BlockSpec((tq, w), lambda h, i: (i, h)),
        scratch_shapes=[
            pltpu.VMEM((nh, 2 * tq, A_DV), BF16),
            pltpu.VMEM((nh, 2 * tq, tq), F32),
            pltpu.VMEM((nh, 2 * tq, tq), F32),
            pltpu.VMEM((nh, 2 * tq, tq), BF16),
            pltpu.VMEM((nh, 2 * tq, tq), BF16),
            pltpu.VMEM((nh, 2 * tq, LANES), F32),
            pltpu.VMEM((nh, 2 * tq, LANES), F32),
            pltpu.VMEM((nh, 2 * tq, 2 * LANES), F32),
        ],
        compiler_params=_params(("parallel", "arbitrary")),
        name="diffattn",
    )(zx, zx, zx, zm, zm, lq1, lk1, lq2, lk2, norm_g)


def _mixout_kernel(ha_ref, gm_ref, ga_ref, h_ref, mw_ref, aw_ref, wo_ref, gpost_ref,
                   qk_ref, mv_ref, mo_ref, gt_ref, qkm_ref, mvm_ref, gtm_ref, cw_ref, cb_ref, ng_ref,
                   o_ref, hm_scr, hm_prev_scr, c_scr, m_scr, xbuf):
    s = pl.program_id(0)
    last = pl.num_programs(0) - 1

    def merge():
        hm_prev_scr[...] = hm_scr[...]
        br_m = jnp.dot(hm_prev_scr[...], mw_ref[...], preferred_element_type=F32)
        yield
        br_a = jnp.dot(ha_ref[...], aw_ref[...], preferred_element_type=F32)
        yield
        y = (jax.nn.sigmoid(gm_ref[...].astype(F32)) * br_m
             + jax.nn.sigmoid(ga_ref[...].astype(F32)) * br_a)
        yield
        out = jnp.dot(y.astype(BF16), wo_ref[...], preferred_element_type=F32)
        yield
        o_ref[...] = h_ref[...] + _rms(out) * gpost_ref[...]

    def mlstm():
        return _mlstm_chunk(qk_ref, mv_ref, mo_ref, gt_ref, cw_ref, cb_ref, ng_ref, hm_scr,
                            c_scr, m_scr, xbuf)

    def run(*gens):
        gens = list(gens)
        while gens:
            for g in list(gens):
                try:
                    next(g)
                except StopIteration:
                    gens.remove(g)

    @pl.when(s == 0)
    def _():
        _mlstm_meta_init(qkm_ref, mvm_ref, gtm_ref, cw_ref, cb_ref, c_scr, m_scr, xbuf)
        run(mlstm())

    @pl.when(jnp.logical_and(s > 0, s < last))
    def _():
        run(merge(), mlstm())

    @pl.when(s == last)
    def _():
        run(merge())


def _mixout(zx, gx, zm, gm, ha, h, conv_w, conv_b, m_norm_g, mw, aw, wo, g_post, *, t):
    rows, d = h.shape
    nm = zm.shape[0]
    n = rows // t
    w_qk = 2 * M_QK
    const = lambda s: (0, 0)
    prev = lambda col: (lambda s: (jnp.maximum(s - 1, 0), col))
    cur = lambda col: (lambda s: (jnp.minimum(s, n - 1), col))
    return pl.pallas_call(
        _mixout_kernel,
        out_shape=jax.ShapeDtypeStruct((rows, d), F32),
        grid=(n + 1,),
        in_specs=[
            pl.BlockSpec((t, A_V), prev(0)),
            pl.BlockSpec((t, d), prev(COL_GM // d)),
            pl.BlockSpec((t, d), prev(COL_GM // d + 1)),
            pl.BlockSpec((t, d), prev(0)),
            pl.BlockSpec((M_V, d), const),
            pl.BlockSpec((A_V, d), const),
            pl.BlockSpec((d, d), const),
            pl.BlockSpec((1, d), const),
            pl.BlockSpec((t, w_qk), cur(COL_QK // w_qk)),
            pl.BlockSpec((t, M_V), cur(COL_MV // M_V)),
            pl.BlockSpec((t, M_V), cur(COL_MO // M_V)),
            pl.BlockSpec((t, LANES), cur(0)),
            pl.BlockSpec((nm, w_qk), lambda s: (0, COL_QK // w_qk)),
            pl.BlockSpec((nm, M_V), lambda s: (0, COL_MV // M_V)),
            pl.BlockSpec((nm, LANES), const),
            pl.BlockSpec((CONV_W, w_qk), const),
            pl.BlockSpec((1, w_qk), const),
            pl.BlockSpec((1, M_V), const),
        ],
        out_specs=pl.BlockSpec((t, d), prev(0)),
        scratch_shapes=[
            pltpu.VMEM((t, M_V), BF16),
            pltpu.VMEM((t, M_V), BF16),
            pltpu.VMEM((M_HEADS, M_DQK, M_DV + LANES), F32),
            pltpu.VMEM((M_HEADS, 8, LANES), F32),
            pltpu.VMEM((t + CONV_PAD, w_qk), F32),
        ],
        compiler_params=_params(("arbitrary",)),
        name="mixout",
    )(ha, zx, zx, h, mw, aw, wo, g_post, zx, zx, zx, gx, zm, zm, gm, conv_w, conv_b, m_norm_g)


def _rope_tables(n_rows):
    half = ROT_DIM // 2
    f32 = np.float32
    inv_freq = np.power(f32(ROPE_THETA), -np.arange(0, ROT_DIM, 2, dtype=f32) / f32(ROT_DIM))
    ang = np.arange(n_rows, dtype=f32)[:, None] * inv_freq[None, :]
    cos, sin = np.cos(ang).astype(f32), np.sin(ang).astype(f32)
    zeros = np.zeros((n_rows, A_DH - ROT_DIM), f32)
    zh = np.zeros((n_rows, half), f32)
    c = np.concatenate([cos, cos, zeros + f32(1.0)], axis=1)
    sa = np.concatenate([-sin, zh, zeros], axis=1)
    sb = np.concatenate([zh, sin, zeros], axis=1)
    return tuple(np.concatenate([tab, tab], axis=1) for tab in (c, sa, sb))


def kernel(x, meta, ffn1_pre_g, ffn1_post_g, ffn1_w_gate, ffn1_w_up, ffn1_w_down, mix_pre_g, mix_post_g, w_in, b_in, m_conv_w, m_conv_b, m_norm_g, m_w_branch, a_lambda_q1, a_lambda_k1, a_lambda_q2, a_lambda_k2, a_norm_g, a_w_branch, w_out, ffn2_pre_g, ffn2_post_g, ffn2_w_gate, ffn2_w_up, ffn2_w_down):
    batch, seq, d = x.shape
    assert batch == 1 and meta.shape == (N_META, d)
    assert (2 * d) % A_QK == 0 and COL_GM % d == 0
    l = 0
    xr = x.reshape(seq, d)
    row = lambda v: v[l].reshape(1, -1)
    bf = lambda w: w[l].astype(BF16)

    w_in_l, b_in_l = w_in[l], b_in[l]
    gate_end = GATE_OFFSET + N_GATES
    w_in_t = w_in_l.T
    b_main = jnp.concatenate([b_in_l[:GATE_OFFSET], b_in_l[gate_end:]]).reshape(1, -1)
    b_gate = jnp.pad(b_in_l[GATE_OFFSET:gate_end], (0, LANES - N_GATES)).reshape(1, -1)
    rope = _rope_tables(N_META + seq)
    rope_m = tuple(tab[:N_META] for tab in rope)
    rope_x = tuple(tab[N_META:] for tab in rope)

    hx, hmeta = _ffn(xr, row(ffn1_pre_g), row(ffn1_post_g), ffn1_w_gate, ffn1_w_up, ffn1_w_down,
                     extra=meta, layer=l, tm=FFN_TM, tf=FFN_TF)
    zx, gx, zm, gm = _inproj(hx, hmeta, row(mix_pre_g), w_in_t, b_main, b_gate, rope_x, rope_m,
                             tm=1024)

    ha = _attn(zx, zm, row(a_lambda_q1), row(a_lambda_k1), row(a_lambda_q2), row(a_lambda_k2),
               row(a_norm_g), tq=512)
    h2 = _mixout(zx, gx, zm, gm, ha, hx, m_conv_w[l], row(m_conv_b), row(m_norm_g),
                 bf(m_w_branch), bf(a_w_branch), bf(w_out), row(mix_post_g), t=256)
    out = _ffn(h2, row(ffn2_pre_g), row(ffn2_post_g), ffn2_w_gate, ffn2_w_up, ffn2_w_down,
               layer=l, tm=FFN_TM, tf=FFN_TF)
    return out.reshape(batch, seq, d)
```

```python
import functools

import jax
import jax.numpy as jnp
import numpy as np
from jax import lax
from jax.experimental import pallas as pl
from jax.experimental.pallas import tpu as pltpu

F32 = jnp.float32
BF16 = jnp.bfloat16

EPS = 1e-6
N_META = 16
M_HEADS = 4
M_DQK = 128
M_DV = 256
M_QK = M_HEADS * M_DQK
M_V = M_HEADS * M_DV
CONV_W = 4
A_HEADS = 8
A_DH = 64
A_DV = 2 * A_DH
A_QK = A_HEADS * 2 * A_DH
A_V = A_HEADS * A_DV
ROT_DIM = A_DH // 4
ROPE_THETA = 500000.0
LAMBDA_INIT = 0.8 - 0.6 * 1.0
Q_SCALE = A_DH ** -0.5 * 1.4426950408889634

LANES = 128
GATE_OFFSET = 2 * M_QK + 2 * M_V
N_GATES = 2 * M_HEADS
COL_QK = 0
COL_MV = 2 * M_QK
COL_MO = COL_MV + M_V
COL_AQ = COL_MO + M_V
COL_AK = COL_AQ + A_QK
COL_AV = COL_AK + A_QK
COL_GM = COL_AV + A_V

VMEM_LIMIT = 56 * 1024 * 1024
VMEM_PHYSICAL = 64 * 1024 * 1024


def _params(sem, vmem_limit=VMEM_LIMIT):
    assert vmem_limit < VMEM_PHYSICAL
    return pltpu.CompilerParams(dimension_semantics=sem, vmem_limit_bytes=vmem_limit)


def _rms(x):
    return x * lax.rsqrt(jnp.mean(x * x, axis=-1, keepdims=True) + EPS)


NORM_STRIP = 16


FFN_TM = 1024
FFN_TF = 256
FFN_VMEM_LIMIT = 62 * 1024 * 1024


def _ffn_kernel(h_ref, gpre_ref, gpost_ref, wg_ref, wu_ref, wd_ref, *rest, n_extra):
    if n_extra:
        x_ref, o_ref, ox_ref, u_scr, accx_scr = rest
    else:
        o_ref, u_scr = rest
    i = pl.program_id(0)
    f = pl.program_id(1)
    last = pl.num_programs(1) - 1
    tm = h_ref.shape[0]

    @pl.when(f == 0)
    def _():
        for r0 in range(0, tm, NORM_STRIP):
            rs = slice(r0, r0 + NORM_STRIP)
            u_scr[rs, :] = (_rms(h_ref[rs, :]) * gpre_ref[...]).astype(BF16)
        o_ref[...] = jnp.zeros_like(o_ref)

    def swiglu_down(u):
        g = jnp.dot(u, wg_ref[...].astype(BF16), preferred_element_type=F32)
        up = jnp.dot(u, wu_ref[...].astype(BF16), preferred_element_type=F32)
        a = (g * jax.nn.sigmoid(g) * up).astype(BF16)
        return jnp.dot(a, wd_ref[...].astype(BF16), preferred_element_type=F32)

    if n_extra:
        @pl.when(jnp.logical_and(i == 0, f == 0))
        def _():
            u_scr[tm:tm + n_extra, :] = (_rms(x_ref[...]) * gpre_ref[...]).astype(BF16)
            accx_scr[...] = jnp.zeros_like(accx_scr)

        @pl.when(i == 0)
        def _():
            down = swiglu_down(u_scr[...])
            o_ref[...] += down[0:tm]
            accx_scr[...] += down[tm:tm + n_extra]

        @pl.when(i > 0)
        def _():
            o_ref[...] += swiglu_down(u_scr[0:tm, :])

        @pl.when(jnp.logical_and(i == 0, f == last))
        def _():
            ox_ref[...] = x_ref[...] + 0.5 * (_rms(accx_scr[...]) * gpost_ref[...])
    else:
        o_ref[...] += swiglu_down(u_scr[...])

    @pl.when(f == last)
    def _():
        for r0 in range(0, tm, NORM_STRIP):
            rs = slice(r0, r0 + NORM_STRIP)
            o_ref[rs, :] = h_ref[rs, :] + 0.5 * (_rms(o_ref[rs, :]) * gpost_ref[...])


def _ffn(h, g_pre, g_post, w_gate, w_up, w_down, extra=None, *, layer, tm, tf):
    rows, d = h.shape
    d_ff = w_gate.shape[2]
    n_extra = 0 if extra is None else extra.shape[0]
    vec = pl.BlockSpec((1, d), lambda i, f: (0, 0))
    tile = pl.BlockSpec((tm, d), lambda i, f: (i, 0))
    in_specs = [tile, vec, vec,
                pl.BlockSpec((None, d, tf), lambda i, f: (layer, 0, f)),
                pl.BlockSpec((None, d, tf), lambda i, f: (layer, 0, f)),
                pl.BlockSpec((None, tf, d), lambda i, f: (layer, f, 0))]
    args = [h, g_pre, g_post, w_gate, w_up, w_down]
    out_shape = jax.ShapeDtypeStruct((rows, d), F32)
    out_specs = tile
    scratch = [pltpu.VMEM((tm + n_extra, d), BF16)]
    if n_extra:
        whole = pl.BlockSpec((n_extra, d), lambda i, f: (0, 0))
        in_specs.append(whole)
        args.append(extra)
        out_shape = (out_shape, jax.ShapeDtypeStruct((n_extra, d), F32))
        out_specs = (tile, whole)
        scratch.append(pltpu.VMEM((n_extra, d), F32))
    return pl.pallas_call(
        functools.partial(_ffn_kernel, n_extra=n_extra),
        out_shape=out_shape,
        grid=(rows // tm, d_ff // tf),
        in_specs=in_specs,
        out_specs=out_specs,
        scratch_shapes=scratch,
        compiler_params=_params(("arbitrary", "arbitrary"), vmem_limit=FFN_VMEM_LIMIT),
        name="ffn",
    )(*args)


def _rope(z, c, sa, sb):
    outs = []
    for grp in range(z.shape[1] // LANES):
        x = z[:, grp * LANES:(grp + 1) * LANES]
        outs.append(x * c + pltpu.roll(x, LANES - ROT_DIM // 2, 1) * sa
                    + pltpu.roll(x, ROT_DIM // 2, 1) * sb)
    return jnp.concatenate(outs, axis=1)


_NT = (((1,), (1,)), ((), ()))
ROPE_GROUP = 256


def _inproj_kernel(h_ref, g_ref, wt_ref, b_ref, wgate_ref, bgate_ref, rope_ref,
                   xh_ref, xrope_ref, z_ref, gate_ref, zx_ref, gatex_ref, u_scr, *, tn):
    i = pl.program_id(0)
    j = pl.program_id(1)
    tm = h_ref.shape[0]
    nx = xh_ref.shape[0]

    def gates(u):
        wg = wgate_ref[...].astype(BF16)
        wg = jnp.concatenate([wg, jnp.zeros((LANES - N_GATES, wg.shape[1]), BF16)], axis=0)
        return lax.dot_general(u, wg, _NT, preferred_element_type=F32) + bgate_ref[...]

    @pl.when(j == 0)
    def _():
        for r0 in range(0, tm, NORM_STRIP):
            rs = slice(r0, r0 + NORM_STRIP)
            u_scr[rs, :] = (_rms(h_ref[rs, :]) * g_ref[...]).astype(BF16)
        gate_ref[...] = gates(u_scr[0:tm, :])

    @pl.when(jnp.logical_and(i == 0, j == 0))
    def _():
        u = (_rms(xh_ref[...]) * g_ref[...]).astype(BF16)
        u_scr[tm:tm + nx, :] = u
        gatex_ref[...] = gates(u)

    def finish(z, kind, tab):
        if kind == "plain":
            return z.astype(BF16)
        z = _rope(z, tab[:, 0:LANES], tab[:, LANES:2 * LANES], tab[:, 2 * LANES:3 * LANES])
        if kind == "q":
            z = z * Q_SCALE
        return z.astype(BF16)

    def tile_out(kind, with_extra):
        w = wt_ref[...].astype(BF16)
        if kind != "plain" and tm % ROPE_GROUP == 0:
            for r0 in range(0, tm, ROPE_GROUP):
                rs = slice(r0, r0 + ROPE_GROUP)
                z = lax.dot_general(u_scr[rs, :], w, _NT, preferred_element_type=F32) + b_ref[...]
                z_ref[rs, :] = finish(z, kind, rope_ref.at[rs, :])
            if with_extra:
                z = (lax.dot_general(u_scr[tm:tm + nx, :], w, _NT, preferred_element_type=F32)
                     + b_ref[...])
                zx_ref[...] = finish(z, kind, xrope_ref)
            return
        u = u_scr[...] if with_extra else u_scr[0:tm, :]
        z = lax.dot_general(u, w, _NT, preferred_element_type=F32) + b_ref[...]
        z_ref[...] = finish(z[0:tm], kind, rope_ref)
        if with_extra:
            zx_ref[...] = finish(z[tm:tm + nx], kind, xrope_ref)

    is_q = j == COL_AQ // tn
    is_k = j == COL_AK // tn
    plain = jnp.logical_not(jnp.logical_or(is_q, is_k))
    for kind, cond in (("q", is_q), ("k", is_k), ("plain", plain)):
        pl.when(jnp.logical_and(cond, i == 0))(functools.partial(tile_out, kind, True))
        pl.when(jnp.logical_and(cond, i > 0))(functools.partial(tile_out, kind, False))


def _inproj(h, xh, g, w_t, b_main, b_gate, rope, rope_x, *, tm):
    rows, d = h.shape
    nx = xh.shape[0]
    tn = A_QK
    n_a = GATE_OFFSET // tn
    n = w_t.shape[0] - N_GATES
    nj = n // tn
    assert COL_AQ == GATE_OFFSET and GATE_OFFSET % tn == 0 and n % tn == 0

    def w_row(i, j):
        return (pl.multiple_of(jnp.where(j < n_a, j * tn, j * tn + N_GATES), N_GATES), 0)

    table = pl.BlockSpec((tm, 3 * LANES), lambda i, j: (i, 0))
    table_x = pl.BlockSpec((nx, 3 * LANES), lambda i, j: (0, 0))
    return pl.pallas_call(
        functools.partial(_inproj_kernel, tn=tn),
        out_shape=(jax.ShapeDtypeStruct((rows, n), BF16),
                   jax.ShapeDtypeStruct((rows, LANES), F32),
                   jax.ShapeDtypeStruct((nx, n), BF16),
                   jax.ShapeDtypeStruct((nx, LANES), F32)),
        grid=(rows // tm, nj),
        in_specs=[
            pl.BlockSpec((tm, d), lambda i, j: (i, 0)),
            pl.BlockSpec((1, d), lambda i, j: (0, 0)),
            pl.BlockSpec((pl.Element(tn), pl.Element(d)), w_row),
            pl.BlockSpec((1, tn), lambda i, j: (0, j)),
            pl.BlockSpec((N_GATES, d), lambda i, j: (GATE_OFFSET // N_GATES, 0)),
            pl.BlockSpec((1, LANES), lambda i, j: (0, 0)),
            table,
            pl.BlockSpec((nx, d), lambda i, j: (0, 0)),
            table_x,
        ],
        out_specs=(pl.BlockSpec((tm, tn), lambda i, j: (i, j)),
                   pl.BlockSpec((tm, LANES), lambda i, j: (i, 0)),
                   pl.BlockSpec((nx, tn), lambda i, j: (0, jnp.where(i == 0, j, nj - 1))),
                   pl.BlockSpec((nx, LANES), lambda i, j: (0, 0))),
        scratch_shapes=[pltpu.VMEM((tm + nx, d), BF16)],
        compiler_params=_params(("arbitrary", "arbitrary")),
        name="inproj",
    )(h, g, w_t, b_main, w_t, b_gate, rope, xh, rope_x)


CONV_PAD = 8


def _conv_silu(xbuf, n, cw_ref, cb_ref):
    y = cb_ref[...]
    for i in range(CONV_W):
        start = CONV_PAD + i - (CONV_W - 1)
        y = y + cw_ref[i:i + 1, :] * xbuf[start:start + n, :]
    return y * jax.nn.sigmoid(y)


def _log_sigmoid(x):
    return jnp.minimum(x, 0.0) - jnp.log1p(jnp.exp(-jnp.abs(x)))


def _cumsum_rows(x):
    n = x.shape[0]
    r = lax.broadcasted_iota(jnp.int32, (n, n), 0)
    c = lax.broadcasted_iota(jnp.int32, (n, n), 1)
    tril = (r >= c).astype(F32)
    return jnp.dot(tril, x, preferred_element_type=F32, precision=lax.Precision.HIGHEST)


def _state_update(c_scr, m_scr, hd, kf, vext, li_col, b_col, g_tot, m_prev):
    log_w = (g_tot - b_col) + li_col
    m_new = jnp.maximum(g_tot + m_prev, jnp.max(log_w, axis=0, keepdims=True))
    wk = jnp.exp(log_w - m_new)
    decay = jnp.exp(g_tot + m_prev - m_new)
    kw = (kf * wk).astype(BF16)
    upd = lax.dot_general(kw, vext, (((0,), (0,)), ((), ())), preferred_element_type=F32)
    c_scr[hd] = decay * c_scr[hd] + upd
    m_scr[hd] = jnp.broadcast_to(m_new, m_scr.shape[1:])


M_KSCALE = M_DQK ** -0.5


def _mlstm_meta_init(qkm_ref, mvm_ref, gtm_ref, cw_ref, cb_ref, c_scr, m_scr, xbuf):
    nm = qkm_ref.shape[0]
    c_scr[...] = jnp.zeros_like(c_scr)
    m_scr[...] = jnp.zeros_like(m_scr)
    xbuf[0:CONV_PAD, :] = jnp.zeros((CONV_PAD, xbuf.shape[1]), F32)
    xbuf[CONV_PAD:CONV_PAD + nm, :] = qkm_ref[...].astype(F32)
    qk = _conv_silu(xbuf, nm, cw_ref, cb_ref)
    xbuf[0:CONV_PAD, :] = xbuf[nm:nm + CONV_PAD, :]
    gts = gtm_ref[...]
    b_all = _cumsum_rows(_log_sigmoid(gts))
    ones = jnp.ones((nm, LANES), BF16)
    for hd in range(M_HEADS):
        kf = qk[:, M_QK + hd * M_DQK:M_QK + (hd + 1) * M_DQK] * M_KSCALE
        vext = jnp.concatenate([mvm_ref[:, hd * M_DV:(hd + 1) * M_DV], ones], axis=1)
        b_col = b_all[:, M_HEADS + hd:M_HEADS + hd + 1]
        _state_update(c_scr, m_scr, hd, kf, vext, gts[:, hd:hd + 1], b_col,
                      b_col[nm - 1:nm, :], m_scr[hd][0:1, 0:1])


def _mlstm_chunk(qk_ref, mv_ref, mo_ref, gt_ref, cw_ref, cb_ref, ng_ref, o_ref, c_scr, m_scr, xbuf):
    t = qk_ref.shape[0]
    kscale = M_KSCALE
    xbuf[CONV_PAD:CONV_PAD + t, :] = qk_ref[...].astype(F32)
    qk = _conv_silu(xbuf, t, cw_ref, cb_ref)
    xbuf[0:CONV_PAD, :] = xbuf[t:t + CONV_PAD, :]

    gts = gt_ref[...]
    b_all = _cumsum_rows(_log_sigmoid(gts))
    lane = lax.broadcasted_iota(jnp.int32, gts.shape, 1)
    rowform = jnp.where(lane < M_HEADS, gts, b_all).T
    r = lax.broadcasted_iota(jnp.int32, (t, t), 0)
    c = lax.broadcasted_iota(jnp.int32, (t, t), 1)
    causal = r >= c
    ones = jnp.ones((t, LANES), BF16)

    for hd in range(M_HEADS):
        yield
        qh = qk[:, hd * M_DQK:(hd + 1) * M_DQK].astype(BF16)
        kf = qk[:, M_QK + hd * M_DQK:M_QK + (hd + 1) * M_DQK] * kscale
        kh = kf.astype(BF16)
        vext = jnp.concatenate([mv_ref[:, hd * M_DV:(hd + 1) * M_DV], ones], axis=1)
        li_col = gts[:, hd:hd + 1]
        b_col = b_all[:, M_HEADS + hd:M_HEADS + hd + 1]
        li_row = rowform[hd:hd + 1, :]
        b_row = rowform[M_HEADS + hd:M_HEADS + hd + 1, :]
        g_tot = b_col[t - 1:t, :]
        m_prev = m_scr[hd][0:1, 0:1]

        log_d = jnp.where(causal, (b_col - b_row) + li_row, -jnp.inf)
        m_inter = b_col + m_prev
        m_row = jnp.maximum(m_inter, jnp.max(log_d, axis=1, keepdims=True))
        s = lax.dot_general(qh, kh, (((1,), (1,)), ((), ())), preferred_element_type=F32)
        sd = (s * jnp.exp(log_d - m_row)).astype(BF16)
        inter = jnp.dot(qh, c_scr[hd].astype(BF16), preferred_element_type=F32)
        numden = jnp.exp(m_inter - m_row) * inter + jnp.dot(sd, vext, preferred_element_type=F32)
        den = jnp.maximum(jnp.abs(numden[:, M_DV:M_DV + 1]), jnp.exp(-m_row))
        hh = numden[:, :M_DV] / den
        hn = _rms(hh) * ng_ref[:, hd * M_DV:(hd + 1) * M_DV]
        og = jax.nn.sigmoid(mo_ref[:, hd * M_DV:(hd + 1) * M_DV].astype(F32))
        o_ref[:, hd * M_DV:(hd + 1) * M_DV] = (og * hn).astype(BF16)

        _state_update(c_scr, m_scr, hd, kf, vext, li_col, b_col, g_tot, m_prev)


ATT_CHUNK = 64


ATT_STRIP = 64
ATT_GROUP = 256
ATT_HEADS_PER_STEP = 2


def _attn_kernel(q_ref, k_ref, v_ref, km_ref, vm_ref, lamv_ref,
                 g_ref, o_ref, q2_scr, s0_scr, s1_scr, p0_scr, p1_scr, m_scr, alpha_scr, acc_scr):
    i = pl.program_id(1)
    tq = q_ref.shape[0]
    tk = tq
    rows = 2 * tq
    ncol = tk // LANES
    nheads = q_ref.shape[1] // A_DV
    nt = (((1,), (1,)), ((), ()))
    ones_k = jnp.ones((tk, LANES), BF16)
    s_slots = (s0_scr, s1_scr)
    p_slots = (p0_scr, p1_scr)

    def cols_of(hh):
        return slice(hh * A_DV, (hh + 1) * A_DV)

    def kblock(hh, blk):
        off = pl.multiple_of(blk * tk, tk)
        return k_ref[pl.ds(off, tk), cols_of(hh)]

    def softmax_pv(hh, slot, blk, masked, nxt=None):
        s_ref, p_ref = s_slots[slot].at[hh], p_slots[slot].at[hh]
        m_ref, alpha_ref, acc_ref = m_scr.at[hh], alpha_scr.at[hh], acc_scr.at[hh]
        off = pl.multiple_of(blk * tk, tk)
        vext = jnp.concatenate([v_ref[pl.ds(off, tk), cols_of(hh)], ones_k], axis=1)
        for g0 in range(0, rows, ATT_GROUP):
            gs = slice(g0, g0 + ATT_GROUP)
            if nxt is not None:
                nxt_slot, nxt_blk = nxt
                s_slots[nxt_slot].at[hh][gs, :] = lax.dot_general(
                    q2_scr.at[hh][gs, :], kblock(hh, nxt_blk), nt, preferred_element_type=F32)
            softmax_rows(s_ref, p_ref, m_ref, alpha_ref, masked, g0)
            pv = jnp.dot(p_ref[gs, :], vext, preferred_element_type=F32)
            alpha = alpha_ref[gs, :]
            for c in range(2):
                sl = slice(c * LANES, (c + 1) * LANES)
                acc_ref[gs, sl] = alpha * acc_ref[gs, sl] + pv[:, sl]
            yield

    def softmax_rows(s_ref, p_ref, m_scr, alpha_scr, masked, g0):
        for st in range(g0 // ATT_STRIP, (g0 + ATT_GROUP) // ATT_STRIP):
            rs = slice(st * ATT_STRIP, (st + 1) * ATT_STRIP)
            if masked:
                qchunk = ((st * ATT_STRIP) % tq) // ATT_CHUNK
                nvis = (qchunk + 1) * ATT_CHUNK
                cvis = lax.broadcasted_iota(jnp.int32, (ATT_STRIP, LANES), 1)
                cols = []
                for c in range(-(-nvis // LANES)):
                    sc = s_ref[rs, c * LANES:(c + 1) * LANES]
                    if (c + 1) * LANES > nvis:
                        sc = jnp.where(cvis < nvis - c * LANES, sc, -jnp.inf)
                    cols.append(sc)
            else:
                cols = [s_ref[rs, c * LANES:(c + 1) * LANES] for c in range(ncol)]
            part = cols[0]
            for sc in cols[1:]:
                part = jnp.maximum(part, sc)
            m_old = m_scr[rs, :]
            m_new = jnp.maximum(m_old, jnp.max(part, axis=1, keepdims=True))
            m_scr[rs, :] = m_new
            alpha_scr[rs, :] = jnp.exp2(m_old - m_new)
            if not masked:
                cols = [s_ref[rs, c * LANES:(c + 1) * LANES] for c in range(ncol)]
            pcols = [jnp.exp2(sc - m_new) for sc in cols]
            pcols += [jnp.zeros_like(pcols[0])] * (ncol - len(cols))
            p_ref[rs, :] = jnp.concatenate(pcols, axis=1).astype(BF16)

    def prologue(hh):
        q = q_ref[:, cols_of(hh)]
        lane = lax.broadcasted_iota(jnp.int32, q.shape, 1)
        zero = jnp.zeros_like(q)
        q2 = q2_scr.at[hh]
        q2[0:tq, :] = jnp.where(lane < A_DH, q, zero)
        q2[tq:rows, :] = jnp.where(lane >= A_DH, q, zero)
        yield
        s = lax.dot_general(q2[...], km_ref[:, cols_of(hh)], nt, preferred_element_type=F32)
        m0 = jnp.max(s, axis=1, keepdims=True)
        p = jnp.exp2(s - m0).astype(BF16)
        vext = jnp.concatenate([vm_ref[:, cols_of(hh)],
                                jnp.ones((vm_ref.shape[0], LANES), BF16)], axis=1)
        acc_scr[hh] = jnp.dot(p, vext, preferred_element_type=F32)
        m_scr[hh] = jnp.broadcast_to(m0, m_scr.shape[1:])
        yield
        s0_scr[hh] = lax.dot_general(q2[...], kblock(hh, 0), nt, preferred_element_type=F32)

    def chain(*gens):
        for g in gens:
            yield from g

    def run(make):
        gens = [make(hh) for hh in range(nheads)]
        while gens:
            for g in list(gens):
                try:
                    next(g)
                except StopIteration:
                    gens.remove(g)

    run(prologue)

    def pair(pr):
        run(lambda hh: chain(softmax_pv(hh, 0, 2 * pr, False, nxt=(1, 2 * pr + 1)),
                             softmax_pv(hh, 1, 2 * pr + 1, False, nxt=(0, 2 * pr + 2))))

    npairs = i // 2

    def two_pairs(t, carry):
        pair(2 * t)
        pair(2 * t + 1)
        return carry

    lax.fori_loop(0, npairs // 2, two_pairs, 0)

    @pl.when(npairs % 2 == 1)
    def _():
        pair(npairs - 1)

    @pl.when(i % 2 == 0)
    def _():
        run(lambda hh: softmax_pv(hh, 0, i, True))

    @pl.when(i % 2 == 1)
    def _():
        run(lambda hh: chain(softmax_pv(hh, 0, i - 1, False, nxt=(1, i)),
                             softmax_pv(hh, 1, i, True)))

    lam = (jnp.exp(jnp.sum(lamv_ref[0:1, :] * lamv_ref[1:2, :], axis=1, keepdims=True))
           - jnp.exp(jnp.sum(lamv_ref[2:3, :] * lamv_ref[3:4, :], axis=1, keepdims=True))
           + LAMBDA_INIT)
    for hh in range(nheads):
        o = acc_scr[hh, :, 0:A_DV] / acc_scr[hh, :, A_DV:A_DV + 1]
        o = o[:tq] - lam * o[tq:]
        o_ref[:, cols_of(hh)] = (_rms(o) * g_ref[:, cols_of(hh)] * (1.0 - LAMBDA_INIT)).astype(BF16)


def _attn(zx, zm, lamv, norm_g, *, tq):
    rows = zx.shape[0]
    nm = zm.shape[0]
    nh = ATT_HEADS_PER_STEP
    w = nh * A_DV
    return pl.pallas_call(
        _attn_kernel,
        out_shape=jax.ShapeDtypeStruct((rows, A_V), BF16),
        grid=(A_HEADS // nh, rows // tq),
        in_specs=[
            pl.BlockSpec((tq, w), lambda h, i: (i, COL_AQ // w + h)),
            pl.BlockSpec((rows, w), lambda h, i: (0, COL_AK // w + h)),
            pl.BlockSpec((rows, w), lambda h, i: (0, COL_AV // w + h)),
            pl.BlockSpec((nm, w), lambda h, i: (0, COL_AK // w + h)),
            pl.BlockSpec((nm, w), lambda h, i: (0, COL_AV // w + h)),
            pl.BlockSpec(lamv.shape, lambda h, i: (0, 0)),
            pl.BlockSpec((1, w), lambda h, i: (0, h)),
        ],
        out_specs=pl.BlockSpec((tq, w), lambda h, i: (i, h)),
        scratch_shapes=[
            pltpu.VMEM((nh, 2 * tq, A_DV), BF16),
            pltpu.VMEM((nh, 2 * tq, tq), F32),
            pltpu.VMEM((nh, 2 * tq, tq), F32),
            pltpu.VMEM((nh, 2 * tq, tq), BF16),
            pltpu.VMEM((nh, 2 * tq, tq), BF16),
            pltpu.VMEM((nh, 2 * tq, LANES), F32),
            pltpu.VMEM((nh, 2 * tq, LANES), F32),
            pltpu.VMEM((nh, 2 * tq, 2 * LANES), F32),
        ],
        compiler_params=_params(("parallel", "arbitrary")),
        name="diffattn",
    )(zx, zx, zx, zm, zm, lamv, norm_g)


def _mixout_kernel(ha_ref, gmga_ref, h_ref, mw_ref, aw_ref, wo_ref, gpost_ref,
                   zml_ref, gt_ref, zmlm_ref, gtm_ref, mpar_ref,
                   o_ref, hm_scr, hm_prev_scr, c_scr, m_scr, xbuf):
    s = pl.program_id(0)
    last = pl.num_programs(0) - 1
    w_qk = 2 * M_QK
    qk_ref = zml_ref.at[:, COL_QK:COL_QK + w_qk]
    mv_ref = zml_ref.at[:, COL_MV:COL_MV + M_V]
    mo_ref = zml_ref.at[:, COL_MO:COL_MO + M_V]
    qkm_ref = zmlm_ref.at[:, COL_QK:COL_QK + w_qk]
    mvm_ref = zmlm_ref.at[:, COL_MV:COL_MV + M_V]
    cw_ref = mpar_ref.at[0:CONV_W, :]
    cb_ref = mpar_ref.at[CONV_W:CONV_W + 1, :]
    ng_ref = mpar_ref.at[CONV_W + 1:CONV_W + 2, :]

    def merge():
        hm_prev_scr[...] = hm_scr[...]
        br_m = jnp.dot(hm_prev_scr[...], mw_ref[...], preferred_element_type=F32)
        yield
        br_a = jnp.dot(ha_ref[...], aw_ref[...], preferred_element_type=F32)
        yield
        d = h_ref.shape[1]
        y = (jax.nn.sigmoid(gmga_ref[:, 0:d].astype(F32)) * br_m
             + jax.nn.sigmoid(gmga_ref[:, d:2 * d].astype(F32)) * br_a)
        yield
        out = jnp.dot(y.astype(BF16), wo_ref[...], preferred_element_type=F32)
        yield
        o_ref[...] = h_ref[...] + _rms(out) * gpost_ref[...]

    def mlstm():
        return _mlstm_chunk(qk_ref, mv_ref, mo_ref, gt_ref, cw_ref, cb_ref, ng_ref, hm_scr,
                            c_scr, m_scr, xbuf)

    def run(*gens):
        gens = list(gens)
        while gens:
            for g in list(gens):
                try:
                    next(g)
                except StopIteration:
                    gens.remove(g)

    @pl.when(s == 0)
    def _():
        _mlstm_meta_init(qkm_ref, mvm_ref, gtm_ref, cw_ref, cb_ref, c_scr, m_scr, xbuf)
        run(mlstm())

    @pl.when(jnp.logical_and(s > 0, s < last))
    def _():
        run(merge(), mlstm())

    @pl.when(s == last)
    def _():
        run(merge())


def _mixout(zx, gx, zm, gm, ha, h, conv_w, conv_b, m_norm_g, mw, aw, wo, g_post, *, t):
    rows, d = h.shape
    nm = zm.shape[0]
    n = rows // t
    w_qk = 2 * M_QK
    w_ml = w_qk + 2 * M_V
    assert (COL_QK, COL_MV, COL_MO) == (0, w_qk, w_qk + M_V) and M_V == w_qk
    pad = jnp.zeros((8 - CONV_W - 2, w_qk), F32)
    mpar = jnp.concatenate([conv_w, conv_b, m_norm_g, pad], axis=0)
    const = lambda s: (0, 0)
    prev = lambda col: (lambda s: (jnp.maximum(s - 1, 0), col))
    cur = lambda col: (lambda s: (jnp.minimum(s, n - 1), col))
    return pl.pallas_call(
        _mixout_kernel,
        out_shape=jax.ShapeDtypeStruct((rows, d), F32),
        grid=(n + 1,),
        in_specs=[
            pl.BlockSpec((t, A_V), prev(0)),
            pl.BlockSpec((pl.Element(t), pl.Element(2 * d)),
                         lambda s: (pl.multiple_of(jnp.maximum(s - 1, 0) * t, t), COL_GM)),
            pl.BlockSpec((t, d), prev(0)),
            pl.BlockSpec((M_V, d), const),
            pl.BlockSpec((A_V, d), const),
            pl.BlockSpec((d, d), const),
            pl.BlockSpec((1, d), const),
            pl.BlockSpec((t, w_ml), cur(0)),
            pl.BlockSpec((t, LANES), cur(0)),
            pl.BlockSpec((nm, w_ml), const),
            pl.BlockSpec((nm, LANES), const),
            pl.BlockSpec(mpar.shape, const),
        ],
        out_specs=pl.BlockSpec((t, d), prev(0)),
        scratch_shapes=[
            pltpu.VMEM((t, M_V), BF16),
            pltpu.VMEM((t, M_V), BF16),
            pltpu.VMEM((M_HEADS, M_DQK, M_DV + LANES), F32),
            pltpu.VMEM((M_HEADS, 8, LANES), F32),
            pltpu.VMEM((t + CONV_PAD, w_qk), F32),
        ],
        compiler_params=_params(("arbitrary",)),
        name="mixout",
    )(ha, zx, h, mw, aw, wo, g_post, zx, gx, zm, gm, mpar)


def _rope_tables(n_rows):
    half = ROT_DIM // 2
    f32 = np.float32
    inv_freq = np.power(f32(ROPE_THETA), -np.arange(0, ROT_DIM, 2, dtype=f32) / f32(ROT_DIM))
    ang = np.arange(n_rows, dtype=f32)[:, None] * inv_freq[None, :]
    cos, sin = np.cos(ang).astype(f32), np.sin(ang).astype(f32)
    zeros = np.zeros((n_rows, A_DH - ROT_DIM), f32)
    zh = np.zeros((n_rows, half), f32)
    c = np.concatenate([cos, cos, zeros + f32(1.0)], axis=1)
    sa = np.concatenate([-sin, zh, zeros], axis=1)
    sb = np.concatenate([zh, sin, zeros], axis=1)
    return np.concatenate([np.concatenate([tab, tab], axis=1) for tab in (c, sa, sb)], axis=1)


def kernel(x, meta, ffn1_pre_g, ffn1_post_g, ffn1_w_gate, ffn1_w_up, ffn1_w_down, mix_pre_g, mix_post_g, w_in, b_in, m_conv_w, m_conv_b, m_norm_g, m_w_branch, a_lambda_q1, a_lambda_k1, a_lambda_q2, a_lambda_k2, a_norm_g, a_w_branch, w_out, ffn2_pre_g, ffn2_post_g, ffn2_w_gate, ffn2_w_up, ffn2_w_down):
    batch, seq, d = x.shape
    assert batch == 1 and meta.shape == (N_META, d)
    assert (2 * d) % A_QK == 0 and COL_GM % d == 0
    l = 0
    xr = x.reshape(seq, d)
    row = lambda v: v[l].reshape(1, -1)
    bf = lambda w: w[l].astype(BF16)

    w_in_l, b_in_l = w_in[l], b_in[l]
    gate_end = GATE_OFFSET + N_GATES
    w_in_t = w_in_l.T
    b_main = jnp.concatenate([b_in_l[:GATE_OFFSET], b_in_l[gate_end:]]).reshape(1, -1)
    b_gate = jnp.pad(b_in_l[GATE_OFFSET:gate_end], (0, LANES - N_GATES)).reshape(1, -1)
    rope = _rope_tables(N_META + seq)
    rope_m, rope_x = rope[:N_META], rope[N_META:]

    hx, hmeta = _ffn(xr, row(ffn1_pre_g), row(ffn1_post_g), ffn1_w_gate, ffn1_w_up, ffn1_w_down,
                     extra=meta, layer=l, tm=FFN_TM, tf=FFN_TF)
    zx, gx, zm, gm = _inproj(hx, hmeta, row(mix_pre_g), w_in_t, b_main, b_gate, rope_x, rope_m,
                             tm=1024)

    lamv = jnp.concatenate([row(a_lambda_q1), row(a_lambda_k1), row(a_lambda_q2),
                            row(a_lambda_k2)], axis=0)
    ha = _attn(zx, zm, lamv, row(a_norm_g), tq=512)
    h2 = _mixout(zx, gx, zm, gm, ha, hx, m_conv_w[l], row(m_conv_b), row(m_norm_g),
                 bf(m_w_branch), bf(a_w_branch), bf(w_out), row(mix_post_g), t=256)
    out = _ffn(h2, row(ffn2_pre_g), row(ffn2_post_g), ffn2_w_gate, ffn2_w_up, ffn2_w_down,
               layer=l, tm=FFN_TM, tf=FFN_TF)
    return out.reshape(batch, seq, d)
```

```python
import functools

import jax
import jax.numpy as jnp
import numpy as np
from jax import lax
from jax.experimental import pallas as pl
from jax.experimental.pallas import tpu as pltpu

F32 = jnp.float32
BF16 = jnp.bfloat16

EPS = 1e-6
N_META = 16
M_HEADS = 4
M_DQK = 128
M_DV = 256
M_QK = M_HEADS * M_DQK
M_V = M_HEADS * M_DV
CONV_W = 4
A_HEADS = 8
A_DH = 64
A_DV = 2 * A_DH
A_QK = A_HEADS * 2 * A_DH
A_V = A_HEADS * A_DV
ROT_DIM = A_DH // 4
ROPE_THETA = 500000.0
LAMBDA_INIT = 0.8 - 0.6 * 1.0
Q_SCALE = A_DH ** -0.5 * 1.4426950408889634

LANES = 128
GATE_OFFSET = 2 * M_QK + 2 * M_V
N_GATES = 2 * M_HEADS
COL_QK = 0
COL_MV = 2 * M_QK
COL_MO = COL_MV + M_V
COL_AQ = COL_MO + M_V
COL_AK = COL_AQ + A_QK
COL_AV = COL_AK + A_QK
COL_GM = COL_AV + A_V

VMEM_LIMIT = 56 * 1024 * 1024
VMEM_PHYSICAL = 64 * 1024 * 1024


def _params(sem, vmem_limit=VMEM_LIMIT):
    assert vmem_limit < VMEM_PHYSICAL
    return pltpu.CompilerParams(dimension_semantics=sem, vmem_limit_bytes=vmem_limit)


def _rms(x):
    return x * lax.rsqrt(jnp.mean(x * x, axis=-1, keepdims=True) + EPS)


NORM_STRIP = 16


FFN_TM = 1024
FFN_TF = 256
FFN_VMEM_LIMIT = 62 * 1024 * 1024


def _ffn_kernel(h_ref, gpre_ref, gpost_ref, wg_ref, wu_ref, wd_ref, *rest, n_extra):
    if n_extra:
        x_ref, o_ref, ox_ref, u_scr, accx_scr = rest
    else:
        o_ref, u_scr = rest
    i = pl.program_id(0)
    f = pl.program_id(1)
    last = pl.num_programs(1) - 1
    tm = h_ref.shape[0]

    @pl.when(f == 0)
    def _():
        for r0 in range(0, tm, NORM_STRIP):
            rs = slice(r0, r0 + NORM_STRIP)
            u_scr[rs, :] = (_rms(h_ref[rs, :]) * gpre_ref[...]).astype(BF16)
        o_ref[...] = jnp.zeros_like(o_ref)

    def swiglu_down(u):
        g = jnp.dot(u, wg_ref[...].astype(BF16), preferred_element_type=F32)
        up = jnp.dot(u, wu_ref[...].astype(BF16), preferred_element_type=F32)
        a = (g * jax.nn.sigmoid(g) * up).astype(BF16)
        return jnp.dot(a, wd_ref[...].astype(BF16), preferred_element_type=F32)

    if n_extra:
        @pl.when(jnp.logical_and(i == 0, f == 0))
        def _():
            u_scr[tm:tm + n_extra, :] = (_rms(x_ref[...]) * gpre_ref[...]).astype(BF16)
            accx_scr[...] = jnp.zeros_like(accx_scr)

        @pl.when(i == 0)
        def _():
            down = swiglu_down(u_scr[...])
            o_ref[...] += down[0:tm]
            accx_scr[...] += down[tm:tm + n_extra]

        @pl.when(i > 0)
        def _():
            o_ref[...] += swiglu_down(u_scr[0:tm, :])

        @pl.when(jnp.logical_and(i == 0, f == last))
        def _():
            ox_ref[...] = x_ref[...] + 0.5 * (_rms(accx_scr[...]) * gpost_ref[...])
    else:
        o_ref[...] += swiglu_down(u_scr[...])

    @pl.when(f == last)
    def _():
        for r0 in range(0, tm, NORM_STRIP):
            rs = slice(r0, r0 + NORM_STRIP)
            o_ref[rs, :] = h_ref[rs, :] + 0.5 * (_rms(o_ref[rs, :]) * gpost_ref[...])


def _ffn(h, g_pre, g_post, w_gate, w_up, w_down, extra=None, *, layer, tm, tf):
    rows, d = h.shape
    d_ff = w_gate.shape[2]
    n_extra = 0 if extra is None else extra.shape[0]
    vec = pl.BlockSpec((1, d), lambda i, f: (0, 0))
    tile = pl.BlockSpec((tm, d), lambda i, f: (i, 0))
    in_specs = [tile, vec, vec,
                pl.BlockSpec((None, d, tf), lambda i, f: (layer, 0, f)),
                pl.BlockSpec((None, d, tf), lambda i, f: (layer, 0, f)),
                pl.BlockSpec((None, tf, d), lambda i, f: (layer, f, 0))]
    args = [h, g_pre, g_post, w_gate, w_up, w_down]
    out_shape = jax.ShapeDtypeStruct((rows, d), F32)
    out_specs = tile
    scratch = [pltpu.VMEM((tm + n_extra, d), BF16)]
    if n_extra:
        whole = pl.BlockSpec((n_extra, d), lambda i, f: (0, 0))
        in_specs.append(whole)
        args.append(extra)
        out_shape = (out_shape, jax.ShapeDtypeStruct((n_extra, d), F32))
        out_specs = (tile, whole)
        scratch.append(pltpu.VMEM((n_extra, d), F32))
    return pl.pallas_call(
        functools.partial(_ffn_kernel, n_extra=n_extra),
        out_shape=out_shape,
        grid=(rows // tm, d_ff // tf),
        in_specs=in_specs,
        out_specs=out_specs,
        scratch_shapes=scratch,
        compiler_params=_params(("arbitrary", "arbitrary"), vmem_limit=FFN_VMEM_LIMIT),
        name="ffn",
    )(*args)


def _rope(z, c, sa, sb):
    outs = []
    for grp in range(z.shape[1] // LANES):
        x = z[:, grp * LANES:(grp + 1) * LANES]
        outs.append(x * c + pltpu.roll(x, LANES - ROT_DIM // 2, 1) * sa
                    + pltpu.roll(x, ROT_DIM // 2, 1) * sb)
    return jnp.concatenate(outs, axis=1)


_NT = (((1,), (1,)), ((), ()))
ROPE_GROUP = 256


def _inproj_kernel(h_ref, g_ref, wt_ref, b_ref, wgate_ref, bgate_ref, rope_ref,
                   xh_ref, xrope_ref, z_ref, gate_ref, zx_ref, gatex_ref, u_scr, wt_scr, *, tn):
    i = pl.program_id(0)
    j = pl.program_id(1)
    tm = h_ref.shape[0]
    nx = xh_ref.shape[0]

    def gates(u):
        wg = wgate_ref[...].astype(BF16)
        wg = jnp.concatenate([wg, jnp.zeros((LANES - N_GATES, wg.shape[1]), BF16)], axis=0)
        return lax.dot_general(u, wg, _NT, preferred_element_type=F32) + bgate_ref[...]

    @pl.when(j == 0)
    def _():
        for r0 in range(0, tm, NORM_STRIP):
            rs = slice(r0, r0 + NORM_STRIP)
            u_scr[rs, :] = (_rms(h_ref[rs, :]) * g_ref[...]).astype(BF16)
        gate_ref[...] = gates(u_scr[0:tm, :])

    @pl.when(jnp.logical_and(i == 0, j == 0))
    def _():
        u = (_rms(xh_ref[...]) * g_ref[...]).astype(BF16)
        u_scr[tm:tm + nx, :] = u
        gatex_ref[...] = gates(u)

    def finish(z, kind, tab):
        if kind == "plain":
            return z.astype(BF16)
        z = _rope(z, tab[:, 0:LANES], tab[:, LANES:2 * LANES], tab[:, 2 * LANES:3 * LANES])
        if kind == "q":
            z = z * Q_SCALE
        return z.astype(BF16)

    def tile_out(kind, with_extra):
        w = wt_ref[...].astype(BF16)
        if kind != "plain" and tm % ROPE_GROUP == 0:
            for r0 in range(0, tm, ROPE_GROUP):
                rs = slice(r0, r0 + ROPE_GROUP)
                z = lax.dot_general(u_scr[rs, :], w, _NT, preferred_element_type=F32) + b_ref[...]
                z_ref[rs, :] = finish(z, kind, rope_ref.at[rs, :])
            if with_extra:
                z = (lax.dot_general(u_scr[tm:tm + nx, :], w, _NT, preferred_element_type=F32)
                     + b_ref[...])
                zx_ref[...] = finish(z, kind, xrope_ref)
            return
        u = u_scr[...] if with_extra else u_scr[0:tm, :]
        if kind == "plain":
            wt_scr[...] = w.T
            z = jnp.dot(u, wt_scr[...], preferred_element_type=F32) + b_ref[...]
        else:
            z = lax.dot_general(u, w, _NT, preferred_element_type=F32) + b_ref[...]
        z_ref[...] = finish(z[0:tm], kind, rope_ref)
        if with_extra:
            zx_ref[...] = finish(z[tm:tm + nx], kind, xrope_ref)

    is_q = j == COL_AQ // tn
    is_k = j == COL_AK // tn
    plain = jnp.logical_not(jnp.logical_or(is_q, is_k))
    used = jnp.logical_or(j < COL_MO // tn,
                          jnp.logical_and(j >= COL_AK // tn, j < COL_GM // tn))
    extra = jnp.logical_and(i == 0, used)

    def tile_no_extra(kind):
        tile_out(kind, False)

        @pl.when(i == 0)
        def _():
            zx_ref[...] = jnp.zeros_like(zx_ref)

    pl.when(is_q)(functools.partial(tile_no_extra, "q"))
    for kind, cond in (("k", is_k), ("plain", plain)):
        pl.when(jnp.logical_and(cond, extra))(functools.partial(tile_out, kind, True))
        pl.when(jnp.logical_and(cond, jnp.logical_not(extra)))(
            functools.partial(tile_no_extra, kind))


def _inproj(h, xh, g, w_t, b_main, b_gate, rope, rope_x, *, tm):
    rows, d = h.shape
    nx = xh.shape[0]
    tn = A_QK
    n_a = GATE_OFFSET // tn
    n = w_t.shape[0] - N_GATES
    nj = n // tn
    assert COL_AQ == GATE_OFFSET and GATE_OFFSET % tn == 0 and n % tn == 0

    def w_row(i, j):
        return (pl.multiple_of(jnp.where(j < n_a, j * tn, j * tn + N_GATES), N_GATES), 0)

    table = pl.BlockSpec((tm, 3 * LANES), lambda i, j: (i, 0))
    table_x = pl.BlockSpec((nx, 3 * LANES), lambda i, j: (0, 0))
    return pl.pallas_call(
        functools.partial(_inproj_kernel, tn=tn),
        out_shape=(jax.ShapeDtypeStruct((rows, n), BF16),
                   jax.ShapeDtypeStruct((rows, LANES), F32),
                   jax.ShapeDtypeStruct((nx, n), BF16),
                   jax.ShapeDtypeStruct((nx, LANES), F32)),
        grid=(rows // tm, nj),
        in_specs=[
            pl.BlockSpec((tm, d), lambda i, j: (i, 0)),
            pl.BlockSpec((1, d), lambda i, j: (0, 0)),
            pl.BlockSpec((pl.Element(tn), pl.Element(d)), w_row),
            pl.BlockSpec((1, tn), lambda i, j: (0, j)),
            pl.BlockSpec((N_GATES, d), lambda i, j: (GATE_OFFSET // N_GATES, 0)),
            pl.BlockSpec((1, LANES), lambda i, j: (0, 0)),
            table,
            pl.BlockSpec((nx, d), lambda i, j: (0, 0)),
            table_x,
        ],
        out_specs=(pl.BlockSpec((tm, tn), lambda i, j: (i, j)),
                   pl.BlockSpec((tm, LANES), lambda i, j: (i, 0)),
                   pl.BlockSpec((nx, tn), lambda i, j: (0, jnp.where(i == 0, j, nj - 1))),
                   pl.BlockSpec((nx, LANES), lambda i, j: (0, 0))),
        scratch_shapes=[pltpu.VMEM((tm + nx, d), BF16), pltpu.VMEM((d, tn), BF16)],
        compiler_params=_params(("arbitrary", "arbitrary"), vmem_limit=60 * 1024 * 1024),
        name="inproj",
    )(h, g, w_t, b_main, w_t, b_gate, rope, xh, rope_x)


CONV_PAD = 8


def _conv_silu(xbuf, n, cw_ref, cb_ref):
    y = cb_ref[...]
    for i in range(CONV_W):
        start = CONV_PAD + i - (CONV_W - 1)
        y = y + cw_ref[i:i + 1, :] * xbuf[start:start + n, :]
    return y * jax.nn.sigmoid(y)


def _log_sigmoid(x):
    return jnp.minimum(x, 0.0) - jnp.log1p(jnp.exp(-jnp.abs(x)))


def _cumsum_rows(x):
    n = x.shape[0]
    r = lax.broadcasted_iota(jnp.int32, (n, n), 0)
    c = lax.broadcasted_iota(jnp.int32, (n, n), 1)
    tril = (r >= c).astype(F32)
    return jnp.dot(tril, x, preferred_element_type=F32, precision=lax.Precision.HIGHEST)


def _state_update(c_scr, m_scr, hd, kf, vext, li_col, b_col, g_tot, m_prev):
    log_w = (g_tot - b_col) + li_col
    m_new = jnp.maximum(g_tot + m_prev, jnp.max(log_w, axis=0, keepdims=True))
    wk = jnp.exp(log_w - m_new)
    decay = jnp.exp(g_tot + m_prev - m_new)
    kw = (kf * wk).astype(BF16)
    upd = lax.dot_general(kw, vext, (((0,), (0,)), ((), ())), preferred_element_type=F32)
    c_scr[hd] = decay * c_scr[hd] + upd
    m_scr[hd] = jnp.broadcast_to(m_new, m_scr.shape[1:])


M_KSCALE = M_DQK ** -0.5


def _mlstm_meta_init(qkm_ref, mvm_ref, gtm_ref, cw_ref, cb_ref, c_scr, m_scr, xbuf):
    nm = qkm_ref.shape[0]
    c_scr[...] = jnp.zeros_like(c_scr)
    m_scr[...] = jnp.zeros_like(m_scr)
    xbuf[0:CONV_PAD, :] = jnp.zeros((CONV_PAD, xbuf.shape[1]), F32)
    xbuf[CONV_PAD:CONV_PAD + nm, :] = qkm_ref[...].astype(F32)
    qk = _conv_silu(xbuf, nm, cw_ref, cb_ref)
    xbuf[0:CONV_PAD, :] = xbuf[nm:nm + CONV_PAD, :]
    gts = gtm_ref[...]
    b_all = _cumsum_rows(_log_sigmoid(gts))
    ones = jnp.ones((nm, LANES), BF16)
    for hd in range(M_HEADS):
        kf = qk[:, M_QK + hd * M_DQK:M_QK + (hd + 1) * M_DQK] * M_KSCALE
        vext = jnp.concatenate([mvm_ref[:, hd * M_DV:(hd + 1) * M_DV], ones], axis=1)
        b_col = b_all[:, M_HEADS + hd:M_HEADS + hd + 1]
        _state_update(c_scr, m_scr, hd, kf, vext, gts[:, hd:hd + 1], b_col,
                      b_col[nm - 1:nm, :], m_scr[hd][0:1, 0:1])


def _mlstm_chunk(qk_ref, mv_ref, mo_ref, gt_ref, cw_ref, cb_ref, ng_ref, o_ref, c_scr, m_scr, xbuf):
    t = qk_ref.shape[0]
    kscale = M_KSCALE
    xbuf[CONV_PAD:CONV_PAD + t, :] = qk_ref[...].astype(F32)
    qk = _conv_silu(xbuf, t, cw_ref, cb_ref)
    xbuf[0:CONV_PAD, :] = xbuf[t:t + CONV_PAD, :]

    gts = gt_ref[...]
    b_all = _cumsum_rows(_log_sigmoid(gts))
    lane = lax.broadcasted_iota(jnp.int32, gts.shape, 1)
    rowform = jnp.where(lane < M_HEADS, gts, b_all).T
    r = lax.broadcasted_iota(jnp.int32, (t, t), 0)
    c = lax.broadcasted_iota(jnp.int32, (t, t), 1)
    causal = r >= c
    ones = jnp.ones((t, LANES), BF16)

    for hd in range(M_HEADS):
        yield
        qh = qk[:, hd * M_DQK:(hd + 1) * M_DQK].astype(BF16)
        kf = qk[:, M_QK + hd * M_DQK:M_QK + (hd + 1) * M_DQK] * kscale
        kh = kf.astype(BF16)
        vext = jnp.concatenate([mv_ref[:, hd * M_DV:(hd + 1) * M_DV], ones], axis=1)
        li_col = gts[:, hd:hd + 1]
        b_col = b_all[:, M_HEADS + hd:M_HEADS + hd + 1]
        li_row = rowform[hd:hd + 1, :]
        b_row = rowform[M_HEADS + hd:M_HEADS + hd + 1, :]
        g_tot = b_col[t - 1:t, :]
        m_prev = m_scr[hd][0:1, 0:1]

        log_d = jnp.where(causal, (b_col - b_row) + li_row, -jnp.inf)
        m_inter = b_col + m_prev
        m_row = jnp.maximum(m_inter, jnp.max(log_d, axis=1, keepdims=True))
        s = lax.dot_general(qh, kh, (((1,), (1,)), ((), ())), preferred_element_type=F32)
        sd = (s * jnp.exp(log_d - m_row)).astype(BF16)
        inter = jnp.dot(qh, c_scr[hd].astype(BF16), preferred_element_type=F32)
        numden = jnp.exp(m_inter - m_row) * inter + jnp.dot(sd, vext, preferred_element_type=F32)
        den = jnp.maximum(jnp.abs(numden[:, M_DV:M_DV + 1]), jnp.exp(-m_row))
        hh = numden[:, :M_DV] / den
        hn = _rms(hh) * ng_ref[:, hd * M_DV:(hd + 1) * M_DV]
        og = jax.nn.sigmoid(mo_ref[:, hd * M_DV:(hd + 1) * M_DV].astype(F32))
        o_ref[:, hd * M_DV:(hd + 1) * M_DV] = (og * hn).astype(BF16)

        _state_update(c_scr, m_scr, hd, kf, vext, li_col, b_col, g_tot, m_prev)


ATT_CHUNK = 64


ATT_STRIP = 64
ATT_GROUP = 256
ATT_HEADS_PER_STEP = 2


def _attn_kernel(q_ref, k_ref, v_ref, km_ref, vm_ref, lamv_ref,
                 g_ref, o_ref, q2_scr, s0_scr, s1_scr, p0_scr, p1_scr, m_scr, alpha_scr, acc_scr):
    i = pl.program_id(1)
    tq = q_ref.shape[0]
    tk = tq
    rows = 2 * tq
    ncol = tk // LANES
    nheads = q_ref.shape[1] // A_DV
    nt = (((1,), (1,)), ((), ()))
    ones_k = jnp.ones((tk, LANES), BF16)
    s_slots = (s0_scr, s1_scr)
    p_slots = (p0_scr, p1_scr)

    def cols_of(hh):
        return slice(hh * A_DV, (hh + 1) * A_DV)

    def kblock(hh, blk):
        off = pl.multiple_of(blk * tk, tk)
        return k_ref[pl.ds(off, tk), cols_of(hh)]

    def softmax_pv(hh, slot, blk, masked, nxt=None):
        s_ref, p_ref = s_slots[slot].at[hh], p_slots[slot].at[hh]
        m_ref, alpha_ref, acc_ref = m_scr.at[hh], alpha_scr.at[hh], acc_scr.at[hh]
        off = pl.multiple_of(blk * tk, tk)
        vext = jnp.concatenate([v_ref[pl.ds(off, tk), cols_of(hh)], ones_k], axis=1)
        for g0 in range(0, rows, ATT_GROUP):
            gs = slice(g0, g0 + ATT_GROUP)
            if nxt is not None:
                nxt_slot, nxt_blk = nxt
                s_slots[nxt_slot].at[hh][gs, :] = lax.dot_general(
                    q2_scr.at[hh][gs, :], kblock(hh, nxt_blk), nt, preferred_element_type=F32)
            softmax_rows(s_ref, p_ref, m_ref, alpha_ref, masked, g0)
            pv = jnp.dot(p_ref[gs, :], vext, preferred_element_type=F32)
            alpha = alpha_ref[gs, :]
            for c in range(2):
                sl = slice(c * LANES, (c + 1) * LANES)
                acc_ref[gs, sl] = alpha * acc_ref[gs, sl] + pv[:, sl]
            yield

    def softmax_rows(s_ref, p_ref, m_scr, alpha_scr, masked, g0):
        for st in range(g0 // ATT_STRIP, (g0 + ATT_GROUP) // ATT_STRIP):
            rs = slice(st * ATT_STRIP, (st + 1) * ATT_STRIP)
            if masked:
                qchunk = ((st * ATT_STRIP) % tq) // ATT_CHUNK
                nvis = (qchunk + 1) * ATT_CHUNK
                cvis = lax.broadcasted_iota(jnp.int32, (ATT_STRIP, LANES), 1)
                cols = []
                for c in range(-(-nvis // LANES)):
                    sc = s_ref[rs, c * LANES:(c + 1) * LANES]
                    if (c + 1) * LANES > nvis:
                        sc = jnp.where(cvis < nvis - c * LANES, sc, -jnp.inf)
                    cols.append(sc)
            else:
                cols = [s_ref[rs, c * LANES:(c + 1) * LANES] for c in range(ncol)]
            part = cols[0]
            for sc in cols[1:]:
                part = jnp.maximum(part, sc)
            m_old = m_scr[rs, :]
            m_new = jnp.maximum(m_old, jnp.max(part, axis=1, keepdims=True))
            m_scr[rs, :] = m_new
            alpha_scr[rs, :] = jnp.exp2(m_old - m_new)
            if not masked:
                cols = [s_ref[rs, c * LANES:(c + 1) * LANES] for c in range(ncol)]
            pcols = [jnp.exp2(sc - m_new) for sc in cols]
            pcols += [jnp.zeros_like(pcols[0])] * (ncol - len(cols))
            p_ref[rs, :] = jnp.concatenate(pcols, axis=1).astype(BF16)

    def prologue(hh):
        q = q_ref[:, cols_of(hh)]
        lane = lax.broadcasted_iota(jnp.int32, q.shape, 1)
        zero = jnp.zeros_like(q)
        q2 = q2_scr.at[hh]
        q2[0:tq, :] = jnp.where(lane < A_DH, q, zero)
        q2[tq:rows, :] = jnp.where(lane >= A_DH, q, zero)
        yield
        s = lax.dot_general(q2[...], km_ref[:, cols_of(hh)], nt, preferred_element_type=F32)
        m0 = jnp.max(s, axis=1, keepdims=True)
        p = jnp.exp2(s - m0).astype(BF16)
        vext = jnp.concatenate([vm_ref[:, cols_of(hh)],
                                jnp.ones((vm_ref.shape[0], LANES), BF16)], axis=1)
        acc_scr[hh] = jnp.dot(p, vext, preferred_element_type=F32)
        m_scr[hh] = jnp.broadcast_to(m0, m_scr.shape[1:])
        yield
        s0_scr[hh] = lax.dot_general(q2[...], kblock(hh, 0), nt, preferred_element_type=F32)

    def chain(*gens):
        for g in gens:
            yield from g

    def run(make):
        gens = [make(hh) for hh in range(nheads)]
        while gens:
            for g in list(gens):
                try:
                    next(g)
                except StopIteration:
                    gens.remove(g)

    run(prologue)

    def pair(pr):
        run(lambda hh: chain(softmax_pv(hh, 0, 2 * pr, False, nxt=(1, 2 * pr + 1)),
                             softmax_pv(hh, 1, 2 * pr + 1, False, nxt=(0, 2 * pr + 2))))

    npairs = i // 2

    def two_pairs(t, carry):
        pair(2 * t)
        pair(2 * t + 1)
        return carry

    lax.fori_loop(0, npairs // 2, two_pairs, 0)

    @pl.when(npairs % 2 == 1)
    def _():
        pair(npairs - 1)

    def finalize(hh):
        lam = (jnp.exp(jnp.sum(lamv_ref[0:1, :] * lamv_ref[1:2, :], axis=1, keepdims=True))
               - jnp.exp(jnp.sum(lamv_ref[2:3, :] * lamv_ref[3:4, :], axis=1, keepdims=True))
               + LAMBDA_INIT)
        o = acc_scr[hh, :, 0:A_DV] / acc_scr[hh, :, A_DV:A_DV + 1]
        o = o[:tq] - lam * o[tq:]
        o_ref[:, cols_of(hh)] = (_rms(o) * g_ref[:, cols_of(hh)] * (1.0 - LAMBDA_INIT)).astype(BF16)
        yield

    def lag(n):
        for _ in range(n):
            yield

    @pl.when(i % 2 == 0)
    def _():
        run(lambda hh: chain(lag(hh), softmax_pv(hh, 0, i, True), finalize(hh)))

    @pl.when(i % 2 == 1)
    def _():
        run(lambda hh: chain(lag(hh), softmax_pv(hh, 0, i - 1, False, nxt=(1, i)),
                             softmax_pv(hh, 1, i, True), finalize(hh)))


def _attn(zx, zm, lamv, norm_g, *, tq):
    rows = zx.shape[0]
    nm = zm.shape[0]
    nh = ATT_HEADS_PER_STEP
    w = nh * A_DV
    return pl.pallas_call(
        _attn_kernel,
        out_shape=jax.ShapeDtypeStruct((rows, A_V), BF16),
        grid=(A_HEADS // nh, rows // tq),
        in_specs=[
            pl.BlockSpec((tq, w), lambda h, i: (i, COL_AQ // w + h)),
            pl.BlockSpec((rows, w), lambda h, i: (0, COL_AK // w + h)),
            pl.BlockSpec((rows, w), lambda h, i: (0, COL_AV // w + h)),
            pl.BlockSpec((nm, w), lambda h, i: (0, COL_AK // w + h)),
            pl.BlockSpec((nm, w), lambda h, i: (0, COL_AV // w + h)),
            pl.BlockSpec(lamv.shape, lambda h, i: (0, 0)),
            pl.BlockSpec((1, w), lambda h, i: (0, h)),
        ],
        out_specs=pl.BlockSpec((tq, w), lambda h, i: (i, h)),
        scratch_shapes=[
            pltpu.VMEM((nh, 2 * tq, A_DV), BF16),
            pltpu.VMEM((nh, 2 * tq, tq), F32),
            pltpu.VMEM((nh, 2 * tq, tq), F32),
            pltpu.VMEM((nh, 2 * tq, tq), BF16),
            pltpu.VMEM((nh, 2 * tq, tq), BF16),
            pltpu.VMEM((nh, 2 * tq, LANES), F32),
            pltpu.VMEM((nh, 2 * tq, LANES), F32),
            pltpu.VMEM((nh, 2 * tq, 2 * LANES), F32),
        ],
        compiler_params=_params(("parallel", "arbitrary")),
        name="diffattn",
    )(zx, zx, zx, zm, zm, lamv, norm_g)


def _mixout_kernel(ha_ref, gmga_ref, h_ref, mw_ref, aw_ref, wo_ref, gpost_ref,
                   zml_ref, gt_ref, zmlm_ref, gtm_ref, mpar_ref,
                   o_ref, hm_scr, hm_prev_scr, c_scr, m_scr, xbuf):
    s = pl.program_id(0)
    last = pl.num_programs(0) - 1
    w_qk = 2 * M_QK
    qk_ref = zml_ref.at[:, COL_QK:COL_QK + w_qk]
    mv_ref = zml_ref.at[:, COL_MV:COL_MV + M_V]
    mo_ref = zml_ref.at[:, COL_MO:COL_MO + M_V]
    qkm_ref = zmlm_ref.at[:, COL_QK:COL_QK + w_qk]
    mvm_ref = zmlm_ref.at[:, COL_MV:COL_MV + M_V]
    cw_ref = mpar_ref.at[0:CONV_W, :]
    cb_ref = mpar_ref.at[CONV_W:CONV_W + 1, :]
    ng_ref = mpar_ref.at[CONV_W + 1:CONV_W + 2, :]

    def merge():
        hm_prev_scr[...] = hm_scr[...]
        br_m = jnp.dot(hm_prev_scr[...], mw_ref[...], preferred_element_type=F32)
        yield
        br_a = jnp.dot(ha_ref[...], aw_ref[...], preferred_element_type=F32)
        yield
        d = h_ref.shape[1]
        y = (jax.nn.sigmoid(gmga_ref[:, 0:d].astype(F32)) * br_m
             + jax.nn.sigmoid(gmga_ref[:, d:2 * d].astype(F32)) * br_a)
        yield
        out = jnp.dot(y.astype(BF16), wo_ref[...], preferred_element_type=F32)
        yield
        o_ref[...] = h_ref[...] + _rms(out) * gpost_ref[...]

    def mlstm():
        return _mlstm_chunk(qk_ref, mv_ref, mo_ref, gt_ref, cw_ref, cb_ref, ng_ref, hm_scr,
                            c_scr, m_scr, xbuf)

    def run(*gens):
        gens = list(gens)
        while gens:
            for g in list(gens):
                try:
                    next(g)
                except StopIteration:
                    gens.remove(g)

    @pl.when(s == 0)
    def _():
        _mlstm_meta_init(qkm_ref, mvm_ref, gtm_ref, cw_ref, cb_ref, c_scr, m_scr, xbuf)
        run(mlstm())

    @pl.when(jnp.logical_and(s > 0, s < last))
    def _():
        run(merge(), mlstm())

    @pl.when(s == last)
    def _():
        run(merge())


def _mixout(zx, gx, zm, gm, ha, h, conv_w, conv_b, m_norm_g, mw, aw, wo, g_post, *, t):
    rows, d = h.shape
    nm = zm.shape[0]
    n = rows // t
    w_qk = 2 * M_QK
    w_ml = w_qk + 2 * M_V
    assert (COL_QK, COL_MV, COL_MO) == (0, w_qk, w_qk + M_V) and M_V == w_qk
    pad = jnp.zeros((8 - CONV_W - 2, w_qk), F32)
    mpar = jnp.concatenate([conv_w, conv_b, m_norm_g, pad], axis=0)
    const = lambda s: (0, 0)
    prev = lambda col: (lambda s: (jnp.maximum(s - 1, 0), col))
    cur = lambda col: (lambda s: (jnp.minimum(s, n - 1), col))
    return pl.pallas_call(
        _mixout_kernel,
        out_shape=jax.ShapeDtypeStruct((rows, d), F32),
        grid=(n + 1,),
        in_specs=[
            pl.BlockSpec((t, A_V), prev(0)),
            pl.BlockSpec((pl.Element(t), pl.Element(2 * d)),
                         lambda s: (pl.multiple_of(jnp.maximum(s - 1, 0) * t, t), COL_GM)),
            pl.BlockSpec((t, d), prev(0)),
            pl.BlockSpec((M_V, d), const),
            pl.BlockSpec((A_V, d), const),
            pl.BlockSpec((d, d), const),
            pl.BlockSpec((1, d), const),
            pl.BlockSpec((t, w_ml), cur(0)),
            pl.BlockSpec((t, LANES), cur(0)),
            pl.BlockSpec((nm, w_ml), const),
            pl.BlockSpec((nm, LANES), const),
            pl.BlockSpec(mpar.shape, const),
        ],
        out_specs=pl.BlockSpec((t, d), prev(0)),
        scratch_shapes=[
            pltpu.VMEM((t, M_V), BF16),
            pltpu.VMEM((t, M_V), BF16),
            pltpu.VMEM((M_HEADS, M_DQK, M_DV + LANES), F32),
            pltpu.VMEM((M_HEADS, 8, LANES), F32),
            pltpu.VMEM((t + CONV_PAD, w_qk), F32),
        ],
        compiler_params=_params(("arbitrary",)),
        name="mixout",
    )(ha, zx, h, mw, aw, wo, g_post, zx, gx, zm, gm, mpar)


def _rope_tables(n_rows):
    half = ROT_DIM // 2
    f32 = np.float32
    inv_freq = np.power(f32(ROPE_THETA), -np.arange(0, ROT_DIM, 2, dtype=f32) / f32(ROT_DIM))
    ang = np.arange(n_rows, dtype=f32)[:, None] * inv_freq[None, :]
    cos, sin = np.cos(ang).astype(f32), np.sin(ang).astype(f32)
    zeros = np.zeros((n_rows, A_DH - ROT_DIM), f32)
    zh = np.zeros((n_rows, half), f32)
    c = np.concatenate([cos, cos, zeros + f32(1.0)], axis=1)
    sa = np.concatenate([-sin, zh, zeros], axis=1)
    sb = np.concatenate([zh, sin, zeros], axis=1)
    return np.concatenate([np.concatenate([tab, tab], axis=1) for tab in (c, sa, sb)], axis=1)


def kernel(x, meta, ffn1_pre_g, ffn1_post_g, ffn1_w_gate, ffn1_w_up, ffn1_w_down, mix_pre_g, mix_post_g, w_in, b_in, m_conv_w, m_conv_b, m_norm_g, m_w_branch, a_lambda_q1, a_lambda_k1, a_lambda_q2, a_lambda_k2, a_norm_g, a_w_branch, w_out, ffn2_pre_g, ffn2_post_g, ffn2_w_gate, ffn2_w_up, ffn2_w_down):
    batch, seq, d = x.shape
    assert batch == 1 and meta.shape == (N_META, d)
    assert (2 * d) % A_QK == 0 and COL_GM % d == 0
    l = 0
    xr = x.reshape(seq, d)
    row = lambda v: v[l].reshape(1, -1)
    bf = lambda w: w[l].astype(BF16)

    w_in_l, b_in_l = w_in[l], b_in[l]
    gate_end = GATE_OFFSET + N_GATES
    w_in_t = w_in_l.T
    b_main = jnp.concatenate([b_in_l[:GATE_OFFSET], b_in_l[gate_end:]]).reshape(1, -1)
    b_gate = jnp.pad(b_in_l[GATE_OFFSET:gate_end], (0, LANES - N_GATES)).reshape(1, -1)
    rope = _rope_tables(N_META + seq)
    rope_m, rope_x = rope[:N_META], rope[N_META:]

    hx, hmeta = _ffn(xr, row(ffn1_pre_g), row(ffn1_post_g), ffn1_w_gate, ffn1_w_up, ffn1_w_down,
                     extra=meta, layer=l, tm=FFN_TM, tf=FFN_TF)
    zx, gx, zm, gm = _inproj(hx, hmeta, row(mix_pre_g), w_in_t, b_main, b_gate, rope_x, rope_m,
                             tm=1024)

    lamv = jnp.concatenate([row(a_lambda_q1), row(a_lambda_k1), row(a_lambda_q2),
                            row(a_lambda_k2)], axis=0)
    ha = _attn(zx, zm, lamv, row(a_norm_g), tq=512)
    h2 = _mixout(zx, gx, zm, gm, ha, hx, m_conv_w[l], row(m_conv_b), row(m_norm_g),
                 bf(m_w_branch), bf(a_w_branch), bf(w_out), row(mix_post_g), t=256)
    out = _ffn(h2, row(ffn2_pre_g), row(ffn2_post_g), ffn2_w_gate, ffn2_w_up, ffn2_w_down,
               layer=l, tm=FFN_TM, tf=FFN_TF)
    return out.reshape(batch, seq, d)
```

```python
import functools

import jax
import jax.numpy as jnp
import numpy as np
from jax import lax
from jax.experimental import pallas as pl
from jax.experimental.pallas import tpu as pltpu

F32 = jnp.float32
BF16 = jnp.bfloat16

EPS = 1e-6
N_META = 16
M_HEADS = 4
M_DQK = 128
M_DV = 256
M_QK = M_HEADS * M_DQK
M_V = M_HEADS * M_DV
CONV_W = 4
A_HEADS = 8
A_DH = 64
A_DV = 2 * A_DH
A_QK = A_HEADS * 2 * A_DH
A_V = A_HEADS * A_DV
ROT_DIM = A_DH // 4
ROPE_THETA = 500000.0
LAMBDA_INIT = 0.8 - 0.6 * 1.0
Q_SCALE = A_DH ** -0.5 * 1.4426950408889634

LANES = 128
GATE_OFFSET = 2 * M_QK + 2 * M_V
N_GATES = 2 * M_HEADS
COL_QK = 0
COL_MV = 2 * M_QK
COL_MO = COL_MV + M_V
COL_AQ = COL_MO + M_V
COL_AK = COL_AQ + A_QK
COL_AV = COL_AK + A_QK
COL_GM = COL_AV + A_V

VMEM_LIMIT = 56 * 1024 * 1024
VMEM_PHYSICAL = 64 * 1024 * 1024


def _params(sem, vmem_limit=VMEM_LIMIT):
    assert vmem_limit < VMEM_PHYSICAL
    return pltpu.CompilerParams(dimension_semantics=sem, vmem_limit_bytes=vmem_limit)


def _rms(x):
    return x * lax.rsqrt(jnp.mean(x * x, axis=-1, keepdims=True) + EPS)


NORM_STRIP = 16


FFN_TM = 1024
FFN_TF = 256
FFN_VMEM_LIMIT = 62 * 1024 * 1024
FFN_EPILOGUE_GROUP = 256


def _ffn_kernel(h_ref, gpre_ref, gpost_ref, wg_ref, wu_ref, wd_ref, *rest, n_extra):
    if n_extra:
        x_ref, o_ref, ox_ref, u_scr, accx_scr = rest
    else:
        o_ref, u_scr = rest
    i = pl.program_id(0)
    f = pl.program_id(1)
    last = pl.num_programs(1) - 1
    tm = h_ref.shape[0]

    def prologue(r_lo, r_hi):
        for r0 in range(r_lo, r_hi, NORM_STRIP):
            rs = slice(r0, r0 + NORM_STRIP)
            u_scr[rs, :] = (_rms(h_ref[rs, :]) * gpre_ref[...]).astype(BF16)

    def epilogue(r_lo, r_hi):
        for r0 in range(r_lo, r_hi, NORM_STRIP):
            rs = slice(r0, r0 + NORM_STRIP)
            o_ref[rs, :] = h_ref[rs, :] + 0.5 * (_rms(o_ref[rs, :]) * gpost_ref[...])

    def swiglu_down(u):
        g = jnp.dot(u, wg_ref[...].astype(BF16), preferred_element_type=F32)
        up = jnp.dot(u, wu_ref[...].astype(BF16), preferred_element_type=F32)
        a = (g * jax.nn.sigmoid(g) * up).astype(BF16)
        return jnp.dot(a, wd_ref[...].astype(BF16), preferred_element_type=F32)

    if n_extra:
        @pl.when(jnp.logical_and(i == 0, f == 0))
        def _():
            prologue(0, tm)
            o_ref[...] = jnp.zeros_like(o_ref)
            u_scr[tm:tm + n_extra, :] = (_rms(x_ref[...]) * gpre_ref[...]).astype(BF16)
            accx_scr[...] = jnp.zeros_like(accx_scr)

        @pl.when(i == 0)
        def _():
            down = swiglu_down(u_scr[...])
            o_ref[...] += down[0:tm]
            accx_scr[...] += down[tm:tm + n_extra]

        @pl.when(jnp.logical_and(i == 0, f == last))
        def _():
            ox_ref[...] = x_ref[...] + 0.5 * (_rms(accx_scr[...]) * gpost_ref[...])
            epilogue(0, tm)

        plain_tile = i > 0
    else:
        plain_tile = True

    grp = FFN_EPILOGUE_GROUP if tm % FFN_EPILOGUE_GROUP == 0 else tm

    @pl.when(jnp.logical_and(plain_tile, f == 0))
    def _():
        for r0 in range(0, tm, grp):
            prologue(r0, r0 + grp)
            o_ref[r0:r0 + grp, :] = swiglu_down(u_scr[r0:r0 + grp, :])

    @pl.when(jnp.logical_and(plain_tile, jnp.logical_and(f > 0, f < last)))
    def _():
        o_ref[...] += swiglu_down(u_scr[0:tm, :])

    @pl.when(jnp.logical_and(plain_tile, f == last))
    def _():
        u = u_scr[0:tm, :]
        g = jnp.dot(u, wg_ref[...].astype(BF16), preferred_element_type=F32)
        up = jnp.dot(u, wu_ref[...].astype(BF16), preferred_element_type=F32)
        a = (g * jax.nn.sigmoid(g) * up).astype(BF16)
        wd = wd_ref[...].astype(BF16)
        for r0 in range(0, tm, grp):
            o_ref[r0:r0 + grp, :] += jnp.dot(a[r0:r0 + grp], wd, preferred_element_type=F32)
            epilogue(r0, r0 + grp)


def _ffn(h, g_pre, g_post, w_gate, w_up, w_down, extra=None, *, layer, tm, tf):
    rows, d = h.shape
    d_ff = w_gate.shape[2]
    n_extra = 0 if extra is None else extra.shape[0]
    vec = pl.BlockSpec((1, d), lambda i, f: (0, 0))
    tile = pl.BlockSpec((tm, d), lambda i, f: (i, 0))
    in_specs = [tile, vec, vec,
                pl.BlockSpec((None, d, tf), lambda i, f: (layer, 0, f)),
                pl.BlockSpec((None, d, tf), lambda i, f: (layer, 0, f)),
                pl.BlockSpec((None, tf, d), lambda i, f: (layer, f, 0))]
    args = [h, g_pre, g_post, w_gate, w_up, w_down]
    out_shape = jax.ShapeDtypeStruct((rows, d), F32)
    out_specs = tile
    scratch = [pltpu.VMEM((tm + n_extra, d), BF16)]
    if n_extra:
        whole = pl.BlockSpec((n_extra, d), lambda i, f: (0, 0))
        in_specs.append(whole)
        args.append(extra)
        out_shape = (out_shape, jax.ShapeDtypeStruct((n_extra, d), F32))
        out_specs = (tile, whole)
        scratch.append(pltpu.VMEM((n_extra, d), F32))
    return pl.pallas_call(
        functools.partial(_ffn_kernel, n_extra=n_extra),
        out_shape=out_shape,
        grid=(rows // tm, d_ff // tf),
        in_specs=in_specs,
        out_specs=out_specs,
        scratch_shapes=scratch,
        compiler_params=_params(("arbitrary", "arbitrary"), vmem_limit=FFN_VMEM_LIMIT),
        name="ffn",
    )(*args)


def _rope(z, c, sa, sb):
    outs = []
    for grp in range(z.shape[1] // LANES):
        x = z[:, grp * LANES:(grp + 1) * LANES]
        outs.append(x * c + pltpu.roll(x, LANES - ROT_DIM // 2, 1) * sa
                    + pltpu.roll(x, ROT_DIM // 2, 1) * sb)
    return jnp.concatenate(outs, axis=1)


_NT = (((1,), (1,)), ((), ()))
ROPE_GROUP = 256


def _inproj_kernel(h_ref, g_ref, wt_ref, b_ref, wgate_ref, bgate_ref, rope_ref,
                   xh_ref, xrope_ref, z_ref, gate_ref, zx_ref, gatex_ref, u_scr, *, tn):
    i = pl.program_id(0)
    j = pl.program_id(1)
    tm = h_ref.shape[0]
    nx = xh_ref.shape[0]

    def gates(u):
        wg = wgate_ref[...].astype(BF16)
        wg = jnp.concatenate([wg, jnp.zeros((LANES - N_GATES, wg.shape[1]), BF16)], axis=0)
        return lax.dot_general(u, wg, _NT, preferred_element_type=F32) + bgate_ref[...]

    @pl.when(j == 0)
    def _():
        for r0 in range(0, tm, NORM_STRIP):
            rs = slice(r0, r0 + NORM_STRIP)
            u_scr[rs, :] = (_rms(h_ref[rs, :]) * g_ref[...]).astype(BF16)
        gate_ref[...] = gates(u_scr[0:tm, :])

    @pl.when(jnp.logical_and(i == 0, j == 0))
    def _():
        u = (_rms(xh_ref[...]) * g_ref[...]).astype(BF16)
        u_scr[tm:tm + nx, :] = u
        gatex_ref[...] = gates(u)

    def finish(z, kind, tab):
        if kind == "plain":
            return z.astype(BF16)
        z = _rope(z, tab[:, 0:LANES], tab[:, LANES:2 * LANES], tab[:, 2 * LANES:3 * LANES])
        if kind == "q":
            z = z * Q_SCALE
        return z.astype(BF16)

    def tile_out(kind, with_extra):
        w = wt_ref[...].astype(BF16)
        if kind != "plain" and tm % ROPE_GROUP == 0:
            for r0 in range(0, tm, ROPE_GROUP):
                rs = slice(r0, r0 + ROPE_GROUP)
                z = lax.dot_general(u_scr[rs, :], w, _NT, preferred_element_type=F32) + b_ref[...]
                z_ref[rs, :] = finish(z, kind, rope_ref.at[rs, :])
            if with_extra:
                z = (lax.dot_general(u_scr[tm:tm + nx, :], w, _NT, preferred_element_type=F32)
                     + b_ref[...])
                zx_ref[...] = finish(z, kind, xrope_ref)
            return
        u = u_scr[...] if with_extra else u_scr[0:tm, :]
        z = lax.dot_general(u, w, _NT, preferred_element_type=F32) + b_ref[...]
        z_ref[...] = finish(z[0:tm], kind, rope_ref)
        if with_extra:
            zx_ref[...] = finish(z[tm:tm + nx], kind, xrope_ref)

    is_q = j == COL_AQ // tn
    is_k = j == COL_AK // tn
    plain = jnp.logical_not(jnp.logical_or(is_q, is_k))
    used = jnp.logical_or(j < COL_MO // tn,
                          jnp.logical_and(j >= COL_AK // tn, j < COL_GM // tn))
    extra = jnp.logical_and(i == 0, used)

    def tile_no_extra(kind):
        tile_out(kind, False)

        @pl.when(i == 0)
        def _():
            zx_ref[...] = jnp.zeros_like(zx_ref)

    pl.when(is_q)(functools.partial(tile_no_extra, "q"))
    for kind, cond in (("k", is_k), ("plain", plain)):
        pl.when(jnp.logical_and(cond, extra))(functools.partial(tile_out, kind, True))
        pl.when(jnp.logical_and(cond, jnp.logical_not(extra)))(
            functools.partial(tile_no_extra, kind))


def _inproj(h, xh, g, w_t, b_main, b_gate, rope, rope_x, *, tm):
    rows, d = h.shape
    nx = xh.shape[0]
    tn = A_QK
    n_a = GATE_OFFSET // tn
    n = w_t.shape[0] - N_GATES
    nj = n // tn
    assert COL_AQ == GATE_OFFSET and GATE_OFFSET % tn == 0 and n % tn == 0

    def w_row(i, j):
        return (pl.multiple_of(jnp.where(j < n_a, j * tn, j * tn + N_GATES), N_GATES), 0)

    table = pl.BlockSpec((tm, 3 * LANES), lambda i, j: (i, 0))
    table_x = pl.BlockSpec((nx, 3 * LANES), lambda i, j: (0, 0))
    return pl.pallas_call(
        functools.partial(_inproj_kernel, tn=tn),
        out_shape=(jax.ShapeDtypeStruct((rows, n), BF16),
                   jax.ShapeDtypeStruct((rows, LANES), F32),
                   jax.ShapeDtypeStruct((nx, n), BF16),
                   jax.ShapeDtypeStruct((nx, LANES), F32)),
        grid=(rows // tm, nj),
        in_specs=[
            pl.BlockSpec((tm, d), lambda i, j: (i, 0)),
            pl.BlockSpec((1, d), lambda i, j: (0, 0)),
            pl.BlockSpec((pl.Element(tn), pl.Element(d)), w_row),
            pl.BlockSpec((1, tn), lambda i, j: (0, j)),
            pl.BlockSpec((N_GATES, d), lambda i, j: (GATE_OFFSET // N_GATES, 0)),
            pl.BlockSpec((1, LANES), lambda i, j: (0, 0)),
            table,
            pl.BlockSpec((nx, d), lambda i, j: (0, 0)),
            table_x,
        ],
        out_specs=(pl.BlockSpec((tm, tn), lambda i, j: (i, j)),
                   pl.BlockSpec((tm, LANES), lambda i, j: (i, 0)),
                   pl.BlockSpec((nx, tn), lambda i, j: (0, jnp.where(i == 0, j, nj - 1))),
                   pl.BlockSpec((nx, LANES), lambda i, j: (0, 0))),
        scratch_shapes=[pltpu.VMEM((tm + nx, d), BF16)],
        compiler_params=_params(("arbitrary", "arbitrary")),
        name="inproj",
    )(h, g, w_t, b_main, w_t, b_gate, rope, xh, rope_x)


CONV_PAD = 8


def _conv_silu(xbuf, n, cw_ref, cb_ref):
    y = cb_ref[...]
    for i in range(CONV_W):
        start = CONV_PAD + i - (CONV_W - 1)
        y = y + cw_ref[i:i + 1, :] * xbuf[start:start + n, :]
    return y * jax.nn.sigmoid(y)


def _log_sigmoid(x):
    return jnp.minimum(x, 0.0) - jnp.log1p(jnp.exp(-jnp.abs(x)))


def _cumsum_rows(x):
    n = x.shape[0]
    r = lax.broadcasted_iota(jnp.int32, (n, n), 0)
    c = lax.broadcasted_iota(jnp.int32, (n, n), 1)
    tril = (r >= c).astype(F32)
    return jnp.dot(tril, x, preferred_element_type=F32, precision=lax.Precision.HIGHEST)


def _state_update(c_scr, m_scr, hd, kf, vext, li_col, b_col, g_tot, m_prev):
    log_w = (g_tot - b_col) + li_col
    m_new = jnp.maximum(g_tot + m_prev, jnp.max(log_w, axis=0, keepdims=True))
    wk = jnp.exp(log_w - m_new)
    decay = jnp.exp(g_tot + m_prev - m_new)
    kw = (kf * wk).astype(BF16)
    upd = lax.dot_general(kw, vext, (((0,), (0,)), ((), ())), preferred_element_type=F32)
    c_scr[hd] = decay * c_scr[hd] + upd
    m_scr[hd] = jnp.broadcast_to(m_new, m_scr.shape[1:])


M_KSCALE = M_DQK ** -0.5


def _mlstm_meta_init(qkm_ref, mvm_ref, gtm_ref, cw_ref, cb_ref, c_scr, m_scr, xbuf):
    nm = qkm_ref.shape[0]
    c_scr[...] = jnp.zeros_like(c_scr)
    m_scr[...] = jnp.zeros_like(m_scr)
    xbuf[0:CONV_PAD, :] = jnp.zeros((CONV_PAD, xbuf.shape[1]), F32)
    xbuf[CONV_PAD:CONV_PAD + nm, :] = qkm_ref[...].astype(F32)
    qk = _conv_silu(xbuf, nm, cw_ref, cb_ref)
    xbuf[0:CONV_PAD, :] = xbuf[nm:nm + CONV_PAD, :]
    gts = gtm_ref[...]
    b_all = _cumsum_rows(_log_sigmoid(gts))
    ones = jnp.ones((nm, LANES), BF16)
    for hd in range(M_HEADS):
        kf = qk[:, M_QK + hd * M_DQK:M_QK + (hd + 1) * M_DQK] * M_KSCALE
        vext = jnp.concatenate([mvm_ref[:, hd * M_DV:(hd + 1) * M_DV], ones], axis=1)
        b_col = b_all[:, M_HEADS + hd:M_HEADS + hd + 1]
        _state_update(c_scr, m_scr, hd, kf, vext, gts[:, hd:hd + 1], b_col,
                      b_col[nm - 1:nm, :], m_scr[hd][0:1, 0:1])


def _mlstm_chunk(qk_ref, mv_ref, mo_ref, gt_ref, cw_ref, cb_ref, ng_ref, o_ref, c_scr, m_scr, xbuf):
    t = qk_ref.shape[0]
    kscale = M_KSCALE
    xbuf[CONV_PAD:CONV_PAD + t, :] = qk_ref[...].astype(F32)
    qk = _conv_silu(xbuf, t, cw_ref, cb_ref)
    xbuf[0:CONV_PAD, :] = xbuf[t:t + CONV_PAD, :]

    gts = gt_ref[...]
    b_all = _cumsum_rows(_log_sigmoid(gts))
    lane = lax.broadcasted_iota(jnp.int32, gts.shape, 1)
    rowform = jnp.where(lane < M_HEADS, gts, b_all).T
    r = lax.broadcasted_iota(jnp.int32, (t, t), 0)
    c = lax.broadcasted_iota(jnp.int32, (t, t), 1)
    causal = r >= c
    ones = jnp.ones((t, LANES), BF16)

    for hd in range(M_HEADS):
        yield
        qh = qk[:, hd * M_DQK:(hd + 1) * M_DQK].astype(BF16)
        kf = qk[:, M_QK + hd * M_DQK:M_QK + (hd + 1) * M_DQK] * kscale
        kh = kf.astype(BF16)
        vext = jnp.concatenate([mv_ref[:, hd * M_DV:(hd + 1) * M_DV], ones], axis=1)
        li_col = gts[:, hd:hd + 1]
        b_col = b_all[:, M_HEADS + hd:M_HEADS + hd + 1]
        li_row = rowform[hd:hd + 1, :]
        b_row = rowform[M_HEADS + hd:M_HEADS + hd + 1, :]
        g_tot = b_col[t - 1:t, :]
        m_prev = m_scr[hd][0:1, 0:1]

        log_d = jnp.where(causal, (b_col - b_row) + li_row, -jnp.inf)
        m_inter = b_col + m_prev
        m_row = jnp.maximum(m_inter, jnp.max(log_d, axis=1, keepdims=True))
        s = lax.dot_general(qh, kh, (((1,), (1,)), ((), ())), preferred_element_type=F32)
        sd = (s * jnp.exp(log_d - m_row)).astype(BF16)
        inter = jnp.dot(qh, c_scr[hd].astype(BF16), preferred_element_type=F32)
        numden = jnp.exp(m_inter - m_row) * inter + jnp.dot(sd, vext, preferred_element_type=F32)
        den = jnp.maximum(jnp.abs(numden[:, M_DV:M_DV + 1]), jnp.exp(-m_row))
        hh = numden[:, :M_DV] / den
        hn = _rms(hh) * ng_ref[:, hd * M_DV:(hd + 1) * M_DV]
        og = jax.nn.sigmoid(mo_ref[:, hd * M_DV:(hd + 1) * M_DV].astype(F32))
        o_ref[:, hd * M_DV:(hd + 1) * M_DV] = (og * hn).astype(BF16)

        _state_update(c_scr, m_scr, hd, kf, vext, li_col, b_col, g_tot, m_prev)


ATT_CHUNK = 64


ATT_STRIP = 64
ATT_GROUP = 256
ATT_HEADS_PER_STEP = 2


def _attn_kernel(q_ref, k_ref, v_ref, km_ref, vm_ref, lamv_ref,
                 g_ref, o_ref, q2_scr, s0_scr, s1_scr, p0_scr, p1_scr, m_scr, alpha_scr, acc_scr):
    i = pl.program_id(1)
    tq = q_ref.shape[0]
    tk = tq
    rows = 2 * tq
    ncol = tk // LANES
    nheads = q_ref.shape[1] // A_DV
    nt = (((1,), (1,)), ((), ()))
    ones_k = jnp.ones((tk, LANES), BF16)
    s_slots = (s0_scr, s1_scr)
    p_slots = (p0_scr, p1_scr)

    def cols_of(hh):
        return slice(hh * A_DV, (hh + 1) * A_DV)

    def kblock(hh, blk):
        off = pl.multiple_of(blk * tk, tk)
        return k_ref[pl.ds(off, tk), cols_of(hh)]

    def softmax_pv(hh, slot, blk, masked, nxt=None):
        s_ref, p_ref = s_slots[slot].at[hh], p_slots[slot].at[hh]
        m_ref, alpha_ref, acc_ref = m_scr.at[hh], alpha_scr.at[hh], acc_scr.at[hh]
        off = pl.multiple_of(blk * tk, tk)
        vext = jnp.concatenate([v_ref[pl.ds(off, tk), cols_of(hh)], ones_k], axis=1)
        for g0 in range(0, rows, ATT_GROUP):
            gs = slice(g0, g0 + ATT_GROUP)
            if nxt is not None:
                nxt_slot, nxt_blk = nxt
                s_slots[nxt_slot].at[hh][gs, :] = lax.dot_general(
                    q2_scr.at[hh][gs, :], kblock(hh, nxt_blk), nt, preferred_element_type=F32)
            softmax_rows(s_ref, p_ref, m_ref, alpha_ref, masked, g0)
            pv = jnp.dot(p_ref[gs, :], vext, preferred_element_type=F32)
            alpha = alpha_ref[gs, :]
            for c in range(2):
                sl = slice(c * LANES, (c + 1) * LANES)
                acc_ref[gs, sl] = alpha * acc_ref[gs, sl] + pv[:, sl]
            yield

    def softmax_rows(s_ref, p_ref, m_scr, alpha_scr, masked, g0):
        for st in range(g0 // ATT_STRIP, (g0 + ATT_GROUP) // ATT_STRIP):
            rs = slice(st * ATT_STRIP, (st + 1) * ATT_STRIP)
            if masked:
                qchunk = ((st * ATT_STRIP) % tq) // ATT_CHUNK
                nvis = (qchunk + 1) * ATT_CHUNK
                cvis = lax.broadcasted_iota(jnp.int32, (ATT_STRIP, LANES), 1)
                cols = []
                for c in range(-(-nvis // LANES)):
                    sc = s_ref[rs, c * LANES:(c + 1) * LANES]
                    if (c + 1) * LANES > nvis:
                        sc = jnp.where(cvis < nvis - c * LANES, sc, -jnp.inf)
                    cols.append(sc)
            else:
                cols = [s_ref[rs, c * LANES:(c + 1) * LANES] for c in range(ncol)]
            part = cols[0]
            for sc in cols[1:]:
                part = jnp.maximum(part, sc)
            m_old = m_scr[rs, :]
            m_new = jnp.maximum(m_old, jnp.max(part, axis=1, keepdims=True))
            m_scr[rs, :] = m_new
            alpha_scr[rs, :] = jnp.exp2(m_old - m_new)
            if not masked:
                cols = [s_ref[rs, c * LANES:(c + 1) * LANES] for c in range(ncol)]
            pcols = [jnp.exp2(sc - m_new) for sc in cols]
            pcols += [jnp.zeros_like(pcols[0])] * (ncol - len(cols))
            p_ref[rs, :] = jnp.concatenate(pcols, axis=1).astype(BF16)

    def prologue(hh):
        q = q_ref[:, cols_of(hh)]
        lane = lax.broadcasted_iota(jnp.int32, q.shape, 1)
        zero = jnp.zeros_like(q)
        q2 = q2_scr.at[hh]
        q2[0:tq, :] = jnp.where(lane < A_DH, q, zero)
        q2[tq:rows, :] = jnp.where(lane >= A_DH, q, zero)
        yield
        s = lax.dot_general(q2[...], km_ref[:, cols_of(hh)], nt, preferred_element_type=F32)
        m0 = jnp.max(s, axis=1, keepdims=True)
        p = jnp.exp2(s - m0).astype(BF16)
        vext = jnp.concatenate([vm_ref[:, cols_of(hh)],
                                jnp.ones((vm_ref.shape[0], LANES), BF16)], axis=1)
        acc_scr[hh] = jnp.dot(p, vext, preferred_element_type=F32)
        m_scr[hh] = jnp.broadcast_to(m0, m_scr.shape[1:])
        yield
        s0_scr[hh] = lax.dot_general(q2[...], kblock(hh, 0), nt, preferred_element_type=F32)

    def chain(*gens):
        for g in gens:
            yield from g

    def run(make):
        gens = [make(hh) for hh in range(nheads)]
        while gens:
            for g in list(gens):
                try:
                    next(g)
                except StopIteration:
                    gens.remove(g)

    run(prologue)

    def pair(pr):
        run(lambda hh: chain(softmax_pv(hh, 0, 2 * pr, False, nxt=(1, 2 * pr + 1)),
                             softmax_pv(hh, 1, 2 * pr + 1, False, nxt=(0, 2 * pr + 2))))

    npairs = i // 2

    def two_pairs(t, carry):
        pair(2 * t)
        pair(2 * t + 1)
        return carry

    lax.fori_loop(0, npairs // 2, two_pairs, 0)

    @pl.when(npairs % 2 == 1)
    def _():
        pair(npairs - 1)

    def finalize(hh):
        lam = (jnp.exp(jnp.sum(lamv_ref[0:1, :] * lamv_ref[1:2, :], axis=1, keepdims=True))
               - jnp.exp(jnp.sum(lamv_ref[2:3, :] * lamv_ref[3:4, :], axis=1, keepdims=True))
               + LAMBDA_INIT)
        o = acc_scr[hh, :, 0:A_DV] / acc_scr[hh, :, A_DV:A_DV + 1]
        o = o[:tq] - lam * o[tq:]
        o_ref[:, cols_of(hh)] = (_rms(o) * g_ref[:, cols_of(hh)] * (1.0 - LAMBDA_INIT)).astype(BF16)
        yield

    def lag(n):
        for _ in range(n):
            yield

    @pl.when(i % 2 == 0)
    def _():
        run(lambda hh: chain(lag(hh), softmax_pv(hh, 0, i, True), finalize(hh)))

    @pl.when(i % 2 == 1)
    def _():
        run(lambda hh: chain(lag(hh), softmax_pv(hh, 0, i - 1, False, nxt=(1, i)),
                             softmax_pv(hh, 1, i, True), finalize(hh)))


def _attn(zx, zm, lamv, norm_g, *, tq):
    rows = zx.shape[0]
    nm = zm.shape[0]
    nh = ATT_HEADS_PER_STEP
    w = nh * A_DV
    return pl.pallas_call(
        _attn_kernel,
        out_shape=jax.ShapeDtypeStruct((rows, A_V), BF16),
        grid=(A_HEADS // nh, rows // tq),
        in_specs=[
            pl.BlockSpec((tq, w), lambda h, i: (i, COL_AQ // w + h)),
            pl.BlockSpec((rows, w), lambda h, i: (0, COL_AK // w + h)),
            pl.BlockSpec((rows, w), lambda h, i: (0, COL_AV // w + h)),
            pl.BlockSpec((nm, w), lambda h, i: (0, COL_AK // w + h)),
            pl.BlockSpec((nm, w), lambda h, i: (0, COL_AV // w + h)),
            pl.BlockSpec(lamv.shape, lambda h, i: (0, 0)),
            pl.BlockSpec((1, w), lambda h, i: (0, h)),
        ],
        out_specs=pl.BlockSpec((tq, w), lambda h, i: (i, h)),
        scratch_shapes=[
            pltpu.VMEM((nh, 2 * tq, A_DV), BF16),
            pltpu.VMEM((nh, 2 * tq, tq), F32),
            pltpu.VMEM((nh, 2 * tq, tq), F32),
            pltpu.VMEM((nh, 2 * tq, tq), BF16),
            pltpu.VMEM((nh, 2 * tq, tq), BF16),
            pltpu.VMEM((nh, 2 * tq, LANES), F32),
            pltpu.VMEM((nh, 2 * tq, LANES), F32),
            pltpu.VMEM((nh, 2 * tq, 2 * LANES), F32),
        ],
        compiler_params=_params(("parallel", "arbitrary")),
        name="diffattn",
    )(zx, zx, zx, zm, zm, lamv, norm_g)


def _mixout_kernel(ha_ref, gmga_ref, h_ref, mw_ref, aw_ref, wo_ref, gpost_ref,
                   zml_ref, gt_ref, zmlm_ref, gtm_ref, mpar_ref,
                   o_ref, hm_scr, hm_prev_scr, c_scr, m_scr, xbuf):
    s = pl.program_id(0)
    last = pl.num_programs(0) - 1
    w_qk = 2 * M_QK
    qk_ref = zml_ref.at[:, COL_QK:COL_QK + w_qk]
    mv_ref = zml_ref.at[:, COL_MV:COL_MV + M_V]
    mo_ref = zml_ref.at[:, COL_MO:COL_MO + M_V]
    qkm_ref = zmlm_ref.at[:, COL_QK:COL_QK + w_qk]
    mvm_ref = zmlm_ref.at[:, COL_MV:COL_MV + M_V]
    cw_ref = mpar_ref.at[0:CONV_W, :]
    cb_ref = mpar_ref.at[CONV_W:CONV_W + 1, :]
    ng_ref = mpar_ref.at[CONV_W + 1:CONV_W + 2, :]

    def merge():
        hm_prev_scr[...] = hm_scr[...]
        br_m = jnp.dot(hm_prev_scr[...], mw_ref[...], preferred_element_type=F32)
        yield
        br_a = jnp.dot(ha_ref[...], aw_ref[...], preferred_element_type=F32)
        yield
        d = h_ref.shape[1]
        y = (jax.nn.sigmoid(gmga_ref[:, 0:d].astype(F32)) * br_m
             + jax.nn.sigmoid(gmga_ref[:, d:2 * d].astype(F32)) * br_a)
        yield
        out = jnp.dot(y.astype(BF16), wo_ref[...], preferred_element_type=F32)
        yield
        o_ref[...] = h_ref[...] + _rms(out) * gpost_ref[...]

    def mlstm():
        return _mlstm_chunk(qk_ref, mv_ref, mo_ref, gt_ref, cw_ref, cb_ref, ng_ref, hm_scr,
                            c_scr, m_scr, xbuf)

    def run(*gens):
        gens = list(gens)
        while gens:
            for g in list(gens):
                try:
                    next(g)
                except StopIteration:
                    gens.remove(g)

    @pl.when(s == 0)
    def _():
        _mlstm_meta_init(qkm_ref, mvm_ref, gtm_ref, cw_ref, cb_ref, c_scr, m_scr, xbuf)
        run(mlstm())

    @pl.when(jnp.logical_and(s > 0, s < last))
    def _():
        run(merge(), mlstm())

    @pl.when(s == last)
    def _():
        run(merge())


def _mixout(zx, gx, zm, gm, ha, h, conv_w, conv_b, m_norm_g, mw, aw, wo, g_post, *, t):
    rows, d = h.shape
    nm = zm.shape[0]
    n = rows // t
    w_qk = 2 * M_QK
    w_ml = w_qk + 2 * M_V
    assert (COL_QK, COL_MV, COL_MO) == (0, w_qk, w_qk + M_V) and M_V == w_qk
    pad = jnp.zeros((8 - CONV_W - 2, w_qk), F32)
    mpar = jnp.concatenate([conv_w, conv_b, m_norm_g, pad], axis=0)
    const = lambda s: (0, 0)
    prev = lambda col: (lambda s: (jnp.maximum(s - 1, 0), col))
    cur = lambda col: (lambda s: (jnp.minimum(s, n - 1), col))
    return pl.pallas_call(
        _mixout_kernel,
        out_shape=jax.ShapeDtypeStruct((rows, d), F32),
        grid=(n + 1,),
        in_specs=[
            pl.BlockSpec((t, A_V), prev(0)),
            pl.BlockSpec((pl.Element(t), pl.Element(2 * d)),
                         lambda s: (pl.multiple_of(jnp.maximum(s - 1, 0) * t, t), COL_GM)),
            pl.BlockSpec((t, d), prev(0)),
            pl.BlockSpec((M_V, d), const),
            pl.BlockSpec((A_V, d), const),
            pl.BlockSpec((d, d), const),
            pl.BlockSpec((1, d), const),
            pl.BlockSpec((t, w_ml), cur(0)),
            pl.BlockSpec((t, LANES), cur(0)),
            pl.BlockSpec((nm, w_ml), const),
            pl.BlockSpec((nm, LANES), const),
            pl.BlockSpec(mpar.shape, const),
        ],
        out_specs=pl.BlockSpec((t, d), prev(0)),
        scratch_shapes=[
            pltpu.VMEM((t, M_V), BF16),
            pltpu.VMEM((t, M_V), BF16),
            pltpu.VMEM((M_HEADS, M_DQK, M_DV + LANES), F32),
            pltpu.VMEM((M_HEADS, 8, LANES), F32),
            pltpu.VMEM((t + CONV_PAD, w_qk), F32),
        ],
        compiler_params=_params(("arbitrary",)),
        name="mixout",
    )(ha, zx, h, mw, aw, wo, g_post, zx, gx, zm, gm, mpar)


def _rope_tables(n_rows):
    half = ROT_DIM // 2
    f32 = np.float32
    inv_freq = np.power(f32(ROPE_THETA), -np.arange(0, ROT_DIM, 2, dtype=f32) / f32(ROT_DIM))
    ang = np.arange(n_rows, dtype=f32)[:, None] * inv_freq[None, :]
    cos, sin = np.cos(ang).astype(f32), np.sin(ang).astype(f32)
    zeros = np.zeros((n_rows, A_DH - ROT_DIM), f32)
    zh = np.zeros((n_rows, half), f32)
    c = np.concatenate([cos, cos, zeros + f32(1.0)], axis=1)
    sa = np.concatenate([-sin, zh, zeros], axis=1)
    sb = np.concatenate([zh, sin, zeros], axis=1)
    return np.concatenate([np.concatenate([tab, tab], axis=1) for tab in (c, sa, sb)], axis=1)


def kernel(x, meta, ffn1_pre_g, ffn1_post_g, ffn1_w_gate, ffn1_w_up, ffn1_w_down, mix_pre_g, mix_post_g, w_in, b_in, m_conv_w, m_conv_b, m_norm_g, m_w_branch, a_lambda_q1, a_lambda_k1, a_lambda_q2, a_lambda_k2, a_norm_g, a_w_branch, w_out, ffn2_pre_g, ffn2_post_g, ffn2_w_gate, ffn2_w_up, ffn2_w_down):
    batch, seq, d = x.shape
    assert batch == 1 and meta.shape == (N_META, d)
    assert (2 * d) % A_QK == 0 and COL_GM % d == 0
    l = 0
    xr = x.reshape(seq, d)
    row = lambda v: v[l].reshape(1, -1)
    bf = lambda w: w[l].astype(BF16)

    w_in_l, b_in_l = w_in[l], b_in[l]
    gate_end = GATE_OFFSET + N_GATES
    w_in_t = w_in_l.T
    b_main = jnp.concatenate([b_in_l[:GATE_OFFSET], b_in_l[gate_end:]]).reshape(1, -1)
    b_gate = jnp.pad(b_in_l[GATE_OFFSET:gate_end], (0, LANES - N_GATES)).reshape(1, -1)
    rope = _rope_tables(N_META + seq)
    rope_m, rope_x = rope[:N_META], rope[N_META:]

    hx, hmeta = _ffn(xr, row(ffn1_pre_g), row(ffn1_post_g), ffn1_w_gate, ffn1_w_up, ffn1_w_down,
                     extra=meta, layer=l, tm=FFN_TM, tf=FFN_TF)
    zx, gx, zm, gm = _inproj(hx, hmeta, row(mix_pre_g), w_in_t, b_main, b_gate, rope_x, rope_m,
                             tm=1024)

    lamv = jnp.concatenate([row(a_lambda_q1), row(a_lambda_k1), row(a_lambda_q2),
                            row(a_lambda_k2)], axis=0)
    ha = _attn(zx, zm, lamv, row(a_norm_g), tq=512)
    h2 = _mixout(zx, gx, zm, gm, ha, hx, m_conv_w[l], row(m_conv_b), row(m_norm_g),
                 bf(m_w_branch), bf(a_w_branch), bf(w_out), row(mix_post_g), t=256)
    out = _ffn(h2, row(ffn2_pre_g), row(ffn2_post_g), ffn2_w_gate, ffn2_w_up, ffn2_w_down,
               layer=l, tm=FFN_TM, tf=FFN_TF)
    return out.reshape(batch, seq, d)
```

```python
import functools

import jax
import jax.numpy as jnp
import numpy as np
from jax import lax
from jax.experimental import pallas as pl
from jax.experimental.pallas import tpu as pltpu

F32 = jnp.float32
BF16 = jnp.bfloat16

EPS = 1e-6
N_META = 16
M_HEADS = 4
M_DQK = 128
M_DV = 256
M_QK = M_HEADS * M_DQK
M_V = M_HEADS * M_DV
CONV_W = 4
A_HEADS = 8
A_DH = 64
A_DV = 2 * A_DH
A_QK = A_HEADS * 2 * A_DH
A_V = A_HEADS * A_DV
ROT_DIM = A_DH // 4
ROPE_THETA = 500000.0
LAMBDA_INIT = 0.8 - 0.6 * 1.0
Q_SCALE = A_DH ** -0.5 * 1.4426950408889634

LANES = 128
GATE_OFFSET = 2 * M_QK + 2 * M_V
N_GATES = 2 * M_HEADS
COL_QK = 0
COL_MV = 2 * M_QK
COL_MO = COL_MV + M_V
COL_AQ = COL_MO + M_V
COL_AK = COL_AQ + A_QK
COL_AV = COL_AK + A_QK
COL_GM = COL_AV + A_V

VMEM_LIMIT = 56 * 1024 * 1024
VMEM_PHYSICAL = 64 * 1024 * 1024


def _params(sem, vmem_limit=VMEM_LIMIT):
    assert vmem_limit < VMEM_PHYSICAL
    return pltpu.CompilerParams(dimension_semantics=sem, vmem_limit_bytes=vmem_limit)


def _rms(x):
    return x * lax.rsqrt(jnp.mean(x * x, axis=-1, keepdims=True) + EPS)


NORM_STRIP = 16


FFN_TM = 1024
FFN_TF = 256
FFN_VMEM_LIMIT = 62 * 1024 * 1024
FFN_EPILOGUE_GROUP = 256


def _ffn_kernel(h_ref, gpre_ref, gpost_ref, wg_ref, wu_ref, wd_ref, *rest, n_extra):
    if n_extra:
        x_ref, o_ref, ox_ref, u_scr, accx_scr = rest
    else:
        o_ref, u_scr = rest
    i = pl.program_id(0)
    f = pl.program_id(1)
    last = pl.num_programs(1) - 1
    tm = h_ref.shape[0]

    def prologue(r_lo, r_hi):
        for r0 in range(r_lo, r_hi, NORM_STRIP):
            rs = slice(r0, r0 + NORM_STRIP)
            u_scr[rs, :] = (_rms(h_ref[rs, :]) * gpre_ref[...]).astype(BF16)

    def epilogue(r_lo, r_hi):
        for r0 in range(r_lo, r_hi, NORM_STRIP):
            rs = slice(r0, r0 + NORM_STRIP)
            o_ref[rs, :] = h_ref[rs, :] + 0.5 * (_rms(o_ref[rs, :]) * gpost_ref[...])

    def gated(u):
        g = lax.dot_general(u, wg_ref[...].astype(BF16), _NT, preferred_element_type=F32)
        up = lax.dot_general(u, wu_ref[...].astype(BF16), _NT, preferred_element_type=F32)
        return (g * jax.nn.sigmoid(g) * up).astype(BF16)

    def swiglu_down(u):
        return jnp.dot(gated(u), wd_ref[...].astype(BF16), preferred_element_type=F32)

    if n_extra:
        @pl.when(jnp.logical_and(i == 0, f == 0))
        def _():
            prologue(0, tm)
            o_ref[...] = jnp.zeros_like(o_ref)
            u_scr[tm:tm + n_extra, :] = (_rms(x_ref[...]) * gpre_ref[...]).astype(BF16)
            accx_scr[...] = jnp.zeros_like(accx_scr)

        @pl.when(i == 0)
        def _():
            down = swiglu_down(u_scr[...])
            o_ref[...] += down[0:tm]
            accx_scr[...] += down[tm:tm + n_extra]

        @pl.when(jnp.logical_and(i == 0, f == last))
        def _():
            ox_ref[...] = x_ref[...] + 0.5 * (_rms(accx_scr[...]) * gpost_ref[...])
            epilogue(0, tm)

        plain_tile = i > 0
    else:
        plain_tile = True

    grp = FFN_EPILOGUE_GROUP if tm % FFN_EPILOGUE_GROUP == 0 else tm

    @pl.when(jnp.logical_and(plain_tile, f == 0))
    def _():
        for r0 in range(0, tm, grp):
            prologue(r0, r0 + grp)
            o_ref[r0:r0 + grp, :] = swiglu_down(u_scr[r0:r0 + grp, :])

    @pl.when(jnp.logical_and(plain_tile, jnp.logical_and(f > 0, f < last)))
    def _():
        o_ref[...] += swiglu_down(u_scr[0:tm, :])

    @pl.when(jnp.logical_and(plain_tile, f == last))
    def _():
        a = gated(u_scr[0:tm, :])
        wd = wd_ref[...].astype(BF16)
        for r0 in range(0, tm, grp):
            o_ref[r0:r0 + grp, :] += jnp.dot(a[r0:r0 + grp], wd, preferred_element_type=F32)
            epilogue(r0, r0 + grp)


def _ffn(h, g_pre, g_post, w_gate, w_up, w_down, extra=None, *, layer, tm, tf):
    rows, d = h.shape
    d_ff = w_gate.shape[2]
    n_extra = 0 if extra is None else extra.shape[0]
    vec = pl.BlockSpec((1, d), lambda i, f: (0, 0))
    tile = pl.BlockSpec((tm, d), lambda i, f: (i, 0))
    w_rows = pl.BlockSpec((None, tf, d), lambda i, f: (layer, f, 0))
    in_specs = [tile, vec, vec, w_rows, w_rows, w_rows]
    args = [h, g_pre, g_post, jnp.swapaxes(w_gate, 1, 2), jnp.swapaxes(w_up, 1, 2), w_down]
    out_shape = jax.ShapeDtypeStruct((rows, d), F32)
    out_specs = tile
    scratch = [pltpu.VMEM((tm + n_extra, d), BF16)]
    if n_extra:
        whole = pl.BlockSpec((n_extra, d), lambda i, f: (0, 0))
        in_specs.append(whole)
        args.append(extra)
        out_shape = (out_shape, jax.ShapeDtypeStruct((n_extra, d), F32))
        out_specs = (tile, whole)
        scratch.append(pltpu.VMEM((n_extra, d), F32))
    return pl.pallas_call(
        functools.partial(_ffn_kernel, n_extra=n_extra),
        out_shape=out_shape,
        grid=(rows // tm, d_ff // tf),
        in_specs=in_specs,
        out_specs=out_specs,
        scratch_shapes=scratch,
        compiler_params=_params(("arbitrary", "arbitrary"), vmem_limit=FFN_VMEM_LIMIT),
        name="ffn",
    )(*args)


def _rope(z, c, sa, sb):
    outs = []
    for grp in range(z.shape[1] // LANES):
        x = z[:, grp * LANES:(grp + 1) * LANES]
        outs.append(x * c + pltpu.roll(x, LANES - ROT_DIM // 2, 1) * sa
                    + pltpu.roll(x, ROT_DIM // 2, 1) * sb)
    return jnp.concatenate(outs, axis=1)


_NT = (((1,), (1,)), ((), ()))
ROPE_GROUP = 256


def _inproj_kernel(h_ref, g_ref, wt_ref, b_ref, wgate_ref, bgate_ref, rope_ref,
                   xh_ref, xrope_ref, z_ref, gate_ref, zx_ref, gatex_ref, u_scr, *, tn):
    i = pl.program_id(0)
    j = pl.program_id(1)
    tm = h_ref.shape[0]
    nx = xh_ref.shape[0]

    def gates(u):
        wg = wgate_ref[...].astype(BF16)
        wg = jnp.concatenate([wg, jnp.zeros((LANES - N_GATES, wg.shape[1]), BF16)], axis=0)
        return lax.dot_general(u, wg, _NT, preferred_element_type=F32) + bgate_ref[...]

    @pl.when(j == 0)
    def _():
        for r0 in range(0, tm, NORM_STRIP):
            rs = slice(r0, r0 + NORM_STRIP)
            u_scr[rs, :] = (_rms(h_ref[rs, :]) * g_ref[...]).astype(BF16)
        gate_ref[...] = gates(u_scr[0:tm, :])

    @pl.when(jnp.logical_and(i == 0, j == 0))
    def _():
        u = (_rms(xh_ref[...]) * g_ref[...]).astype(BF16)
        u_scr[tm:tm + nx, :] = u
        gatex_ref[...] = gates(u)

    def finish(z, kind, tab):
        if kind == "plain":
            return z.astype(BF16)
        z = _rope(z, tab[:, 0:LANES], tab[:, LANES:2 * LANES], tab[:, 2 * LANES:3 * LANES])
        if kind == "q":
            z = z * Q_SCALE
        return z.astype(BF16)

    def tile_out(kind, with_extra):
        w = wt_ref[...].astype(BF16)
        if kind != "plain" and tm % ROPE_GROUP == 0:
            for r0 in range(0, tm, ROPE_GROUP):
                rs = slice(r0, r0 + ROPE_GROUP)
                z = lax.dot_general(u_scr[rs, :], w, _NT, preferred_element_type=F32) + b_ref[...]
                z_ref[rs, :] = finish(z, kind, rope_ref.at[rs, :])
            if with_extra:
                z = (lax.dot_general(u_scr[tm:tm + nx, :], w, _NT, preferred_element_type=F32)
                     + b_ref[...])
                zx_ref[...] = finish(z, kind, xrope_ref)
            return
        u = u_scr[...] if with_extra else u_scr[0:tm, :]
        z = lax.dot_general(u, w, _NT, preferred_element_type=F32) + b_ref[...]
        z_ref[...] = finish(z[0:tm], kind, rope_ref)
        if with_extra:
            zx_ref[...] = finish(z[tm:tm + nx], kind, xrope_ref)

    is_q = j == COL_AQ // tn
    is_k = j == COL_AK // tn
    plain = jnp.logical_not(jnp.logical_or(is_q, is_k))
    used = jnp.logical_or(j < COL_MO // tn,
                          jnp.logical_and(j >= COL_AK // tn, j < COL_GM // tn))
    extra = jnp.logical_and(i == 0, used)

    def tile_no_extra(kind):
        tile_out(kind, False)

        @pl.when(i == 0)
        def _():
            zx_ref[...] = jnp.zeros_like(zx_ref)

    pl.when(is_q)(functools.partial(tile_no_extra, "q"))
    for kind, cond in (("k", is_k), ("plain", plain)):
        pl.when(jnp.logical_and(cond, extra))(functools.partial(tile_out, kind, True))
        pl.when(jnp.logical_and(cond, jnp.logical_not(extra)))(
            functools.partial(tile_no_extra, kind))


def _inproj(h, xh, g, w_t, b_main, b_gate, rope, rope_x, *, tm):
    rows, d = h.shape
    nx = xh.shape[0]
    tn = A_QK
    n_a = GATE_OFFSET // tn
    n = w_t.shape[0] - N_GATES
    nj = n // tn
    assert COL_AQ == GATE_OFFSET and GATE_OFFSET % tn == 0 and n % tn == 0

    def w_row(i, j):
        return (pl.multiple_of(jnp.where(j < n_a, j * tn, j * tn + N_GATES), N_GATES), 0)

    table = pl.BlockSpec((tm, 3 * LANES), lambda i, j: (i, 0))
    table_x = pl.BlockSpec((nx, 3 * LANES), lambda i, j: (0, 0))
    return pl.pallas_call(
        functools.partial(_inproj_kernel, tn=tn),
        out_shape=(jax.ShapeDtypeStruct((rows, n), BF16),
                   jax.ShapeDtypeStruct((rows, LANES), F32),
                   jax.ShapeDtypeStruct((nx, n), BF16),
                   jax.ShapeDtypeStruct((nx, LANES), F32)),
        grid=(rows // tm, nj),
        in_specs=[
            pl.BlockSpec((tm, d), lambda i, j: (i, 0)),
            pl.BlockSpec((1, d), lambda i, j: (0, 0)),
            pl.BlockSpec((pl.Element(tn), pl.Element(d)), w_row),
            pl.BlockSpec((1, tn), lambda i, j: (0, j)),
            pl.BlockSpec((N_GATES, d), lambda i, j: (GATE_OFFSET // N_GATES, 0)),
            pl.BlockSpec((1, LANES), lambda i, j: (0, 0)),
            table,
            pl.BlockSpec((nx, d), lambda i, j: (0, 0)),
            table_x,
        ],
        out_specs=(pl.BlockSpec((tm, tn), lambda i, j: (i, j)),
                   pl.BlockSpec((tm, LANES), lambda i, j: (i, 0)),
                   pl.BlockSpec((nx, tn), lambda i, j: (0, jnp.where(i == 0, j, nj - 1))),
                   pl.BlockSpec((nx, LANES), lambda i, j: (0, 0))),
        scratch_shapes=[pltpu.VMEM((tm + nx, d), BF16)],
        compiler_params=_params(("arbitrary", "arbitrary")),
        name="inproj",
    )(h, g, w_t, b_main, w_t, b_gate, rope, xh, rope_x)


CONV_PAD = 8


def _conv_silu(xbuf, n, cw_ref, cb_ref):
    y = cb_ref[...]
    for i in range(CONV_W):
        start = CONV_PAD + i - (CONV_W - 1)
        y = y + cw_ref[i:i + 1, :] * xbuf[start:start + n, :]
    return y * jax.nn.sigmoid(y)


def _log_sigmoid(x):
    return jnp.minimum(x, 0.0) - jnp.log1p(jnp.exp(-jnp.abs(x)))


def _cumsum_rows(x):
    n = x.shape[0]
    r = lax.broadcasted_iota(jnp.int32, (n, n), 0)
    c = lax.broadcasted_iota(jnp.int32, (n, n), 1)
    tril = (r >= c).astype(F32)
    return jnp.dot(tril, x, preferred_element_type=F32, precision=lax.Precision.HIGHEST)


def _state_update(c_scr, m_scr, hd, kf, vext, li_col, b_col, g_tot, m_prev):
    log_w = (g_tot - b_col) + li_col
    m_new = jnp.maximum(g_tot + m_prev, jnp.max(log_w, axis=0, keepdims=True))
    wk = jnp.exp(log_w - m_new)
    decay = jnp.exp(g_tot + m_prev - m_new)
    kw = (kf * wk).astype(BF16)
    upd = lax.dot_general(kw, vext, (((0,), (0,)), ((), ())), preferred_element_type=F32)
    c_scr[hd] = decay * c_scr[hd] + upd
    m_scr[hd] = jnp.broadcast_to(m_new, m_scr.shape[1:])


M_KSCALE = M_DQK ** -0.5


def _mlstm_meta_init(qkm_ref, mvm_ref, gtm_ref, cw_ref, cb_ref, c_scr, m_scr, xbuf):
    nm = qkm_ref.shape[0]
    c_scr[...] = jnp.zeros_like(c_scr)
    m_scr[...] = jnp.zeros_like(m_scr)
    xbuf[0:CONV_PAD, :] = jnp.zeros((CONV_PAD, xbuf.shape[1]), F32)
    xbuf[CONV_PAD:CONV_PAD + nm, :] = qkm_ref[...].astype(F32)
    qk = _conv_silu(xbuf, nm, cw_ref, cb_ref)
    xbuf[0:CONV_PAD, :] = xbuf[nm:nm + CONV_PAD, :]
    gts = gtm_ref[...]
    b_all = _cumsum_rows(_log_sigmoid(gts))
    ones = jnp.ones((nm, LANES), BF16)
    for hd in range(M_HEADS):
        kf = qk[:, M_QK + hd * M_DQK:M_QK + (hd + 1) * M_DQK] * M_KSCALE
        vext = jnp.concatenate([mvm_ref[:, hd * M_DV:(hd + 1) * M_DV], ones], axis=1)
        b_col = b_all[:, M_HEADS + hd:M_HEADS + hd + 1]
        _state_update(c_scr, m_scr, hd, kf, vext, gts[:, hd:hd + 1], b_col,
                      b_col[nm - 1:nm, :], m_scr[hd][0:1, 0:1])


def _mlstm_chunk(qk_ref, mv_ref, mo_ref, gt_ref, cw_ref, cb_ref, ng_ref, o_ref, c_scr, m_scr, xbuf):
    t = qk_ref.shape[0]
    kscale = M_KSCALE
    xbuf[CONV_PAD:CONV_PAD + t, :] = qk_ref[...].astype(F32)
    qk = _conv_silu(xbuf, t, cw_ref, cb_ref)
    xbuf[0:CONV_PAD, :] = xbuf[t:t + CONV_PAD, :]

    gts = gt_ref[...]
    b_all = _cumsum_rows(_log_sigmoid(gts))
    lane = lax.broadcasted_iota(jnp.int32, gts.shape, 1)
    rowform = jnp.where(lane < M_HEADS, gts, b_all).T
    r = lax.broadcasted_iota(jnp.int32, (t, t), 0)
    c = lax.broadcasted_iota(jnp.int32, (t, t), 1)
    causal = r >= c
    ones = jnp.ones((t, LANES), BF16)

    for hd in range(M_HEADS):
        yield
        qh = qk[:, hd * M_DQK:(hd + 1) * M_DQK].astype(BF16)
        kf = qk[:, M_QK + hd * M_DQK:M_QK + (hd + 1) * M_DQK] * kscale
        kh = kf.astype(BF16)
        vext = jnp.concatenate([mv_ref[:, hd * M_DV:(hd + 1) * M_DV], ones], axis=1)
        li_col = gts[:, hd:hd + 1]
        b_col = b_all[:, M_HEADS + hd:M_HEADS + hd + 1]
        li_row = rowform[hd:hd + 1, :]
        b_row = rowform[M_HEADS + hd:M_HEADS + hd + 1, :]
        g_tot = b_col[t - 1:t, :]
        m_prev = m_scr[hd][0:1, 0:1]

        log_d = jnp.where(causal, (b_col - b_row) + li_row, -jnp.inf)
        m_inter = b_col + m_prev
        m_row = jnp.maximum(m_inter, jnp.max(log_d, axis=1, keepdims=True))
        s = lax.dot_general(qh, kh, (((1,), (1,)), ((), ())), preferred_element_type=F32)
        sd = (s * jnp.exp(log_d - m_row)).astype(BF16)
        inter = jnp.dot(qh, c_scr[hd].astype(BF16), preferred_element_type=F32)
        numden = jnp.exp(m_inter - m_row) * inter + jnp.dot(sd, vext, preferred_element_type=F32)
        den = jnp.maximum(jnp.abs(numden[:, M_DV:M_DV + 1]), jnp.exp(-m_row))
        hh = numden[:, :M_DV] / den
        hn = _rms(hh) * ng_ref[:, hd * M_DV:(hd + 1) * M_DV]
        og = jax.nn.sigmoid(mo_ref[:, hd * M_DV:(hd + 1) * M_DV].astype(F32))
        o_ref[:, hd * M_DV:(hd + 1) * M_DV] = (og * hn).astype(BF16)

        _state_update(c_scr, m_scr, hd, kf, vext, li_col, b_col, g_tot, m_prev)


ATT_CHUNK = 64


ATT_STRIP = 64
ATT_GROUP = 256
ATT_HEADS_PER_STEP = 2


def _attn_kernel(q_ref, k_ref, v_ref, km_ref, vm_ref, lamv_ref,
                 g_ref, o_ref, q2_scr, s0_scr, s1_scr, p0_scr, p1_scr, m_scr, alpha_scr, acc_scr):
    i = pl.program_id(1)
    tq = q_ref.shape[0]
    tk = tq
    rows = 2 * tq
    ncol = tk // LANES
    nheads = q_ref.shape[1] // A_DV
    nt = (((1,), (1,)), ((), ()))
    ones_k = jnp.ones((tk, LANES), BF16)
    s_slots = (s0_scr, s1_scr)
    p_slots = (p0_scr, p1_scr)

    def cols_of(hh):
        return slice(hh * A_DV, (hh + 1) * A_DV)

    def kblock(hh, blk):
        off = pl.multiple_of(blk * tk, tk)
        return k_ref[pl.ds(off, tk), cols_of(hh)]

    def softmax_pv(hh, slot, blk, masked, nxt=None):
        s_ref, p_ref = s_slots[slot].at[hh], p_slots[slot].at[hh]
        m_ref, alpha_ref, acc_ref = m_scr.at[hh], alpha_scr.at[hh], acc_scr.at[hh]
        off = pl.multiple_of(blk * tk, tk)
        vext = jnp.concatenate([v_ref[pl.ds(off, tk), cols_of(hh)], ones_k], axis=1)
        for g0 in range(0, rows, ATT_GROUP):
            gs = slice(g0, g0 + ATT_GROUP)
            if nxt is not None:
                nxt_slot, nxt_blk = nxt
                s_slots[nxt_slot].at[hh][gs, :] = lax.dot_general(
                    q2_scr.at[hh][gs, :], kblock(hh, nxt_blk), nt, preferred_element_type=F32)
            softmax_rows(s_ref, p_ref, m_ref, alpha_ref, masked, g0)
            pv = jnp.dot(p_ref[gs, :], vext, preferred_element_type=F32)
            alpha = alpha_ref[gs, :]
            for c in range(2):
                sl = slice(c * LANES, (c + 1) * LANES)
                acc_ref[gs, sl] = alpha * acc_ref[gs, sl] + pv[:, sl]
            yield

    def softmax_rows(s_ref, p_ref, m_scr, alpha_scr, masked, g0):
        for st in range(g0 // ATT_STRIP, (g0 + ATT_GROUP) // ATT_STRIP):
            rs = slice(st * ATT_STRIP, (st + 1) * ATT_STRIP)
            if masked:
                qchunk = ((st * ATT_STRIP) % tq) // ATT_CHUNK
                nvis = (qchunk + 1) * ATT_CHUNK
                cvis = lax.broadcasted_iota(jnp.int32, (ATT_STRIP, LANES), 1)
                cols = []
                for c in range(-(-nvis // LANES)):
                    sc = s_ref[rs, c * LANES:(c + 1) * LANES]
                    if (c + 1) * LANES > nvis:
                        sc = jnp.where(cvis < nvis - c * LANES, sc, -jnp.inf)
                    cols.append(sc)
            else:
                cols = [s_ref[rs, c * LANES:(c + 1) * LANES] for c in range(ncol)]
            part = cols[0]
            for sc in cols[1:]:
                part = jnp.maximum(part, sc)
            m_old = m_scr[rs, :]
            m_new = jnp.maximum(m_old, jnp.max(part, axis=1, keepdims=True))
            m_scr[rs, :] = m_new
            alpha_scr[rs, :] = jnp.exp2(m_old - m_new)
            if not masked:
                cols = [s_ref[rs, c * LANES:(c + 1) * LANES] for c in range(ncol)]
            pcols = [jnp.exp2(sc - m_new) for sc in cols]
            pcols += [jnp.zeros_like(pcols[0])] * (ncol - len(cols))
            p_ref[rs, :] = jnp.concatenate(pcols, axis=1).astype(BF16)

    def prologue(hh):
        q = q_ref[:, cols_of(hh)]
        lane = lax.broadcasted_iota(jnp.int32, q.shape, 1)
        zero = jnp.zeros_like(q)
        q2 = q2_scr.at[hh]
        q2[0:tq, :] = jnp.where(lane < A_DH, q, zero)
        q2[tq:rows, :] = jnp.where(lane >= A_DH, q, zero)
        yield
        s = lax.dot_general(q2[...], km_ref[:, cols_of(hh)], nt, preferred_element_type=F32)
        m0 = jnp.max(s, axis=1, keepdims=True)
        p = jnp.exp2(s - m0).astype(BF16)
        vext = jnp.concatenate([vm_ref[:, cols_of(hh)],
                                jnp.ones((vm_ref.shape[0], LANES), BF16)], axis=1)
        acc_scr[hh] = jnp.dot(p, vext, preferred_element_type=F32)
        m_scr[hh] = jnp.broadcast_to(m0, m_scr.shape[1:])
        yield
        s0_scr[hh] = lax.dot_general(q2[...], kblock(hh, 0), nt, preferred_element_type=F32)

    def chain(*gens):
        for g in gens:
            yield from g

    def run(make):
        gens = [make(hh) for hh in range(nheads)]
        while gens:
            for g in list(gens):
                try:
                    next(g)
                except StopIteration:
                    gens.remove(g)

    run(prologue)

    def pair(pr):
        run(lambda hh: chain(softmax_pv(hh, 0, 2 * pr, False, nxt=(1, 2 * pr + 1)),
                             softmax_pv(hh, 1, 2 * pr + 1, False, nxt=(0, 2 * pr + 2))))

    npairs = i // 2

    def two_pairs(t, carry):
        pair(2 * t)
        pair(2 * t + 1)
        return carry

    lax.fori_loop(0, npairs // 2, two_pairs, 0)

    @pl.when(npairs % 2 == 1)
    def _():
        pair(npairs - 1)

    def finalize(hh):
        lam = (jnp.exp(jnp.sum(lamv_ref[0:1, :] * lamv_ref[1:2, :], axis=1, keepdims=True))
               - jnp.exp(jnp.sum(lamv_ref[2:3, :] * lamv_ref[3:4, :], axis=1, keepdims=True))
               + LAMBDA_INIT)
        o = acc_scr[hh, :, 0:A_DV] / acc_scr[hh, :, A_DV:A_DV + 1]
        o = o[:tq] - lam * o[tq:]
        o_ref[:, cols_of(hh)] = (_rms(o) * g_ref[:, cols_of(hh)] * (1.0 - LAMBDA_INIT)).astype(BF16)
        yield

    def lag(n):
        for _ in range(n):
            yield

    @pl.when(i % 2 == 0)
    def _():
        run(lambda hh: chain(lag(hh), softmax_pv(hh, 0, i, True), finalize(hh)))

    @pl.when(i % 2 == 1)
    def _():
        run(lambda hh: chain(lag(hh), softmax_pv(hh, 0, i - 1, False, nxt=(1, i)),
                             softmax_pv(hh, 1, i, True), finalize(hh)))


def _attn(zx, zm, lamv, norm_g, *, tq):
    rows = zx.shape[0]
    nm = zm.shape[0]
    nh = ATT_HEADS_PER_STEP
    w = nh * A_DV
    return pl.pallas_call(
        _attn_kernel,
        out_shape=jax.ShapeDtypeStruct((rows, A_V), BF16),
        grid=(A_HEADS // nh, rows // tq),
        in_specs=[
            pl.BlockSpec((tq, w), lambda h, i: (i, COL_AQ // w + h)),
            pl.BlockSpec((rows, w), lambda h, i: (0, COL_AK // w + h)),
            pl.BlockSpec((rows, w), lambda h, i: (0, COL_AV // w + h)),
            pl.BlockSpec((nm, w), lambda h, i: (0, COL_AK // w + h)),
            pl.BlockSpec((nm, w), lambda h, i: (0, COL_AV // w + h)),
            pl.BlockSpec(lamv.shape, lambda h, i: (0, 0)),
            pl.BlockSpec((1, w), lambda h, i: (0, h)),
        ],
        out_specs=pl.BlockSpec((tq, w), lambda h, i: (i, h)),
        scratch_shapes=[
            pltpu.VMEM((nh, 2 * tq, A_DV), BF16),
            pltpu.VMEM((nh, 2 * tq, tq), F32),
            pltpu.VMEM((nh, 2 * tq, tq), F32),
            pltpu.VMEM((nh, 2 * tq, tq), BF16),
            pltpu.VMEM((nh, 2 * tq, tq), BF16),
            pltpu.VMEM((nh, 2 * tq, LANES), F32),
            pltpu.VMEM((nh, 2 * tq, LANES), F32),
            pltpu.VMEM((nh, 2 * tq, 2 * LANES), F32),
        ],
        compiler_params=_params(("parallel", "arbitrary")),
        name="diffattn",
    )(zx, zx, zx, zm, zm, lamv, norm_g)


def _mixout_kernel(ha_ref, gmga_ref, h_ref, mw_ref, aw_ref, wo_ref, gpost_ref,
                   zml_ref, gt_ref, zmlm_ref, gtm_ref, mpar_ref,
                   o_ref, hm_scr, hm_prev_scr, c_scr, m_scr, xbuf):
    s = pl.program_id(0)
    last = pl.num_programs(0) - 1
    w_qk = 2 * M_QK
    qk_ref = zml_ref.at[:, COL_QK:COL_QK + w_qk]
    mv_ref = zml_ref.at[:, COL_MV:COL_MV + M_V]
    mo_ref = zml_ref.at[:, COL_MO:COL_MO + M_V]
    qkm_ref = zmlm_ref.at[:, COL_QK:COL_QK + w_qk]
    mvm_ref = zmlm_ref.at[:, COL_MV:COL_MV + M_V]
    cw_ref = mpar_ref.at[0:CONV_W, :]
    cb_ref = mpar_ref.at[CONV_W:CONV_W + 1, :]
    ng_ref = mpar_ref.at[CONV_W + 1:CONV_W + 2, :]

    def merge():
        hm_prev_scr[...] = hm_scr[...]
        br_m = jnp.dot(hm_prev_scr[...], mw_ref[...], preferred_element_type=F32)
        yield
        br_a = jnp.dot(ha_ref[...], aw_ref[...], preferred_element_type=F32)
        yield
        d = h_ref.shape[1]
        y = (jax.nn.sigmoid(gmga_ref[:, 0:d].astype(F32)) * br_m
             + jax.nn.sigmoid(gmga_ref[:, d:2 * d].astype(F32)) * br_a)
        yield
        out = jnp.dot(y.astype(BF16), wo_ref[...], preferred_element_type=F32)
        yield
        o_ref[...] = h_ref[...] + _rms(out) * gpost_ref[...]

    def mlstm():
        return _mlstm_chunk(qk_ref, mv_ref, mo_ref, gt_ref, cw_ref, cb_ref, ng_ref, hm_scr,
                            c_scr, m_scr, xbuf)

    def run(*gens):
        gens = list(gens)
        while gens:
            for g in list(gens):
                try:
                    next(g)
                except StopIteration:
                    gens.remove(g)

    @pl.when(s == 0)
    def _():
        _mlstm_meta_init(qkm_ref, mvm_ref, gtm_ref, cw_ref, cb_ref, c_scr, m_scr, xbuf)
        run(mlstm())

    @pl.when(jnp.logical_and(s > 0, s < last))
    def _():
        run(merge(), mlstm())

    @pl.when(s == last)
    def _():
        run(merge())


def _mixout(zx, gx, zm, gm, ha, h, conv_w, conv_b, m_norm_g, mw, aw, wo, g_post, *, t):
    rows, d = h.shape
    nm = zm.shape[0]
    n = rows // t
    w_qk = 2 * M_QK
    w_ml = w_qk + 2 * M_V
    assert (COL_QK, COL_MV, COL_MO) == (0, w_qk, w_qk + M_V) and M_V == w_qk
    pad = jnp.zeros((8 - CONV_W - 2, w_qk), F32)
    mpar = jnp.concatenate([conv_w, conv_b, m_norm_g, pad], axis=0)
    const = lambda s: (0, 0)
    prev = lambda col: (lambda s: (jnp.maximum(s - 1, 0), col))
    cur = lambda col: (lambda s: (jnp.minimum(s, n - 1), col))
    return pl.pallas_call(
        _mixout_kernel,
        out_shape=jax.ShapeDtypeStruct((rows, d), F32),
        grid=(n + 1,),
        in_specs=[
            pl.BlockSpec((t, A_V), prev(0)),
            pl.BlockSpec((pl.Element(t), pl.Element(2 * d)),
                         lambda s: (pl.multiple_of(jnp.maximum(s - 1, 0) * t, t), COL_GM)),
            pl.BlockSpec((t, d), prev(0)),
            pl.BlockSpec((M_V, d), const),
            pl.BlockSpec((A_V, d), const),
            pl.BlockSpec((d, d), const),
            pl.BlockSpec((1, d), const),
            pl.BlockSpec((t, w_ml), cur(0)),
            pl.BlockSpec((t, LANES), cur(0)),
            pl.BlockSpec((nm, w_ml), const),
            pl.BlockSpec((nm, LANES), const),
            pl.BlockSpec(mpar.shape, const),
        ],
        out_specs=pl.BlockSpec((t, d), prev(0)),
        scratch_shapes=[
            pltpu.VMEM((t, M_V), BF16),
            pltpu.VMEM((t, M_V), BF16),
            pltpu.VMEM((M_HEADS, M_DQK, M_DV + LANES), F32),
            pltpu.VMEM((M_HEADS, 8, LANES), F32),
            pltpu.VMEM((t + CONV_PAD, w_qk), F32),
        ],
        compiler_params=_params(("arbitrary",)),
        name="mixout",
    )(ha, zx, h, mw, aw, wo, g_post, zx, gx, zm, gm, mpar)


def _rope_tables(n_rows):
    half = ROT_DIM // 2
    f32 = np.float32
    inv_freq = np.power(f32(ROPE_THETA), -np.arange(0, ROT_DIM, 2, dtype=f32) / f32(ROT_DIM))
    ang = np.arange(n_rows, dtype=f32)[:, None] * inv_freq[None, :]
    cos, sin = np.cos(ang).astype(f32), np.sin(ang).astype(f32)
    zeros = np.zeros((n_rows, A_DH - ROT_DIM), f32)
    zh = np.zeros((n_rows, half), f32)
    c = np.concatenate([cos, cos, zeros + f32(1.0)], axis=1)
    sa = np.concatenate([-sin, zh, zeros], axis=1)
    sb = np.concatenate([zh, sin, zeros], axis=1)
    return np.concatenate([np.concatenate([tab, tab], axis=1) for tab in (c, sa, sb)], axis=1)


def kernel(x, meta, ffn1_pre_g, ffn1_post_g, ffn1_w_gate, ffn1_w_up, ffn1_w_down, mix_pre_g, mix_post_g, w_in, b_in, m_conv_w, m_conv_b, m_norm_g, m_w_branch, a_lambda_q1, a_lambda_k1, a_lambda_q2, a_lambda_k2, a_norm_g, a_w_branch, w_out, ffn2_pre_g, ffn2_post_g, ffn2_w_gate, ffn2_w_up, ffn2_w_down):
    batch, seq, d = x.shape
    assert batch == 1 and meta.shape == (N_META, d)
    assert (2 * d) % A_QK == 0 and COL_GM % d == 0
    l = 0
    xr = x.reshape(seq, d)
    row = lambda v: v[l].reshape(1, -1)
    bf = lambda w: w[l].astype(BF16)

    w_in_l, b_in_l = w_in[l], b_in[l]
    gate_end = GATE_OFFSET + N_GATES
    w_in_t = w_in_l.T
    b_main = jnp.concatenate([b_in_l[:GATE_OFFSET], b_in_l[gate_end:]]).reshape(1, -1)
    b_gate = jnp.pad(b_in_l[GATE_OFFSET:gate_end], (0, LANES - N_GATES)).reshape(1, -1)
    rope = _rope_tables(N_META + seq)
    rope_m, rope_x = rope[:N_META], rope[N_META:]

    hx, hmeta = _ffn(xr, row(ffn1_pre_g), row(ffn1_post_g), ffn1_w_gate, ffn1_w_up, ffn1_w_down,
                     extra=meta, layer=l, tm=FFN_TM, tf=FFN_TF)
    zx, gx, zm, gm = _inproj(hx, hmeta, row(mix_pre_g), w_in_t, b_main, b_gate, rope_x, rope_m,
                             tm=1024)

    lamv = jnp.concatenate([row(a_lambda_q1), row(a_lambda_k1), row(a_lambda_q2),
                            row(a_lambda_k2)], axis=0)
    ha = _attn(zx, zm, lamv, row(a_norm_g), tq=512)
    h2 = _mixout(zx, gx, zm, gm, ha, hx, m_conv_w[l], row(m_conv_b), row(m_norm_g),
                 bf(m_w_branch), bf(a_w_branch), bf(w_out), row(mix_post_g), t=256)
    out = _ffn(h2, row(ffn2_pre_g), row(ffn2_post_g), ffn2_w_gate, ffn2_w_up, ffn2_w_down,
               layer=l, tm=FFN_TM, tf=FFN_TF)
    return out.reshape(batch, seq, d)
```

```python
import functools

import jax
import jax.numpy as jnp
import numpy as np
from jax import lax
from jax.experimental import pallas as pl
from jax.experimental.pallas import tpu as pltpu

F32 = jnp.float32
BF16 = jnp.bfloat16

EPS = 1e-6
N_META = 16
M_HEADS = 4
M_DQK = 128
M_DV = 256
M_QK = M_HEADS * M_DQK
M_V = M_HEADS * M_DV
CONV_W = 4
A_HEADS = 8
A_DH = 64
A_DV = 2 * A_DH
A_QK = A_HEADS * 2 * A_DH
A_V = A_HEADS * A_DV
ROT_DIM = A_DH // 4
ROPE_THETA = 500000.0
LAMBDA_INIT = 0.8 - 0.6 * 1.0
Q_SCALE = A_DH ** -0.5 * 1.4426950408889634

LANES = 128
GATE_OFFSET = 2 * M_QK + 2 * M_V
N_GATES = 2 * M_HEADS
COL_QK = 0
COL_MV = 2 * M_QK
COL_MO = COL_MV + M_V
COL_AQ = COL_MO + M_V
COL_AK = COL_AQ + A_QK
COL_AV = COL_AK + A_QK
COL_GM = COL_AV + A_V

MIB = 1024 * 1024
VMEM_PHYSICAL = 64 * MIB
VMEM_LIMIT = 56 * MIB
FFN_TM = 1024
FFN_TF = 256
FFN_TF_BF16 = 512
FFN_VMEM_LIMIT = 62 * MIB
INPROJ_TM = 1024
INPROJ_WBUFS = 3
INPROJ_VMEM_LIMIT = 62 * MIB
ATT_TQ = 512
ATT_HEADS_PER_STEP = 2
ATT_STRIP = 64
ATT_GROUP = 256
MIX_T = 256
NORM_STRIP = 16
ROPE_GROUP = 256
CONV_PAD = 8


def _params(sem, vmem_limit=VMEM_LIMIT):
    assert vmem_limit < VMEM_PHYSICAL
    return pltpu.CompilerParams(dimension_semantics=sem, vmem_limit_bytes=vmem_limit)


def _rms(x):
    return x * lax.rsqrt(jnp.mean(x * x, axis=-1, keepdims=True) + EPS)


def _ffn_kernel(h_ref, gpre_ref, gpost_ref, wg_ref, wu_ref, wd_ref, *rest, n_extra):
    if n_extra:
        x_ref, o_ref, ox_ref, u_scr, accx_scr = rest
    else:
        o_ref, u_scr = rest
    i = pl.program_id(0)
    f = pl.program_id(1)
    last = pl.num_programs(1) - 1
    tm = h_ref.shape[0]

    @pl.when(f == 0)
    def _():
        for r0 in range(0, tm, NORM_STRIP):
            rs = slice(r0, r0 + NORM_STRIP)
            u_scr[rs, :] = (_rms(h_ref[rs, :]) * gpre_ref[...]).astype(BF16)
        o_ref[...] = jnp.zeros_like(o_ref)

    def swiglu_down(u):
        g = jnp.dot(u, wg_ref[...].astype(BF16), preferred_element_type=F32)
        up = jnp.dot(u, wu_ref[...].astype(BF16), preferred_element_type=F32)
        a = (g * jax.nn.sigmoid(g) * up).astype(BF16)
        return jnp.dot(a, wd_ref[...].astype(BF16), preferred_element_type=F32)

    if n_extra:
        @pl.when(jnp.logical_and(i == 0, f == 0))
        def _():
            u_scr[tm:tm + n_extra, :] = (_rms(x_ref[...]) * gpre_ref[...]).astype(BF16)
            accx_scr[...] = jnp.zeros_like(accx_scr)

        @pl.when(i == 0)
        def _():
            down = swiglu_down(u_scr[...])
            o_ref[...] += down[0:tm]
            accx_scr[...] += down[tm:tm + n_extra]

        @pl.when(i > 0)
        def _():
            o_ref[...] += swiglu_down(u_scr[0:tm, :])

        @pl.when(jnp.logical_and(i == 0, f == last))
        def _():
            ox_ref[...] = x_ref[...] + 0.5 * (_rms(accx_scr[...]) * gpost_ref[...])
    else:
        o_ref[...] += swiglu_down(u_scr[...])

    @pl.when(f == last)
    def _():
        for r0 in range(0, tm, NORM_STRIP):
            rs = slice(r0, r0 + NORM_STRIP)
            o_ref[rs, :] = h_ref[rs, :] + 0.5 * (_rms(o_ref[rs, :]) * gpost_ref[...])


def _ffn(h, g_pre, g_post, w_gate, w_up, w_down, extra=None, *, layer, tm, tf):
    rows, d = h.shape
    d_ff = w_gate.shape[2]
    n_extra = 0 if extra is None else extra.shape[0]
    vec = pl.BlockSpec((1, d), lambda i, f: (0, 0))
    tile = pl.BlockSpec((tm, d), lambda i, f: (i, 0))
    in_specs = [tile, vec, vec,
                pl.BlockSpec((None, d, tf), lambda i, f: (layer, 0, f)),
                pl.BlockSpec((None, d, tf), lambda i, f: (layer, 0, f)),
                pl.BlockSpec((None, tf, d), lambda i, f: (layer, f, 0))]
    args = [h, g_pre, g_post, w_gate, w_up, w_down]
    out_shape = jax.ShapeDtypeStruct((rows, d), F32)
    out_specs = tile
    scratch = [pltpu.VMEM((tm + n_extra, d), BF16)]
    if n_extra:
        whole = pl.BlockSpec((n_extra, d), lambda i, f: (0, 0))
        in_specs.append(whole)
        args.append(extra)
        out_shape = (out_shape, jax.ShapeDtypeStruct((n_extra, d), F32))
        out_specs = (tile, whole)
        scratch.append(pltpu.VMEM((n_extra, d), F32))
    return pl.pallas_call(
        functools.partial(_ffn_kernel, n_extra=n_extra),
        out_shape=out_shape,
        grid=(rows // tm, d_ff // tf),
        in_specs=in_specs,
        out_specs=out_specs,
        scratch_shapes=scratch,
        compiler_params=_params(("arbitrary", "arbitrary"), vmem_limit=FFN_VMEM_LIMIT),
        name="ffn",
    )(*args)


def _rope(z, c, sa, sb):
    outs = []
    for grp in range(z.shape[1] // LANES):
        x = z[:, grp * LANES:(grp + 1) * LANES]
        outs.append(x * c + pltpu.roll(x, LANES - ROT_DIM // 2, 1) * sa
                    + pltpu.roll(x, ROT_DIM // 2, 1) * sb)
    return jnp.concatenate(outs, axis=1)


_NT = (((1,), (1,)), ((), ()))


def _inproj_kernel(h_ref, g_ref, w_hbm, b_ref, wgate_ref, bgate_ref, rope_ref,
                   xh_ref, xrope_ref, z_ref, gate_ref, zx_ref, gatex_ref, u_scr, wbuf, wsem,
                   *, tn):
    i = pl.program_id(0)
    j = pl.program_id(1)
    nj = pl.num_programs(1)
    tm = h_ref.shape[0]
    nx = xh_ref.shape[0]
    n_a = GATE_OFFSET // tn
    step = i * nj + j
    total = pl.num_programs(0) * nj

    def tile_copy(t):
        jt = t % nj
        off = pl.multiple_of(jnp.where(jt < n_a, jt * tn, jt * tn + N_GATES), N_GATES)
        slot = t % INPROJ_WBUFS
        return pltpu.make_async_copy(w_hbm.at[pl.ds(off, tn), :], wbuf.at[slot], wsem.at[slot])

    @pl.when(step == 0)
    def _():
        for t in range(INPROJ_WBUFS - 1):
            tile_copy(t).start()

    @pl.when(step + INPROJ_WBUFS - 1 < total)
    def _():
        tile_copy(step + INPROJ_WBUFS - 1).start()

    tile_copy(step).wait()
    wt_ref = wbuf.at[step % INPROJ_WBUFS]

    def gates(u):
        wg = wgate_ref[...].astype(BF16)
        wg = jnp.concatenate([wg, jnp.zeros((LANES - N_GATES, wg.shape[1]), BF16)], axis=0)
        return lax.dot_general(u, wg, _NT, preferred_element_type=F32) + bgate_ref[...]

    @pl.when(j == 0)
    def _():
        for r0 in range(0, tm, NORM_STRIP):
            rs = slice(r0, r0 + NORM_STRIP)
            u_scr[rs, :] = (_rms(h_ref[rs, :]) * g_ref[...]).astype(BF16)
        gate_ref[...] = gates(u_scr[0:tm, :])

    @pl.when(jnp.logical_and(i == 0, j == 0))
    def _():
        u = (_rms(xh_ref[...]) * g_ref[...]).astype(BF16)
        u_scr[tm:tm + nx, :] = u
        gatex_ref[...] = gates(u)

    def finish(z, kind, tab):
        if kind == "plain":
            return z.astype(BF16)
        z = _rope(z, tab[:, 0:LANES], tab[:, LANES:2 * LANES], tab[:, 2 * LANES:3 * LANES])
        if kind == "q":
            z = z * Q_SCALE
        return z.astype(BF16)

    def tile_out(kind, with_extra):
        w = wt_ref[...].astype(BF16)
        if kind != "plain" and tm % ROPE_GROUP == 0:
            for r0 in range(0, tm, ROPE_GROUP):
                rs = slice(r0, r0 + ROPE_GROUP)
                z = lax.dot_general(u_scr[rs, :], w, _NT, preferred_element_type=F32) + b_ref[...]
                z_ref[rs, :] = finish(z, kind, rope_ref.at[rs, :])
            if with_extra:
                z = (lax.dot_general(u_scr[tm:tm + nx, :], w, _NT, preferred_element_type=F32)
                     + b_ref[...])
                zx_ref[...] = finish(z, kind, xrope_ref)
            return
        u = u_scr[...] if with_extra else u_scr[0:tm, :]
        z = lax.dot_general(u, w, _NT, preferred_element_type=F32) + b_ref[...]
        z_ref[...] = finish(z[0:tm], kind, rope_ref)
        if with_extra:
            zx_ref[...] = finish(z[tm:tm + nx], kind, xrope_ref)

    is_q = j == COL_AQ // tn
    is_k = j == COL_AK // tn
    plain = jnp.logical_not(jnp.logical_or(is_q, is_k))
    used = jnp.logical_or(j < COL_MO // tn,
                          jnp.logical_and(j >= COL_AK // tn, j < COL_GM // tn))
    extra = jnp.logical_and(i == 0, used)

    def tile_no_extra(kind):
        tile_out(kind, False)

        @pl.when(i == 0)
        def _():
            zx_ref[...] = jnp.zeros_like(zx_ref)

    pl.when(is_q)(functools.partial(tile_no_extra, "q"))
    for kind, cond in (("k", is_k), ("plain", plain)):
        pl.when(jnp.logical_and(cond, extra))(functools.partial(tile_out, kind, True))
        pl.when(jnp.logical_and(cond, jnp.logical_not(extra)))(
            functools.partial(tile_no_extra, kind))


def _inproj(h, xh, g, w_t, b_main, b_gate, rope, rope_x, *, tm):
    rows, d = h.shape
    nx = xh.shape[0]
    tn = A_QK
    n_a = GATE_OFFSET // tn
    n = w_t.shape[0] - N_GATES
    nj = n // tn
    assert COL_AQ == GATE_OFFSET and GATE_OFFSET % tn == 0 and n % tn == 0
    assert (rows // tm) * nj >= INPROJ_WBUFS

    table = pl.BlockSpec((tm, 3 * LANES), lambda i, j: (i, 0))
    table_x = pl.BlockSpec((nx, 3 * LANES), lambda i, j: (0, 0))
    return pl.pallas_call(
        functools.partial(_inproj_kernel, tn=tn),
        out_shape=(jax.ShapeDtypeStruct((rows, n), BF16),
                   jax.ShapeDtypeStruct((rows, LANES), F32),
                   jax.ShapeDtypeStruct((nx, n), BF16),
                   jax.ShapeDtypeStruct((nx, LANES), F32)),
        grid=(rows // tm, nj),
        in_specs=[
            pl.BlockSpec((tm, d), lambda i, j: (i, 0)),
            pl.BlockSpec((1, d), lambda i, j: (0, 0)),
            pl.BlockSpec(memory_space=pl.ANY),
            pl.BlockSpec((1, tn), lambda i, j: (0, j)),
            pl.BlockSpec((N_GATES, d), lambda i, j: (GATE_OFFSET // N_GATES, 0)),
            pl.BlockSpec((1, LANES), lambda i, j: (0, 0)),
            table,
            pl.BlockSpec((nx, d), lambda i, j: (0, 0)),
            table_x,
        ],
        out_specs=(pl.BlockSpec((tm, tn), lambda i, j: (i, j)),
                   pl.BlockSpec((tm, LANES), lambda i, j: (i, 0)),
                   pl.BlockSpec((nx, tn), lambda i, j: (0, jnp.where(i == 0, j, nj - 1))),
                   pl.BlockSpec((nx, LANES), lambda i, j: (0, 0))),
        scratch_shapes=[pltpu.VMEM((tm + nx, d), BF16),
                        pltpu.VMEM((INPROJ_WBUFS, tn, d), w_t.dtype),
                        pltpu.SemaphoreType.DMA((INPROJ_WBUFS,))],
        compiler_params=_params(("arbitrary", "arbitrary"), vmem_limit=INPROJ_VMEM_LIMIT),
        name="inproj",
    )(h, g, w_t, b_main, w_t, b_gate, rope, xh, rope_x)


def _conv_silu(xbuf, n, cw_ref, cb_ref):
    y = cb_ref[...]
    for i in range(CONV_W):
        start = CONV_PAD + i - (CONV_W - 1)
        y = y + cw_ref[i:i + 1, :] * xbuf[start:start + n, :]
    return y * jax.nn.sigmoid(y)


def _log_sigmoid(x):
    return jnp.minimum(x, 0.0) - jnp.log1p(jnp.exp(-jnp.abs(x)))


def _cumsum_rows(x):
    n = x.shape[0]
    r = lax.broadcasted_iota(jnp.int32, (n, n), 0)
    c = lax.broadcasted_iota(jnp.int32, (n, n), 1)
    tril = (r >= c).astype(F32)
    return jnp.dot(tril, x, preferred_element_type=F32, precision=lax.Precision.HIGHEST)


def _state_update(c_scr, m_scr, hd, kf, vext, li_col, b_col, g_tot, m_prev):
    log_w = (g_tot - b_col) + li_col
    m_new = jnp.maximum(g_tot + m_prev, jnp.max(log_w, axis=0, keepdims=True))
    wk = jnp.exp(log_w - m_new)
    decay = jnp.exp(g_tot + m_prev - m_new)
    kw = (kf * wk).astype(BF16)
    upd = lax.dot_general(kw, vext, (((0,), (0,)), ((), ())), preferred_element_type=F32)
    c_scr[hd] = decay * c_scr[hd] + upd
    m_scr[hd] = jnp.broadcast_to(m_new, m_scr.shape[1:])


M_KSCALE = M_DQK ** -0.5


def _mlstm_meta_init(qkm_ref, mvm_ref, gtm_ref, cw_ref, cb_ref, c_scr, m_scr, xbuf):
    nm = qkm_ref.shape[0]
    c_scr[...] = jnp.zeros_like(c_scr)
    m_scr[...] = jnp.zeros_like(m_scr)
    xbuf[0:CONV_PAD, :] = jnp.zeros((CONV_PAD, xbuf.shape[1]), F32)
    xbuf[CONV_PAD:CONV_PAD + nm, :] = qkm_ref[...].astype(F32)
    qk = _conv_silu(xbuf, nm, cw_ref, cb_ref)
    xbuf[0:CONV_PAD, :] = xbuf[nm:nm + CONV_PAD, :]
    gts = gtm_ref[...]
    b_all = _cumsum_rows(_log_sigmoid(gts))
    ones = jnp.ones((nm, LANES), BF16)
    for hd in range(M_HEADS):
        kf = qk[:, M_QK + hd * M_DQK:M_QK + (hd + 1) * M_DQK] * M_KSCALE
        vext = jnp.concatenate([mvm_ref[:, hd * M_DV:(hd + 1) * M_DV], ones], axis=1)
        b_col = b_all[:, M_HEADS + hd:M_HEADS + hd + 1]
        _state_update(c_scr, m_scr, hd, kf, vext, gts[:, hd:hd + 1], b_col,
                      b_col[nm - 1:nm, :], m_scr[hd][0:1, 0:1])


def _mlstm_chunk(qk_ref, mv_ref, mo_ref, gt_ref, cw_ref, cb_ref, ng_ref, o_ref, c_scr, m_scr, xbuf):
    t = qk_ref.shape[0]
    kscale = M_KSCALE
    xbuf[CONV_PAD:CONV_PAD + t, :] = qk_ref[...].astype(F32)
    qk = _conv_silu(xbuf, t, cw_ref, cb_ref)
    xbuf[0:CONV_PAD, :] = xbuf[t:t + CONV_PAD, :]

    gts = gt_ref[...]
    b_all = _cumsum_rows(_log_sigmoid(gts))
    lane = lax.broadcasted_iota(jnp.int32, gts.shape, 1)
    rowform = jnp.where(lane < M_HEADS, gts, b_all).T
    r = lax.broadcasted_iota(jnp.int32, (t, t), 0)
    c = lax.broadcasted_iota(jnp.int32, (t, t), 1)
    causal = r >= c
    ones = jnp.ones((t, LANES), BF16)

    for hd in range(M_HEADS):
        yield
        qh = qk[:, hd * M_DQK:(hd + 1) * M_DQK].astype(BF16)
        kf = qk[:, M_QK + hd * M_DQK:M_QK + (hd + 1) * M_DQK] * kscale
        kh = kf.astype(BF16)
        vext = jnp.concatenate([mv_ref[:, hd * M_DV:(hd + 1) * M_DV], ones], axis=1)
        li_col = gts[:, hd:hd + 1]
        b_col = b_all[:, M_HEADS + hd:M_HEADS + hd + 1]
        li_row = rowform[hd:hd + 1, :]
        b_row = rowform[M_HEADS + hd:M_HEADS + hd + 1, :]
        g_tot = b_col[t - 1:t, :]
        m_prev = m_scr[hd][0:1, 0:1]

        log_d = jnp.where(causal, (b_col - b_row) + li_row, -jnp.inf)
        m_inter = b_col + m_prev
        m_row = jnp.maximum(m_inter, jnp.max(log_d, axis=1, keepdims=True))
        s = lax.dot_general(qh, kh, (((1,), (1,)), ((), ())), preferred_element_type=F32)
        sd = (s * jnp.exp(log_d - m_row)).astype(BF16)
        inter = jnp.dot(qh, c_scr[hd].astype(BF16), preferred_element_type=F32)
        numden = jnp.exp(m_inter - m_row) * inter + jnp.dot(sd, vext, preferred_element_type=F32)
        den = jnp.maximum(jnp.abs(numden[:, M_DV:M_DV + 1]), jnp.exp(-m_row))
        hh = numden[:, :M_DV] / den
        hn = _rms(hh) * ng_ref[:, hd * M_DV:(hd + 1) * M_DV]
        og = jax.nn.sigmoid(mo_ref[:, hd * M_DV:(hd + 1) * M_DV].astype(F32))
        o_ref[:, hd * M_DV:(hd + 1) * M_DV] = (og * hn).astype(BF16)

        _state_update(c_scr, m_scr, hd, kf, vext, li_col, b_col, g_tot, m_prev)


ATT_CHUNK = 64


def _attn_kernel(q_ref, k_ref, v_ref, km_ref, vm_ref, lamv_ref, g_ref, *rest):
    n_w = (len(rest) - 9) // 2
    w_refs, o_ref, wb_refs = rest[:n_w], rest[n_w], rest[n_w + 1:2 * n_w + 1]
    q2_scr, s0_scr, s1_scr, p0_scr, p1_scr, m_scr, alpha_scr, acc_scr = rest[2 * n_w + 1:]
    for w_ref, wb_ref in zip(w_refs, wb_refs):
        wb_ref[...] = w_ref[...].astype(BF16)
    _attn_body(q_ref, k_ref, v_ref, km_ref, vm_ref, lamv_ref, g_ref, o_ref,
               q2_scr, s0_scr, s1_scr, p0_scr, p1_scr, m_scr, alpha_scr, acc_scr)


def _attn_body(q_ref, k_ref, v_ref, km_ref, vm_ref, lamv_ref,
               g_ref, o_ref, q2_scr, s0_scr, s1_scr, p0_scr, p1_scr, m_scr, alpha_scr, acc_scr):
    i = pl.program_id(1)
    tq = q_ref.shape[0]
    tk = tq
    rows = 2 * tq
    ncol = tk // LANES
    nheads = q_ref.shape[1] // A_DV
    nt = (((1,), (1,)), ((), ()))
    ones_k = jnp.ones((tk, LANES), BF16)
    s_slots = (s0_scr, s1_scr)
    p_slots = (p0_scr, p1_scr)

    def cols_of(hh):
        return slice(hh * A_DV, (hh + 1) * A_DV)

    def kblock(hh, blk):
        off = pl.multiple_of(blk * tk, tk)
        return k_ref[pl.ds(off, tk), cols_of(hh)]

    def softmax_pv(hh, slot, blk, masked, nxt=None):
        s_ref, p_ref = s_slots[slot].at[hh], p_slots[slot].at[hh]
        m_ref, alpha_ref, acc_ref = m_scr.at[hh], alpha_scr.at[hh], acc_scr.at[hh]
        off = pl.multiple_of(blk * tk, tk)
        vext = jnp.concatenate([v_ref[pl.ds(off, tk), cols_of(hh)], ones_k], axis=1)
        for g0 in range(0, rows, ATT_GROUP):
            gs = slice(g0, g0 + ATT_GROUP)
            if nxt is not None:
                nxt_slot, nxt_blk = nxt
                s_slots[nxt_slot].at[hh][gs, :] = lax.dot_general(
                    q2_scr.at[hh][gs, :], kblock(hh, nxt_blk), nt, preferred_element_type=F32)
            softmax_rows(s_ref, p_ref, m_ref, alpha_ref, masked, g0)
            pv = jnp.dot(p_ref[gs, :], vext, preferred_element_type=F32)
            alpha = alpha_ref[gs, :]
            for c in range(2):
                sl = slice(c * LANES, (c + 1) * LANES)
                acc_ref[gs, sl] = alpha * acc_ref[gs, sl] + pv[:, sl]
            yield

    def softmax_rows(s_ref, p_ref, m_scr, alpha_scr, masked, g0):
        for st in range(g0 // ATT_STRIP, (g0 + ATT_GROUP) // ATT_STRIP):
            rs = slice(st * ATT_STRIP, (st + 1) * ATT_STRIP)
            if masked:
                qchunk = ((st * ATT_STRIP) % tq) // ATT_CHUNK
                nvis = (qchunk + 1) * ATT_CHUNK
                cvis = lax.broadcasted_iota(jnp.int32, (ATT_STRIP, LANES), 1)
                cols = []
                for c in range(-(-nvis // LANES)):
                    sc = s_ref[rs, c * LANES:(c + 1) * LANES]
                    if (c + 1) * LANES > nvis:
                        sc = jnp.where(cvis < nvis - c * LANES, sc, -jnp.inf)
                    cols.append(sc)
            else:
                cols = [s_ref[rs, c * LANES:(c + 1) * LANES] for c in range(ncol)]
            part = cols[0]
            for sc in cols[1:]:
                part = jnp.maximum(part, sc)
            m_old = m_scr[rs, :]
            m_new = jnp.maximum(m_old, jnp.max(part, axis=1, keepdims=True))
            m_scr[rs, :] = m_new
            alpha_scr[rs, :] = jnp.exp2(m_old - m_new)
            if not masked:
                cols = [s_ref[rs, c * LANES:(c + 1) * LANES] for c in range(ncol)]
            pcols = [jnp.exp2(sc - m_new) for sc in cols]
            pcols += [jnp.zeros_like(pcols[0])] * (ncol - len(cols))
            p_ref[rs, :] = jnp.concatenate(pcols, axis=1).astype(BF16)

    def prologue(hh):
        q = q_ref[:, cols_of(hh)]
        lane = lax.broadcasted_iota(jnp.int32, q.shape, 1)
        zero = jnp.zeros_like(q)
        q2 = q2_scr.at[hh]
        q2[0:tq, :] = jnp.where(lane < A_DH, q, zero)
        q2[tq:rows, :] = jnp.where(lane >= A_DH, q, zero)
        yield
        s = lax.dot_general(q2[...], km_ref[:, cols_of(hh)], nt, preferred_element_type=F32)
        m0 = jnp.max(s, axis=1, keepdims=True)
        p = jnp.exp2(s - m0).astype(BF16)
        vext = jnp.concatenate([vm_ref[:, cols_of(hh)],
                                jnp.ones((vm_ref.shape[0], LANES), BF16)], axis=1)
        acc_scr[hh] = jnp.dot(p, vext, preferred_element_type=F32)
        m_scr[hh] = jnp.broadcast_to(m0, m_scr.shape[1:])
        yield
        s0_scr[hh] = lax.dot_general(q2[...], kblock(hh, 0), nt, preferred_element_type=F32)

    def chain(*gens):
        for g in gens:
            yield from g

    def run(make):
        gens = [make(hh) for hh in range(nheads)]
        while gens:
            for g in list(gens):
                try:
                    next(g)
                except StopIteration:
                    gens.remove(g)

    run(prologue)

    def pair(pr):
        run(lambda hh: chain(softmax_pv(hh, 0, 2 * pr, False, nxt=(1, 2 * pr + 1)),
                             softmax_pv(hh, 1, 2 * pr + 1, False, nxt=(0, 2 * pr + 2))))

    npairs = i // 2

    def two_pairs(t, carry):
        pair(2 * t)
        pair(2 * t + 1)
        return carry

    lax.fori_loop(0, npairs // 2, two_pairs, 0)

    @pl.when(npairs % 2 == 1)
    def _():
        pair(npairs - 1)

    def finalize(hh):
        lam = (jnp.exp(jnp.sum(lamv_ref[0:1, :] * lamv_ref[1:2, :], axis=1, keepdims=True))
               - jnp.exp(jnp.sum(lamv_ref[2:3, :] * lamv_ref[3:4, :], axis=1, keepdims=True))
               + LAMBDA_INIT)
        o = acc_scr[hh, :, 0:A_DV] / acc_scr[hh, :, A_DV:A_DV + 1]
        o = o[:tq] - lam * o[tq:]
        o_ref[:, cols_of(hh)] = (_rms(o) * g_ref[:, cols_of(hh)] * (1.0 - LAMBDA_INIT)).astype(BF16)
        yield

    def lag(n):
        for _ in range(n):
            yield

    @pl.when(i % 2 == 0)
    def _():
        run(lambda hh: chain(lag(hh), softmax_pv(hh, 0, i, True), finalize(hh)))

    @pl.when(i % 2 == 1)
    def _():
        run(lambda hh: chain(lag(hh), softmax_pv(hh, 0, i - 1, False, nxt=(1, i)),
                             softmax_pv(hh, 1, i, True), finalize(hh)))


def _attn(zx, zm, lamv, norm_g, weights=(), *, layer, tq):
    rows = zx.shape[0]
    nm = zm.shape[0]
    nh = ATT_HEADS_PER_STEP
    w = nh * A_DV
    nq = rows // tq
    nsteps = (A_HEADS // nh) * nq
    w_specs, wb_specs, wb_shapes = [], [], []
    for wt in weights:
        r, c = wt.shape[1:]
        per = next(p for p in (1, 2, 4, 8) if r % (nsteps // p) == 0
                   and (r // (nsteps // p)) % 16 == 0)
        rb = r // (nsteps // per)
        w_specs.append(pl.BlockSpec((None, rb, c),
                                    lambda h, i, per=per: (layer, (h * nq + i) // per, 0)))
        wb_specs.append(pl.BlockSpec((rb, c), lambda h, i, per=per: ((h * nq + i) // per, 0)))
        wb_shapes.append(jax.ShapeDtypeStruct((r, c), BF16))
    return pl.pallas_call(
        _attn_kernel,
        out_shape=(jax.ShapeDtypeStruct((rows, A_V), BF16), *wb_shapes),
        grid=(A_HEADS // nh, nq),
        in_specs=[
            pl.BlockSpec((tq, w), lambda h, i: (i, COL_AQ // w + h)),
            pl.BlockSpec((rows, w), lambda h, i: (0, COL_AK // w + h)),
            pl.BlockSpec((rows, w), lambda h, i: (0, COL_AV // w + h)),
            pl.BlockSpec((nm, w), lambda h, i: (0, COL_AK // w + h)),
            pl.BlockSpec((nm, w), lambda h, i: (0, COL_AV // w + h)),
            pl.BlockSpec(lamv.shape, lambda h, i: (0, 0)),
            pl.BlockSpec((1, w), lambda h, i: (0, h)),
            *w_specs,
        ],
        out_specs=(pl.BlockSpec((tq, w), lambda h, i: (i, h)), *wb_specs),
        scratch_shapes=[
            pltpu.VMEM((nh, 2 * tq, A_DV), BF16),
            pltpu.VMEM((nh, 2 * tq, tq), F32),
            pltpu.VMEM((nh, 2 * tq, tq), F32),
            pltpu.VMEM((nh, 2 * tq, tq), BF16),
            pltpu.VMEM((nh, 2 * tq, tq), BF16),
            pltpu.VMEM((nh, 2 * tq, LANES), F32),
            pltpu.VMEM((nh, 2 * tq, LANES), F32),
            pltpu.VMEM((nh, 2 * tq, 2 * LANES), F32),
        ],
        compiler_params=_params(("parallel", "arbitrary")),
        name="diffattn",
    )(zx, zx, zx, zm, zm, lamv, norm_g, *weights)


def _mixout_kernel(ha_ref, gmga_ref, h_ref, mw_ref, aw_ref, wo_ref, gpost_ref,
                   zml_ref, gt_ref, zmlm_ref, gtm_ref, mpar_ref,
                   o_ref, hm_scr, hm_prev_scr, c_scr, m_scr, xbuf):
    s = pl.program_id(0)
    last = pl.num_programs(0) - 1
    w_qk = 2 * M_QK
    qk_ref = zml_ref.at[:, COL_QK:COL_QK + w_qk]
    mv_ref = zml_ref.at[:, COL_MV:COL_MV + M_V]
    mo_ref = zml_ref.at[:, COL_MO:COL_MO + M_V]
    qkm_ref = zmlm_ref.at[:, COL_QK:COL_QK + w_qk]
    mvm_ref = zmlm_ref.at[:, COL_MV:COL_MV + M_V]
    cw_ref = mpar_ref.at[0:CONV_W, :]
    cb_ref = mpar_ref.at[CONV_W:CONV_W + 1, :]
    ng_ref = mpar_ref.at[CONV_W + 1:CONV_W + 2, :]

    def merge():
        hm_prev_scr[...] = hm_scr[...]
        br_m = jnp.dot(hm_prev_scr[...], mw_ref[...], preferred_element_type=F32)
        yield
        br_a = jnp.dot(ha_ref[...], aw_ref[...], preferred_element_type=F32)
        yield
        d = h_ref.shape[1]
        y = (jax.nn.sigmoid(gmga_ref[:, 0:d].astype(F32)) * br_m
             + jax.nn.sigmoid(gmga_ref[:, d:2 * d].astype(F32)) * br_a)
        yield
        out = jnp.dot(y.astype(BF16), wo_ref[...], preferred_element_type=F32)
        yield
        o_ref[...] = h_ref[...] + _rms(out) * gpost_ref[...]

    def mlstm():
        return _mlstm_chunk(qk_ref, mv_ref, mo_ref, gt_ref, cw_ref, cb_ref, ng_ref, hm_scr,
                            c_scr, m_scr, xbuf)

    def run(*gens):
        gens = list(gens)
        while gens:
            for g in list(gens):
                try:
                    next(g)
                except StopIteration:
                    gens.remove(g)

    @pl.when(s == 0)
    def _():
        _mlstm_meta_init(qkm_ref, mvm_ref, gtm_ref, cw_ref, cb_ref, c_scr, m_scr, xbuf)
        run(mlstm())

    @pl.when(jnp.logical_and(s > 0, s < last))
    def _():
        run(merge(), mlstm())

    @pl.when(s == last)
    def _():
        run(merge())


def _mixout(zx, gx, zm, gm, ha, h, conv_w, conv_b, m_norm_g, mw, aw, wo, g_post, *, t):
    rows, d = h.shape
    nm = zm.shape[0]
    n = rows // t
    w_qk = 2 * M_QK
    w_ml = w_qk + 2 * M_V
    assert (COL_QK, COL_MV, COL_MO) == (0, w_qk, w_qk + M_V) and M_V == w_qk
    pad = jnp.zeros((8 - CONV_W - 2, w_qk), F32)
    mpar = jnp.concatenate([conv_w, conv_b, m_norm_g, pad], axis=0)
    const = lambda s: (0, 0)
    prev = lambda col: (lambda s: (jnp.maximum(s - 1, 0), col))
    cur = lambda col: (lambda s: (jnp.minimum(s, n - 1), col))
    return pl.pallas_call(
        _mixout_kernel,
        out_shape=jax.ShapeDtypeStruct((rows, d), F32),
        grid=(n + 1,),
        in_specs=[
            pl.BlockSpec((t, A_V), prev(0)),
            pl.BlockSpec((pl.Element(t), pl.Element(2 * d)),
                         lambda s: (pl.multiple_of(jnp.maximum(s - 1, 0) * t, t), COL_GM)),
            pl.BlockSpec((t, d), prev(0)),
            pl.BlockSpec((M_V, d), const),
            pl.BlockSpec((A_V, d), const),
            pl.BlockSpec((d, d), const),
            pl.BlockSpec((1, d), const),
            pl.BlockSpec((t, w_ml), cur(0)),
            pl.BlockSpec((t, LANES), cur(0)),
            pl.BlockSpec((nm, w_ml), const),
            pl.BlockSpec((nm, LANES), const),
            pl.BlockSpec(mpar.shape, const),
        ],
        out_specs=pl.BlockSpec((t, d), prev(0)),
        scratch_shapes=[
            pltpu.VMEM((t, M_V), BF16),
            pltpu.VMEM((t, M_V), BF16),
            pltpu.VMEM((M_HEADS, M_DQK, M_DV + LANES), F32),
            pltpu.VMEM((M_HEADS, 8, LANES), F32),
            pltpu.VMEM((t + CONV_PAD, w_qk), F32),
        ],
        compiler_params=_params(("arbitrary",)),
        name="mixout",
    )(ha, zx, h, mw, aw, wo, g_post, zx, gx, zm, gm, mpar)


def _rope_tables(n_rows):
    half = ROT_DIM // 2
    f32 = np.float32
    inv_freq = np.power(f32(ROPE_THETA), -np.arange(0, ROT_DIM, 2, dtype=f32) / f32(ROT_DIM))
    ang = np.arange(n_rows, dtype=f32)[:, None] * inv_freq[None, :]
    cos, sin = np.cos(ang).astype(f32), np.sin(ang).astype(f32)
    zeros = np.zeros((n_rows, A_DH - ROT_DIM), f32)
    zh = np.zeros((n_rows, half), f32)
    c = np.concatenate([cos, cos, zeros + f32(1.0)], axis=1)
    sa = np.concatenate([-sin, zh, zeros], axis=1)
    sb = np.concatenate([zh, sin, zeros], axis=1)
    return np.concatenate([np.concatenate([tab, tab], axis=1) for tab in (c, sa, sb)], axis=1)


def kernel(x, meta, ffn1_pre_g, ffn1_post_g, ffn1_w_gate, ffn1_w_up, ffn1_w_down, mix_pre_g, mix_post_g, w_in, b_in, m_conv_w, m_conv_b, m_norm_g, m_w_branch, a_lambda_q1, a_lambda_k1, a_lambda_q2, a_lambda_k2, a_norm_g, a_w_branch, w_out, ffn2_pre_g, ffn2_post_g, ffn2_w_gate, ffn2_w_up, ffn2_w_down):
    batch, seq, d = x.shape
    assert batch == 1 and meta.shape == (N_META, d) and w_in.shape[0] == 1
    assert (2 * d) % A_QK == 0 and COL_GM % d == 0
    assert seq % FFN_TM == 0 and seq % INPROJ_TM == 0 and seq % ATT_TQ == 0 and seq % MIX_T == 0
    assert ffn1_w_gate.shape[2] % FFN_TF == 0 and ffn1_w_gate.shape[2] // FFN_TF >= 2
    l = 0
    xr = x.reshape(seq, d)
    row = lambda v: v[l].reshape(1, -1)

    w_in_l, b_in_l = w_in[l], b_in[l]
    gate_end = GATE_OFFSET + N_GATES
    w_in_t = w_in_l.T
    b_main = jnp.concatenate([b_in_l[:GATE_OFFSET], b_in_l[gate_end:]]).reshape(1, -1)
    b_gate = jnp.pad(b_in_l[GATE_OFFSET:gate_end], (0, LANES - N_GATES)).reshape(1, -1)
    rope = _rope_tables(N_META + seq)
    rope_m, rope_x = rope[:N_META], rope[N_META:]

    hx, hmeta = _ffn(xr, row(ffn1_pre_g), row(ffn1_post_g), ffn1_w_gate, ffn1_w_up, ffn1_w_down,
                     extra=meta, layer=l, tm=FFN_TM, tf=FFN_TF)
    zx, gx, zm, gm = _inproj(hx, hmeta, row(mix_pre_g), w_in_t, b_main, b_gate, rope_x, rope_m,
                             tm=INPROJ_TM)

    lamv = jnp.concatenate([row(a_lambda_q1), row(a_lambda_k1), row(a_lambda_q2),
                            row(a_lambda_k2)], axis=0)
    ha, mw_b, aw_b, wo_b, wg2_b, wu2_b, wd2_b = _attn(
        zx, zm, lamv, row(a_norm_g),
        (m_w_branch, a_w_branch, w_out, ffn2_w_gate, ffn2_w_up, ffn2_w_down), layer=l, tq=ATT_TQ)
    h2 = _mixout(zx, gx, zm, gm, ha, hx, m_conv_w[l], row(m_conv_b), row(m_norm_g),
                 mw_b, aw_b, wo_b, row(mix_post_g), t=MIX_T)
    out = _ffn(h2, row(ffn2_pre_g), row(ffn2_post_g), wg2_b[None], wu2_b[None], wd2_b[None],
               layer=0, tm=FFN_TM, tf=FFN_TF_BF16)
    return out.reshape(batch, seq, d)
```
